```python
import math
import jax, jax.numpy as jnp
from jax import lax
import numpy as np


D_MODEL = 2048
BATCH = 2
SEQ = 8192
DEPTH = 2

N_HEADS = 8
HEAD_DIM = 64
ATTN_WIDTH = N_HEADS * 2 * HEAD_DIM
POOL_WINDOWS = (2, 4, 8, 16)
POOL_GROUP = 256
POOL_WIDTH = POOL_GROUP * len(POOL_WINDOWS)
MAX_WINDOW = max(POOL_WINDOWS)
N_BUCKETS = 32
MAX_DISTANCE = 128
BLOCK_Q = 128
D_FF_DENSE = 5632
N_EXPERTS = 8
TOP_K = 2
D_FF_EXPERT = 7168
ALPHA = (2.0 * DEPTH) ** 0.25
BETA = (8.0 * DEPTH) ** -0.25
LN_EPS = 1e-5
N_DENSE = (DEPTH + 1) // 2
N_MOE = DEPTH // 2
IN_COLS = (ATTN_WIDTH, ATTN_WIDTH, ATTN_WIDTH, POOL_WIDTH, D_MODEL, D_MODEL)
IN_WIDTH = sum(IN_COLS)
SPLIT_POINTS = tuple(int(s) for s in np.cumsum(IN_COLS)[:-1])

kernel_name = 'hybrid_diffattn_pool_moe_block'


def layer_norm(x, g, b):
    xf = x.astype(jnp.float32)
    mu = jnp.mean(xf, axis=-1, keepdims=True)
    var = jnp.mean(jnp.square(xf - mu), axis=-1, keepdims=True)
    return ((xf - mu) * lax.rsqrt(var + LN_EPS) * g.astype(jnp.float32) + b.astype(jnp.float32)).astype(x.dtype)


def t5_causal_bucket(q_pos, k_pos):
    n = jnp.maximum(q_pos[:, None] - k_pos[None, :], 0)
    max_exact = N_BUCKETS // 2
    large = max_exact + (jnp.log(jnp.maximum(n, 1).astype(jnp.float32) / max_exact)
                         / math.log(MAX_DISTANCE / max_exact) * (N_BUCKETS - max_exact)).astype(jnp.int32)
    large = jnp.minimum(large, N_BUCKETS - 1)
    return jnp.where(n < max_exact, n, large)


def diff_attention(q, k, v, rel_bias, lam):
    B, S = q.shape[0], q.shape[1]
    k_pos = jnp.arange(S)
    scale = HEAD_DIM ** -0.5

    def block(i):
        start = i * BLOCK_Q
        qb = lax.dynamic_slice_in_dim(q, start, BLOCK_Q, axis=1)
        q_pos = start + jnp.arange(BLOCK_Q)
        s = jnp.einsum('bqhcd,bkhcd->bchqk', qb, k).astype(jnp.float32) * scale
        bias = jnp.transpose(rel_bias[t5_causal_bucket(q_pos, k_pos)].astype(jnp.float32), (2, 0, 1))
        causal = k_pos[None, :] <= q_pos[:, None]
        s = jnp.where(causal, s + bias, -jnp.inf)
        p = jax.nn.softmax(s, axis=-1)
        a = p[:, 0] - lam * p[:, 1]
        return jnp.einsum('bhqk,bkhe->bqhe', a.astype(v.dtype), v)

    o = lax.map(block, jnp.arange(S // BLOCK_Q))
    return jnp.moveaxis(o, 0, 1).reshape(B, S, N_HEADS, 2 * HEAD_DIM)


def multiscale_pool(u, pool_w, pool_scale):
    B, S, _ = u.shape
    ug = u.astype(jnp.float32).reshape(B, S, len(POOL_WINDOWS), POOL_GROUP)
    c = jnp.cumsum(ug, axis=1)
    c = jnp.concatenate([jnp.zeros((B, MAX_WINDOW) + c.shape[2:], jnp.float32), c], axis=1)
    t = jnp.arange(S)
    means = []
    for g, w in enumerate(POOL_WINDOWS):
        win_sum = c[:, MAX_WINDOW:, g] - c[:, MAX_WINDOW - w:MAX_WINDOW - w + S, g]
        cnt = jnp.minimum(t + 1, w).astype(jnp.float32)
        means.append(win_sum / cnt[None, :, None])
    z = (jnp.stack(means, axis=2) - ug).astype(u.dtype)
    y = jnp.einsum('bsgc,gce->bsge', z, pool_w).reshape(B, S, POOL_WIDTH)
    return y * pool_scale


def hybrid_mixer(h, w_in, lam_vec, subln_w, pool_w, pool_scale, w_ba, w_bp, w_out, rel_bias, lambda_init):
    B, S, _ = h.shape
    proj = h @ w_in
    q, k, v, u, ga, gp = jnp.split(proj, SPLIT_POINTS, axis=-1)
    q = q.reshape(B, S, N_HEADS, 2, HEAD_DIM)
    k = k.reshape(B, S, N_HEADS, 2, HEAD_DIM)
    v = v.reshape(B, S, N_HEADS, 2 * HEAD_DIM)
    lv = lam_vec.astype(jnp.float32)
    lam = jnp.exp(jnp.sum(lv[0] * lv[1])) - jnp.exp(jnp.sum(lv[2] * lv[3])) + lambda_init
    o = diff_attention(q, k, v, rel_bias, lam).astype(jnp.float32)
    o = o * lax.rsqrt(jnp.mean(jnp.square(o), axis=-1, keepdims=True) + LN_EPS) * subln_w.astype(jnp.float32) * (1.0 - lambda_init)
    attn_out = o.reshape(B, S, ATTN_WIDTH).astype(h.dtype)
    pool_out = multiscale_pool(u, pool_w, pool_scale)
    merged = jax.nn.sigmoid(ga) * (attn_out @ w_ba) + jax.nn.sigmoid(gp) * (pool_out @ w_bp)
    return merged @ w_out


def dense_swiglu(h, wg, wu, wd):
    return (jax.nn.silu(h @ wg) * (h @ wu)) @ wd


def moe_swiglu(h, router_w, wg, wu, wd):
    B, S, D = h.shape
    t = h.reshape(B * S, D)
    logits = (t @ router_w).astype(jnp.float32)
    top_v, top_i = lax.top_k(logits, TOP_K)
    top_w = jax.nn.softmax(top_v, axis=-1)
    gates = jnp.sum(jax.nn.one_hot(top_i, N_EXPERTS, dtype=jnp.float32) * top_w[..., None], axis=1)
    y = jnp.zeros((B * S, D), jnp.float32)
    for e in range(N_EXPERTS):
        he = jax.nn.silu(t @ wg[e]) * (t @ wu[e])
        y = y + gates[:, e:e + 1] * (he @ wd[e]).astype(jnp.float32)
    return y.reshape(B, S, D).astype(h.dtype)


def setup_inputs(seed: int = 0) -> dict:
    key = jax.random.key(seed)
    ks = jax.random.split(key, 24)

    def nrm(k, shape, s):
        return jax.random.normal(k, shape, jnp.float32) * s

    return {
        'x': nrm(ks[0], (BATCH, SEQ, D_MODEL), 1.0),
        'w_in': nrm(ks[1], (DEPTH, D_MODEL, IN_WIDTH), D_MODEL ** -0.5),
        'lambdas': nrm(ks[2], (DEPTH, 4, HEAD_DIM), 0.1),
        'subln_w': 1.0 + nrm(ks[3], (DEPTH, 2 * HEAD_DIM), 0.02),
        'pool_w': nrm(ks[4], (DEPTH, len(POOL_WINDOWS), POOL_GROUP, POOL_GROUP), POOL_GROUP ** -0.5),
        'pool_scale': 1.0 + nrm(ks[5], (DEPTH, POOL_WIDTH), 0.02),
        'w_branch_attn': nrm(ks[6], (DEPTH, ATTN_WIDTH, D_MODEL), ATTN_WIDTH ** -0.5),
        'w_branch_pool': nrm(ks[7], (DEPTH, POOL_WIDTH, D_MODEL), POOL_WIDTH ** -0.5),
        'w_out': nrm(ks[8], (DEPTH, D_MODEL, D_MODEL), BETA * D_MODEL ** -0.5),
        'rel_bias': nrm(ks[9], (N_BUCKETS, N_HEADS), 0.5),
        'ln1_g': 1.0 + nrm(ks[10], (DEPTH, D_MODEL), 0.02),
        'ln1_b': nrm(ks[11], (DEPTH, D_MODEL), 0.02),
        'dense_w_gate': nrm(ks[12], (N_DENSE, D_MODEL, D_FF_DENSE), D_MODEL ** -0.5),
        'dense_w_up': nrm(ks[13], (N_DENSE, D_MODEL, D_FF_DENSE), D_MODEL ** -0.5),
        'dense_w_down': nrm(ks[14], (N_DENSE, D_FF_DENSE, D_MODEL), BETA * D_FF_DENSE ** -0.5),
        'router_w': nrm(ks[15], (N_MOE, D_MODEL, N_EXPERTS), D_MODEL ** -0.5),
        'moe_w_gate': nrm(ks[16], (N_MOE, N_EXPERTS, D_MODEL, D_FF_EXPERT), D_MODEL ** -0.5),
        'moe_w_up': nrm(ks[17], (N_MOE, N_EXPERTS, D_MODEL, D_FF_EXPERT), D_MODEL ** -0.5),
        'moe_w_down': nrm(ks[18], (N_MOE, N_EXPERTS, D_FF_EXPERT, D_MODEL), BETA * D_FF_EXPERT ** -0.5),
        'ln2_g': 1.0 + nrm(ks[19], (DEPTH, D_MODEL), 0.02),
        'ln2_b': nrm(ks[20], (DEPTH, D_MODEL), 0.02),
    }


def reference(x, w_in, lambdas, subln_w, pool_w, pool_scale, w_branch_attn, w_branch_pool, w_out, rel_bias,
              ln1_g, ln1_b, dense_w_gate, dense_w_up, dense_w_down, router_w, moe_w_gate, moe_w_up, moe_w_down,
              ln2_g, ln2_b):
    for l in range(DEPTH):
        lambda_init = 0.8 - 0.6 * math.exp(-0.3 * l)
        mix = hybrid_mixer(x, w_in[l], lambdas[l], subln_w[l], pool_w[l], pool_scale[l],
                           w_branch_attn[l], w_branch_pool[l], w_out[l], rel_bias, lambda_init)
        x = layer_norm(ALPHA * x + mix, ln1_g[l], ln1_b[l])
        if l % 2 == 0:
            f = dense_swiglu(x, dense_w_gate[l // 2], dense_w_up[l // 2], dense_w_down[l // 2])
        else:
            f = moe_swiglu(x, router_w[l // 2], moe_w_gate[l // 2], moe_w_up[l // 2], moe_w_down[l // 2])
        x = layer_norm(ALPHA * x + f, ln2_g[l], ln2_b[l])
    return x
```

```python
import functools
import math

import jax
import jax.numpy as jnp
from jax import lax
from jax.experimental import pallas as pl
from jax.experimental.pallas import tpu as pltpu

D_MODEL = 2048
DEPTH = 2
N_HEADS = 8
HEAD_DIM = 64
HEAD_WIDTH = 2 * HEAD_DIM
ATTN_WIDTH = N_HEADS * HEAD_WIDTH
POOL_WINDOWS = (2, 4, 8, 16)
POOL_GROUP = 256
POOL_WIDTH = POOL_GROUP * len(POOL_WINDOWS)
MAX_WINDOW = max(POOL_WINDOWS)
N_BUCKETS = 32
MAX_DISTANCE = 128
N_EXPERTS = 8
TOP_K = 2
ALPHA = (2.0 * DEPTH) ** 0.25
LN_EPS = 1e-5
QK_SCALE = HEAD_DIM ** -0.5

LANES = 128
MASK_VALUE = -1e30
VMEM_LIMIT = 56 * 1024 * 1024

PROJ_TM = 1024
PROJ_TN = 1024
ATTN_TQ = 512
ATTN_TK = 512
TAIL_TM = 256
FFN_TM = 512
FFN_TF = 512
ROUTER_TM = 512
DISPATCH_TM = 512
MOE_TM = 512
COMBINE_TM = 256

F32 = jnp.float32
BF16 = jnp.bfloat16


def _params(*sem):
    return pltpu.CompilerParams(dimension_semantics=sem, vmem_limit_bytes=VMEM_LIMIT)


def _layer_norm(r, g, b):
    mu = jnp.mean(r, axis=-1, keepdims=True)
    c = r - mu
    var = jnp.mean(c * c, axis=-1, keepdims=True)
    return c * lax.rsqrt(var + LN_EPS) * g + b


def _proj_kernel(x_ref, w_ref, o_ref):
    x = x_ref[...].astype(BF16)
    o_ref[...] = jnp.dot(x, w_ref[...], preferred_element_type=F32).astype(o_ref.dtype)


def _proj(x, w, col_block0, n_col_blocks, out_dtype):
    t, k = x.shape
    tm, tn = PROJ_TM, PROJ_TN
    return pl.pallas_call(
        _proj_kernel,
        grid=(t // tm, n_col_blocks),
        in_specs=[
            pl.BlockSpec((tm, k), lambda i, j: (i, 0)),
            pl.BlockSpec((k, tn), lambda i, j: (0, j + col_block0)),
        ],
        out_specs=pl.BlockSpec((tm, tn), lambda i, j: (i, j)),
        out_shape=jax.ShapeDtypeStruct((t, n_col_blocks * tn), out_dtype),
        compiler_params=_params("parallel", "arbitrary"),
        name="in_proj",
    )(x, w)


def _bias_kernel(rb_ref, o_ref):
    h = pl.program_id(0)
    tq, tk = o_ref.shape[2], o_ref.shape[3]
    row = lax.broadcasted_iota(jnp.int32, (tq, tk), 0)
    col = lax.broadcasted_iota(jnp.int32, (tq, tk), 1)
    max_exact = N_BUCKETS // 2
    far = rb_ref[h * N_BUCKETS + N_BUCKETS - 1]
    for blk in range(2):
        dist = row - col + blk * tk
        n = jnp.maximum(dist, 0)
        large = max_exact + (
            jnp.log(jnp.maximum(n, 1).astype(F32) / max_exact) / math.log(MAX_DISTANCE / max_exact)
            * (N_BUCKETS - max_exact)
        ).astype(jnp.int32)
        large = jnp.minimum(large, N_BUCKETS - 1)
        bucket = jnp.where(n < max_exact, n, large)
        bias = jnp.zeros((tq, tk), F32)
        for b in range(N_BUCKETS):
            bias = jnp.where(bucket == b, rb_ref[h * N_BUCKETS + b], bias)
        bias = bias - far
        if blk == 0:
            bias = jnp.where(dist >= 0, bias, MASK_VALUE)
        o_ref[0, blk] = bias


def _bias_tiles(rel_bias):
    assert ATTN_TK >= MAX_DISTANCE
    rb = jnp.transpose(rel_bias).reshape(-1)
    return pl.pallas_call(
        _bias_kernel,
        grid_spec=pltpu.PrefetchScalarGridSpec(
            num_scalar_prefetch=1,
            grid=(N_HEADS,),
            in_specs=[],
            out_specs=pl.BlockSpec((1, 2, ATTN_TQ, ATTN_TK), lambda h, rb: (h, 0, 0, 0)),
        ),
        out_shape=jax.ShapeDtypeStruct((N_HEADS, 2, ATTN_TQ, ATTN_TK), F32),
        compiler_params=_params("arbitrary"),
        name="bias_tiles",
    )(rb)


def _attn_kernel(q_ref, k_ref, v_ref, bias_ref, lam_ref, sub_ref, o_ref, m_ref, l_ref, acc_ref, *, lambda_init):
    qi = pl.program_id(2)
    tq = q_ref.shape[1]
    tk = bias_ref.shape[3]
    lane = lax.broadcasted_iota(jnp.int32, (tq, HEAD_WIDTH), 1)
    q = q_ref[0] * QK_SCALE
    zero = jnp.zeros_like(q)
    qc = (jnp.where(lane < HEAD_DIM, q, zero), jnp.where(lane >= HEAD_DIM, q, zero))

    m_ref[...] = jnp.full(m_ref.shape, MASK_VALUE, F32)
    l_ref[...] = jnp.zeros(l_ref.shape, F32)
    acc_ref[...] = jnp.zeros(acc_ref.shape, F32)

    def process(kstart, bias):
        kk = k_ref[0, pl.ds(kstart, tk), :]
        vv = v_ref[0, pl.ds(kstart, tk), :]
        for c in range(2):
            s = lax.dot_general(qc[c], kk, (((1,), (1,)), ((), ())), preferred_element_type=F32)
            if bias is not None:
                s = s + bias
            m_prev = m_ref[c]
            m_new = jnp.maximum(m_prev, jnp.max(s, axis=-1, keepdims=True))
            alpha = jnp.exp(m_prev - m_new)
            p = jnp.exp(s - m_new)
            l_ref[c] = alpha * l_ref[c] + jnp.sum(p, axis=-1, keepdims=True)
            acc_ref[c] = alpha * acc_ref[c] + jnp.dot(p.astype(BF16), vv, preferred_element_type=F32)
            m_ref[c] = m_new

    def far_body(ki, carry):
        process(pl.multiple_of(ki * tk, tk), None)
        return carry

    lax.fori_loop(0, jnp.maximum(qi - 1, 0), far_body, 0)

    @pl.when(qi >= 1)
    def _():
        process(pl.multiple_of((qi - 1) * tk, tk), bias_ref[0, 1])

    process(pl.multiple_of(qi * tk, tk), bias_ref[0, 0])

    lv = lam_ref[...]
    lam = (
        jnp.exp(jnp.sum(lv[0:1] * lv[1:2], axis=-1, keepdims=True))
        - jnp.exp(jnp.sum(lv[2:3] * lv[3:4], axis=-1, keepdims=True))
        + lambda_init
    )
    o = acc_ref[0] / l_ref[0] - lam * (acc_ref[1] / l_ref[1])
    o = o * lax.rsqrt(jnp.mean(o * o, axis=-1, keepdims=True) + LN_EPS) * sub_ref[...] * (1.0 - lambda_init)
    o_ref[0] = o.astype(o_ref.dtype)


def _attention(qkv, bias_tiles, lam_vec, subln_w, lambda_init):
    b, s, _ = qkv.shape
    tq = ATTN_TQ
    assert ATTN_TQ == ATTN_TK and s % tq == 0
    kernel = functools.partial(_attn_kernel, lambda_init=lambda_init)
    return pl.pallas_call(
        kernel,
        grid=(b, N_HEADS, s // tq),
        in_specs=[
            pl.BlockSpec((1, tq, HEAD_WIDTH), lambda bi, h, qi: (bi, qi, h)),
            pl.BlockSpec((1, s, HEAD_WIDTH), lambda bi, h, qi: (bi, 0, N_HEADS + h)),
            pl.BlockSpec((1, s, HEAD_WIDTH), lambda bi, h, qi: (bi, 0, 2 * N_HEADS + h)),
            pl.BlockSpec((1, 2, tq, ATTN_TK), lambda bi, h, qi: (h, 0, 0, 0)),
            pl.BlockSpec((4, HEAD_DIM), lambda bi, h, qi: (0, 0)),
            pl.BlockSpec((1, HEAD_WIDTH), lambda bi, h, qi: (0, 0)),
        ],
        out_specs=pl.BlockSpec((1, tq, HEAD_WIDTH), lambda bi, h, qi: (bi, qi, h)),
        out_shape=jax.ShapeDtypeStruct((b, s, ATTN_WIDTH), BF16),
        scratch_shapes=[
            pltpu.VMEM((2, tq, 1), F32),
            pltpu.VMEM((2, tq, 1), F32),
            pltpu.VMEM((2, tq, HEAD_WIDTH), F32),
        ],
        compiler_params=_params("parallel", "parallel", "arbitrary"),
        name="diff_attention",
    )(qkv, qkv, qkv, bias_tiles, lam_vec, subln_w.reshape(1, HEAD_WIDTH))


def _tail_kernel(attn_ref, u_ref, halo_ref, ga_ref, gp_ref, x_ref, pw_ref, ps_ref, wba_ref, wbp_ref, wout_ref,
                 g_ref, b_ref, of_ref, ob_ref, ubuf, *, tiles_per_seq):
    i = pl.program_id(0)
    tm = u_ref.shape[0]
    tile_in_seq = i % tiles_per_seq
    halo = jnp.where(tile_in_seq == 0, 0.0, halo_ref[...])
    ubuf[0:MAX_WINDOW, :] = halo
    ubuf[MAX_WINDOW:MAX_WINDOW + tm, :] = u_ref[...]
    t = tile_in_seq * tm + lax.broadcasted_iota(jnp.int32, (tm, 1), 0)

    pooled = []
    for g, w in enumerate(POOL_WINDOWS):
        cs = slice(g * POOL_GROUP, (g + 1) * POOL_GROUP)
        cur = ubuf[MAX_WINDOW:MAX_WINDOW + tm, cs]
        win_sum = cur
        for j in range(1, w):
            win_sum = win_sum + ubuf[MAX_WINDOW - j:MAX_WINDOW - j + tm, cs]
        cnt = jnp.minimum(t + 1, w).astype(F32)
        z = (win_sum / cnt - cur).astype(BF16)
        y = jnp.dot(z, pw_ref[g], preferred_element_type=F32)
        pooled.append((y * ps_ref[:, cs]).astype(BF16))
    pool = jnp.concatenate(pooled, axis=-1)

    a = jnp.dot(attn_ref[...], wba_ref[...], preferred_element_type=F32)
    p = jnp.dot(pool, wbp_ref[...], preferred_element_type=F32)
    merged = jax.nn.sigmoid(ga_ref[...]) * a + jax.nn.sigmoid(gp_ref[...]) * p
    mix = jnp.dot(merged.astype(BF16), wout_ref[...], preferred_element_type=F32)
    out = _layer_norm(ALPHA * x_ref[...] + mix, g_ref[...], b_ref[...])
    of_ref[...] = out
    ob_ref[...] = out.astype(BF16)


def _resident(shape):
    zeros = (0,) * len(shape)
    return pl.BlockSpec(shape, lambda i: zeros, pipeline_mode=pl.Buffered(1))


def _mixer_tail(attn, u, gates, x, pool_w, pool_scale, w_ba, w_bp, w_out, ln_g, ln_b, seq):
    t = x.shape[0]
    tm = TAIL_TM
    assert seq % tm == 0 and tm % MAX_WINDOW == 0
    halo_blocks = tm // MAX_WINDOW
    kernel = functools.partial(_tail_kernel, tiles_per_seq=seq // tm)
    return pl.pallas_call(
        kernel,
        grid=(t // tm,),
        in_specs=[
            pl.BlockSpec((tm, ATTN_WIDTH), lambda i: (i, 0)),
            pl.BlockSpec((tm, POOL_WIDTH), lambda i: (i, 0)),
            pl.BlockSpec((MAX_WINDOW, POOL_WIDTH), lambda i: (jnp.maximum(i * halo_blocks - 1, 0), 0)),
            pl.BlockSpec((tm, D_MODEL), lambda i: (i, 0)),
            pl.BlockSpec((tm, D_MODEL), lambda i: (i, 1)),
            pl.BlockSpec((tm, D_MODEL), lambda i: (i, 0)),
            _resident(pool_w.shape),
            _resident((1, POOL_WIDTH)),
            _resident(w_ba.shape),
            _resident(w_bp.shape),
            _resident(w_out.shape),
            _resident((1, D_MODEL)),
            _resident((1, D_MODEL)),
        ],
        out_specs=[
            pl.BlockSpec((tm, D_MODEL), lambda i: (i, 0)),
            pl.BlockSpec((tm, D_MODEL), lambda i: (i, 0)),
        ],
        out_shape=[
            jax.ShapeDtypeStruct((t, D_MODEL), F32),
            jax.ShapeDtypeStruct((t, D_MODEL), BF16),
        ],
        scratch_shapes=[pltpu.VMEM((MAX_WINDOW + tm, POOL_WIDTH), F32)],
        compiler_params=_params("parallel"),
        name="mixer_tail",
    )(attn, u, u, gates, gates, x, pool_w, pool_scale.reshape(1, POOL_WIDTH), w_ba, w_bp, w_out,
      ln_g.reshape(1, D_MODEL), ln_b.reshape(1, D_MODEL))


def _swiglu(hb, wg, wu):
    g = jnp.dot(hb, wg, preferred_element_type=F32)
    u = jnp.dot(hb, wu, preferred_element_type=F32)
    return (g * jax.nn.sigmoid(g) * u).astype(BF16)


def _dense_ffn_kernel(hb_ref, hf_ref, wg_ref, wu_ref, wd_ref, g_ref, b_ref, of_ref, ob_ref, acc_ref):
    f = pl.program_id(1)

    @pl.when(f == 0)
    def _():
        acc_ref[...] = jnp.zeros(acc_ref.shape, F32)

    a = _swiglu(hb_ref[...], wg_ref[...], wu_ref[...])
    acc_ref[...] += jnp.dot(a, wd_ref[...], preferred_element_type=F32)

    @pl.when(f == pl.num_programs(1) - 1)
    def _():
        out = _layer_norm(ALPHA * hf_ref[...] + acc_ref[...], g_ref[...], b_ref[...])
        of_ref[...] = out
        ob_ref[...] = out.astype(BF16)


def _dense_ffn(hb, hf, wg, wu, wd, ln_g, ln_b):
    t = hb.shape[0]
    d_ff = wg.shape[1]
    tm, tf = FFN_TM, FFN_TF
    assert d_ff % tf == 0
    return pl.pallas_call(
        _dense_ffn_kernel,
        grid=(t // tm, d_ff // tf),
        in_specs=[
            pl.BlockSpec((tm, D_MODEL), lambda i, f: (i, 0)),
            pl.BlockSpec((tm, D_MODEL), lambda i, f: (i, 0)),
            pl.BlockSpec((D_MODEL, tf), lambda i, f: (0, f)),
            pl.BlockSpec((D_MODEL, tf), lambda i, f: (0, f)),
            pl.BlockSpec((tf, D_MODEL), lambda i, f: (f, 0)),
            pl.BlockSpec((1, D_MODEL), lambda i, f: (0, 0)),
            pl.BlockSpec((1, D_MODEL), lambda i, f: (0, 0)),
        ],
        out_specs=[
            pl.BlockSpec((tm, D_MODEL), lambda i, f: (i, 0)),
            pl.BlockSpec((tm, D_MODEL), lambda i, f: (i, 0)),
        ],
        out_shape=[
            jax.ShapeDtypeStruct((t, D_MODEL), F32),
            jax.ShapeDtypeStruct((t, D_MODEL), BF16),
        ],
        scratch_shapes=[pltpu.VMEM((tm, D_MODEL), F32)],
        compiler_params=_params("parallel", "arbitrary"),
        name="dense_ffn",
    )(hb, hf, wg, wu, wd, ln_g.reshape(1, D_MODEL), ln_b.reshape(1, D_MODEL))


ROUTE_W0, ROUTE_W1, ROUTE_E0, ROUTE_E1, ROUTE_R0, ROUTE_R1 = range(6)


def _router_kernel(h_ref, rw_ref, route_ref, counts_ref, carry_ref):
    i = pl.program_id(0)
    tm = h_ref.shape[0]

    @pl.when(i == 0)
    def _():
        carry_ref[...] = jnp.zeros(carry_ref.shape, F32)

    logits = jnp.dot(h_ref[...], rw_ref[...], preferred_element_type=F32, precision=lax.Precision.HIGHEST)
    lane = lax.broadcasted_iota(jnp.int32, (tm, LANES), 1)
    lg = jnp.where(lane < N_EXPERTS, logits, -jnp.inf)
    m1 = jnp.max(lg, axis=-1, keepdims=True)
    e1 = jnp.min(jnp.where(lg == m1, lane, LANES), axis=-1, keepdims=True)
    lg2 = jnp.where(lane == e1, -jnp.inf, lg)
    m2 = jnp.max(lg2, axis=-1, keepdims=True)
    e2 = jnp.min(jnp.where(lg2 == m2, lane, LANES), axis=-1, keepdims=True)
    x2 = jnp.exp(m2 - m1)
    w1 = 1.0 / (1.0 + x2)
    w2 = x2 / (1.0 + x2)

    sel1 = lane == e1
    sel2 = lane == e2
    mask = jnp.where(sel1 | sel2, 1.0, 0.0)
    r = lax.broadcasted_iota(jnp.int32, (tm, tm), 0)
    c = lax.broadcasted_iota(jnp.int32, (tm, tm), 1)
    tri = jnp.where(r >= c, 1.0, 0.0).astype(BF16)
    incl = jnp.dot(tri, mask.astype(BF16), preferred_element_type=F32)
    rank = incl - mask + carry_ref[...]
    total = carry_ref[...] + jnp.sum(mask, axis=0, keepdims=True)
    carry_ref[...] = total
    counts_ref[...] = total

    r1 = jnp.sum(jnp.where(sel1, rank, 0.0), axis=-1, keepdims=True)
    r2 = jnp.sum(jnp.where(sel2, rank, 0.0), axis=-1, keepdims=True)
    cols = (w1, w2, e1.astype(F32), e2.astype(F32), r1, r2)
    route = jnp.zeros((tm, LANES), F32)
    for idx, val in enumerate(cols):
        route = jnp.where(lane == idx, val, route)
    route_ref[...] = route


def _router(hf, router_w):
    t = hf.shape[0]
    tm = ROUTER_TM
    rw = jnp.pad(router_w, ((0, 0), (0, LANES - N_EXPERTS)))
    return pl.pallas_call(
        _router_kernel,
        grid=(t // tm,),
        in_specs=[
            pl.BlockSpec((tm, D_MODEL), lambda i: (i, 0)),
            pl.BlockSpec((D_MODEL, LANES), lambda i: (0, 0)),
        ],
        out_specs=[
            pl.BlockSpec((tm, LANES), lambda i: (i, 0)),
            pl.BlockSpec((1, LANES), lambda i: (0, 0)),
        ],
        out_shape=[
            jax.ShapeDtypeStruct((t, LANES), F32),
            jax.ShapeDtypeStruct((1, LANES), F32),
        ],
        scratch_shapes=[pltpu.VMEM((1, LANES), F32)],
        compiler_params=_params("arbitrary"),
        name="moe_router",
    )(hf, rw)


def _dispatch_kernel(dest_ref, h_hbm, xs_in_hbm, xs_hbm, sem):
    del xs_in_hbm
    i = pl.program_id(0)
    n = dest_ref.shape[2] // TOP_K
    base = i * n

    def row_copy(r, k):
        return pltpu.make_async_copy(
            h_hbm.at[pl.ds(base + r, 1)], xs_hbm.at[pl.ds(dest_ref[0, 0, TOP_K * r + k], 1)], sem)

    def start(r, carry):
        for k in range(TOP_K):
            row_copy(r, k).start()
        return carry

    def wait(r, carry):
        for k in range(TOP_K):
            row_copy(r, k).wait()
        return carry

    lax.fori_loop(0, n, start, 0)
    lax.fori_loop(0, n, wait, 0)


def _dispatch(hf, dest, n_rows):
    t = hf.shape[0]
    tm = DISPATCH_TM
    xs0 = jnp.zeros((n_rows, D_MODEL), F32)
    return pl.pallas_call(
        _dispatch_kernel,
        grid=(t // tm,),
        in_specs=[
            pl.BlockSpec((1, 1, TOP_K * tm), lambda i: (i, 0, 0), memory_space=pltpu.SMEM),
            pl.BlockSpec(memory_space=pl.ANY),
            pl.BlockSpec(memory_space=pl.ANY),
        ],
        out_specs=pl.BlockSpec(memory_space=pl.ANY),
        out_shape=jax.ShapeDtypeStruct((n_rows, D_MODEL), F32),
        scratch_shapes=[pltpu.SemaphoreType.DMA(())],
        input_output_aliases={2: 0},
        compiler_params=_params("arbitrary"),
        name="moe_dispatch",
    )(dest.reshape(t // tm, 1, TOP_K * tm), hf, xs0)


def _moe_ffn_kernel(te_ref, ok_ref, xs_ref, wg_ref, wu_ref, wd_ref, ys_ref, xb_ref, acc_ref):
    i = pl.program_id(0)
    f = pl.program_id(1)
    ok = ok_ref[i] == 1

    @pl.when(ok & (f == 0))
    def _():
        xb_ref[...] = xs_ref[...].astype(BF16)
        acc_ref[...] = jnp.zeros(acc_ref.shape, F32)

    @pl.when(ok)
    def _():
        a = _swiglu(xb_ref[...], wg_ref[...], wu_ref[...])
        acc_ref[...] += jnp.dot(a, wd_ref[...], preferred_element_type=F32)

    @pl.when(f == pl.num_programs(1) - 1)
    def _():
        ys_ref[...] = jnp.where(ok, acc_ref[...], 0.0)


def _moe_ffn(xs, tile_expert, tile_ok, wg, wu, wd):
    n_rows = xs.shape[0]
    d_ff = wg.shape[2]
    tm, tf = MOE_TM, FFN_TF
    n_f = d_ff // tf
    n_tiles = n_rows // tm

    def f_idx(i, f, ok):
        return jnp.where(ok[i] == 1, f, n_f - 1)

    return pl.pallas_call(
        _moe_ffn_kernel,
        grid_spec=pltpu.PrefetchScalarGridSpec(
            num_scalar_prefetch=2,
            grid=(n_tiles, n_f),
            in_specs=[
                pl.BlockSpec((tm, D_MODEL), lambda i, f, te, ok: (i, 0)),
                pl.BlockSpec((None, D_MODEL, tf), lambda i, f, te, ok: (te[i], 0, f_idx(i, f, ok))),
                pl.BlockSpec((None, D_MODEL, tf), lambda i, f, te, ok: (te[i], 0, f_idx(i, f, ok))),
                pl.BlockSpec((None, tf, D_MODEL), lambda i, f, te, ok: (te[i], f_idx(i, f, ok), 0)),
            ],
            out_specs=pl.BlockSpec((tm, D_MODEL), lambda i, f, te, ok: (i, 0)),
            scratch_shapes=[pltpu.VMEM((tm, D_MODEL), BF16), pltpu.VMEM((tm, D_MODEL), F32)],
        ),
        out_shape=jax.ShapeDtypeStruct((n_rows, D_MODEL), F32),
        compiler_params=_params("arbitrary", "arbitrary"),
        name="moe_ffn",
    )(tile_expert, tile_ok, xs, wg, wu, wd)


def _combine_kernel(dest_ref, h_ref, route_ref, ys_hbm, g_ref, b_ref, o_ref, y_ref, sem):
    tm = h_ref.shape[0]

    def row_copy(r, k):
        return pltpu.make_async_copy(
            ys_hbm.at[pl.ds(dest_ref[0, 0, TOP_K * r + k], 1)], y_ref.at[k, pl.ds(r, 1)], sem)

    def start(r, carry):
        for k in range(TOP_K):
            row_copy(r, k).start()
        return carry

    def wait(r, carry):
        for k in range(TOP_K):
            row_copy(r, k).wait()
        return carry

    lax.fori_loop(0, tm, start, 0)
    lax.fori_loop(0, tm, wait, 0)

    route = route_ref[...]
    w1 = route[:, ROUTE_W0:ROUTE_W0 + 1]
    w2 = route[:, ROUTE_W1:ROUTE_W1 + 1]
    y = w1 * y_ref[0] + w2 * y_ref[1]
    o_ref[...] = _layer_norm(ALPHA * h_ref[...] + y, g_ref[...], b_ref[...])


def _combine(hf, route, dest, ys, ln_g, ln_b):
    t = hf.shape[0]
    tm = COMBINE_TM
    return pl.pallas_call(
        _combine_kernel,
        grid=(t // tm,),
        in_specs=[
            pl.BlockSpec((1, 1, TOP_K * tm), lambda i: (i, 0, 0), memory_space=pltpu.SMEM),
            pl.BlockSpec((tm, D_MODEL), lambda i: (i, 0)),
            pl.BlockSpec((tm, LANES), lambda i: (i, 0)),
            pl.BlockSpec(memory_space=pl.ANY),
            pl.BlockSpec((1, D_MODEL), lambda i: (0, 0)),
            pl.BlockSpec((1, D_MODEL), lambda i: (0, 0)),
        ],
        out_specs=pl.BlockSpec((tm, D_MODEL), lambda i: (i, 0)),
        out_shape=jax.ShapeDtypeStruct((t, D_MODEL), F32),
        scratch_shapes=[pltpu.VMEM((TOP_K, tm, D_MODEL), F32), pltpu.SemaphoreType.DMA(())],
        compiler_params=_params("arbitrary"),
        name="moe_combine",
    )(dest.reshape(t // tm, 1, TOP_K * tm), hf, route, ys, ln_g.reshape(1, D_MODEL), ln_b.reshape(1, D_MODEL))


def _moe(hf, router_w, wg, wu, wd, ln_g, ln_b):
    t = hf.shape[0]
    tm = MOE_TM
    n_tiles = (TOP_K * t) // tm + N_EXPERTS
    route, counts = _router(hf, router_w)

    counts = counts[0, :N_EXPERTS].astype(jnp.int32)
    tiles_per_expert = (counts + tm - 1) // tm
    tile_end = jnp.cumsum(tiles_per_expert)
    group_start = (tile_end - tiles_per_expert) * tm
    tile_ids = jnp.arange(n_tiles, dtype=jnp.int32)
    n_used = tile_end[-1]
    last_used = jnp.maximum(n_used - 1, 0)
    expert_of_tile = jnp.sum(tile_ids[:, None] >= tile_end[None, :], axis=1).astype(jnp.int32)
    tile_ok = (tile_ids < n_used).astype(jnp.int32)
    tile_expert = jnp.where(tile_ok == 1, expert_of_tile, expert_of_tile[last_used])

    experts = route[:, ROUTE_E0:ROUTE_E1 + 1].astype(jnp.int32)
    ranks = route[:, ROUTE_R0:ROUTE_R1 + 1].astype(jnp.int32)
    dest = (group_start[experts] + ranks).reshape(-1)

    xs = _dispatch(hf, dest, n_tiles * tm)
    ys = _moe_ffn(xs, tile_expert, tile_ok, wg, wu, wd)
    return _combine(hf, route, dest, ys, ln_g, ln_b)


def kernel(x, w_in, lambdas, subln_w, pool_w, pool_scale, w_branch_attn, w_branch_pool, w_out, rel_bias,
           ln1_g, ln1_b, dense_w_gate, dense_w_up, dense_w_down, router_w, moe_w_gate, moe_w_up, moe_w_down,
           ln2_g, ln2_b):
    b, s, d = x.shape
    t = b * s
    xf = x.reshape(t, d)
    xin = xf
    bias_tiles = _bias_tiles(rel_bias)
    qkv_blocks = 3 * ATTN_WIDTH // PROJ_TN
    u_blocks = POOL_WIDTH // PROJ_TN
    gate_blocks = 2 * D_MODEL // PROJ_TN
    for l in range(DEPTH):
        lambda_init = 0.8 - 0.6 * math.exp(-0.3 * l)
        w_in_l = w_in[l].astype(BF16)
        qkv = _proj(xin, w_in_l, 0, qkv_blocks, BF16)
        u = _proj(xin, w_in_l, qkv_blocks, u_blocks, F32)
        gates = _proj(xin, w_in_l, qkv_blocks + u_blocks, gate_blocks, F32)
        attn = _attention(qkv.reshape(b, s, 3 * ATTN_WIDTH), bias_tiles, lambdas[l], subln_w[l], lambda_init)
        hf, hb = _mixer_tail(attn.reshape(t, ATTN_WIDTH), u, gates, xf, pool_w[l].astype(BF16), pool_scale[l],
                             w_branch_attn[l].astype(BF16), w_branch_pool[l].astype(BF16), w_out[l].astype(BF16),
                             ln1_g[l], ln1_b[l], s)
        if l % 2 == 0:
            xf, xin = _dense_ffn(hb, hf, dense_w_gate[l // 2].astype(BF16), dense_w_up[l // 2].astype(BF16),
                                 dense_w_down[l // 2].astype(BF16), ln2_g[l], ln2_b[l])
        else:
            xf = _moe(hf, router_w[l // 2], moe_w_gate[l // 2].astype(BF16), moe_w_up[l // 2].astype(BF16),
                      moe_w_down[l // 2].astype(BF16), ln2_g[l], ln2_b[l])
            xin = xf
    return xf.reshape(b, s, d)
```

```python
import functools
import math

import jax
import jax.numpy as jnp
from jax import lax
from jax.experimental import pallas as pl
from jax.experimental.pallas import tpu as pltpu

D_MODEL = 2048
DEPTH = 2
N_HEADS = 8
HEAD_DIM = 64
HEAD_WIDTH = 2 * HEAD_DIM
ATTN_WIDTH = N_HEADS * HEAD_WIDTH
POOL_WINDOWS = (2, 4, 8, 16)
POOL_GROUP = 256
POOL_WIDTH = POOL_GROUP * len(POOL_WINDOWS)
MAX_WINDOW = max(POOL_WINDOWS)
N_BUCKETS = 32
MAX_DISTANCE = 128
N_EXPERTS = 8
TOP_K = 2
ALPHA = (2.0 * DEPTH) ** 0.25
LN_EPS = 1e-5
QK_SCALE = HEAD_DIM ** -0.5

LANES = 128
MASK_VALUE = -1e30
VMEM_LIMIT = 56 * 1024 * 1024

PROJ_TM = 1024
PROJ_TN = 1024
ATTN_TQ = 512
ATTN_TK = 512
TAIL_TM = 256
FFN_TM = 512
FFN_TF = 512
ROUTER_TM = 512
DISPATCH_TM = 512
MOE_TM = 512
COMBINE_TM = 256

F32 = jnp.float32
BF16 = jnp.bfloat16


def _params(*sem):
    return pltpu.CompilerParams(dimension_semantics=sem, vmem_limit_bytes=VMEM_LIMIT)


def _layer_norm(r, g, b):
    mu = jnp.mean(r, axis=-1, keepdims=True)
    c = r - mu
    var = jnp.mean(c * c, axis=-1, keepdims=True)
    return c * lax.rsqrt(var + LN_EPS) * g + b


def _proj_kernel(x_ref, w_ref, o_ref):
    x = x_ref[...].astype(BF16)
    o_ref[...] = jnp.dot(x, w_ref[...], preferred_element_type=F32).astype(o_ref.dtype)


def _proj(x, w, col_block0, n_col_blocks, out_dtype):
    t, k = x.shape
    tm, tn = PROJ_TM, PROJ_TN
    return pl.pallas_call(
        _proj_kernel,
        grid=(t // tm, n_col_blocks),
        in_specs=[
            pl.BlockSpec((tm, k), lambda i, j: (i, 0)),
            pl.BlockSpec((k, tn), lambda i, j: (0, j + col_block0)),
        ],
        out_specs=pl.BlockSpec((tm, tn), lambda i, j: (i, j)),
        out_shape=jax.ShapeDtypeStruct((t, n_col_blocks * tn), out_dtype),
        compiler_params=_params("parallel", "arbitrary"),
        name="in_proj",
    )(x, w)


def _bias_kernel(rb_ref, o_ref):
    h = pl.program_id(0)
    tk, tq = o_ref.shape[2], o_ref.shape[3]
    kpos = lax.broadcasted_iota(jnp.int32, (tk, tq), 0)
    qpos = lax.broadcasted_iota(jnp.int32, (tk, tq), 1)
    max_exact = N_BUCKETS // 2
    far = rb_ref[h * N_BUCKETS + N_BUCKETS - 1]
    for blk in range(2):
        dist = qpos - kpos + blk * tk
        n = jnp.maximum(dist, 0)
        large = max_exact + (
            jnp.log(jnp.maximum(n, 1).astype(F32) / max_exact) / math.log(MAX_DISTANCE / max_exact)
            * (N_BUCKETS - max_exact)
        ).astype(jnp.int32)
        large = jnp.minimum(large, N_BUCKETS - 1)
        bucket = jnp.where(n < max_exact, n, large)
        bias = jnp.zeros((tk, tq), F32)
        for b in range(N_BUCKETS):
            bias = jnp.where(bucket == b, rb_ref[h * N_BUCKETS + b], bias)
        bias = bias - far
        if blk == 0:
            bias = jnp.where(dist >= 0, bias, MASK_VALUE)
        o_ref[0, blk] = bias


def _bias_tiles(rel_bias):
    assert ATTN_TK >= MAX_DISTANCE
    rb = jnp.transpose(rel_bias).reshape(-1)
    return pl.pallas_call(
        _bias_kernel,
        grid_spec=pltpu.PrefetchScalarGridSpec(
            num_scalar_prefetch=1,
            grid=(N_HEADS,),
            in_specs=[],
            out_specs=pl.BlockSpec((1, 2, ATTN_TK, ATTN_TQ), lambda h, rb: (h, 0, 0, 0)),
        ),
        out_shape=jax.ShapeDtypeStruct((N_HEADS, 2, ATTN_TK, ATTN_TQ), F32),
        compiler_params=_params("arbitrary"),
        name="bias_tiles",
    )(rb)


def _attn_kernel(q_ref, k_ref, v_ref, bias_ref, lam_ref, sub_ref, o_ref, qt_ref, vt_ref, m_ref, l_ref, acc_ref,
                 s0_ref, *, lambda_init):
    qi = pl.program_id(2)
    tq = q_ref.shape[1]
    n_kb, _, tk = vt_ref.shape

    @pl.when(qi == 0)
    def _():
        for kb in range(n_kb):
            vt_ref[kb] = v_ref[0, kb * tk:(kb + 1) * tk, :].astype(F32).T.astype(BF16)

    qt = (q_ref[0] * QK_SCALE).astype(F32).T
    dim = lax.broadcasted_iota(jnp.int32, (HEAD_WIDTH, tq), 0)
    qt_ref[0] = jnp.where(dim < HEAD_DIM, qt, 0.0).astype(BF16)
    qt_ref[1] = jnp.where(dim >= HEAD_DIM, qt, 0.0).astype(BF16)

    m_ref[...] = jnp.full(m_ref.shape, MASK_VALUE, F32)
    l_ref[...] = jnp.zeros(l_ref.shape, F32)
    acc_ref[...] = jnp.zeros(acc_ref.shape, F32)

    half = tq // 2
    chunks = [(c, slice(hf * half, (hf + 1) * half)) for c in range(2) for hf in range(2)]

    def keys(kb):
        return k_ref[0, pl.ds(pl.multiple_of(kb * tk, tk), tk), :]

    def scores(kk, chunk):
        c, cols = chunk
        return jnp.dot(kk, qt_ref[c, :, cols], preferred_element_type=F32)

    def softmax_update(s, chunk, bias):
        c, cols = chunk
        if bias is not None:
            s = s + bias[:, cols]
        m_prev = m_ref[c, :, cols]
        m_new = jnp.maximum(m_prev, jnp.max(s, axis=0, keepdims=True))
        alpha = jnp.exp(m_prev - m_new)
        p = jnp.exp(s - m_new)
        l_ref[c, :, cols] = alpha * l_ref[c, :, cols] + jnp.sum(p, axis=0, keepdims=True)
        m_ref[c, :, cols] = m_new
        return p.astype(BF16), alpha

    def accumulate(vt, p, alpha, chunk):
        c, cols = chunk
        acc_ref[c, :, cols] = alpha * acc_ref[c, :, cols] + jnp.dot(vt, p, preferred_element_type=F32)

    def process(kb, bias, has_next):
        kk = keys(kb)
        vt = vt_ref[kb]
        s1 = scores(kk, chunks[1])
        p0, a0 = softmax_update(s0_ref[...], chunks[0], bias)
        s2 = scores(kk, chunks[2])
        accumulate(vt, p0, a0, chunks[0])
        p1, a1 = softmax_update(s1, chunks[1], bias)
        s3 = scores(kk, chunks[3])
        accumulate(vt, p1, a1, chunks[1])
        p2, a2 = softmax_update(s2, chunks[2], bias)
        if has_next:
            s0_ref[...] = scores(keys(kb + 1), chunks[0])
        accumulate(vt, p2, a2, chunks[2])
        p3, a3 = softmax_update(s3, chunks[3], bias)
        accumulate(vt, p3, a3, chunks[3])

    s0_ref[...] = scores(keys(0), chunks[0])

    def far_body(kb, carry):
        process(kb, None, True)
        return carry

    lax.fori_loop(0, jnp.maximum(qi - 1, 0), far_body, 0)

    @pl.when(qi >= 1)
    def _():
        process(qi - 1, bias_ref[0, 1], True)

    process(qi, bias_ref[0, 0], False)

    lv = lam_ref[...]
    lam = (
        jnp.exp(jnp.sum(lv[0:1] * lv[1:2], axis=-1, keepdims=True))
        - jnp.exp(jnp.sum(lv[2:3] * lv[3:4], axis=-1, keepdims=True))
        + lambda_init
    )
    ot = acc_ref[0] / l_ref[0] - lam * (acc_ref[1] / l_ref[1])
    ot = ot * lax.rsqrt(jnp.mean(ot * ot, axis=0, keepdims=True) + LN_EPS)
    o = ot.T * (sub_ref[...] * (1.0 - lambda_init))
    o_ref[0] = o.astype(o_ref.dtype)


def _attention(qkv, bias_tiles, lam_vec, subln_w, lambda_init):
    b, s, _ = qkv.shape
    tq, tk = ATTN_TQ, ATTN_TK
    assert tq == tk and s % tq == 0
    kernel = functools.partial(_attn_kernel, lambda_init=lambda_init)
    return pl.pallas_call(
        kernel,
        grid=(b, N_HEADS, s // tq),
        in_specs=[
            pl.BlockSpec((1, tq, HEAD_WIDTH), lambda bi, h, qi: (bi, qi, h)),
            pl.BlockSpec((1, s, HEAD_WIDTH), lambda bi, h, qi: (bi, 0, N_HEADS + h)),
            pl.BlockSpec((1, s, HEAD_WIDTH), lambda bi, h, qi: (bi, 0, 2 * N_HEADS + h)),
            pl.BlockSpec((1, 2, tk, tq), lambda bi, h, qi: (h, 0, 0, 0)),
            pl.BlockSpec((4, HEAD_DIM), lambda bi, h, qi: (0, 0)),
            pl.BlockSpec((1, HEAD_WIDTH), lambda bi, h, qi: (0, 0)),
        ],
        out_specs=pl.BlockSpec((1, tq, HEAD_WIDTH), lambda bi, h, qi: (bi, qi, h)),
        out_shape=jax.ShapeDtypeStruct((b, s, ATTN_WIDTH), BF16),
        scratch_shapes=[
            pltpu.VMEM((2, HEAD_WIDTH, tq), BF16),
            pltpu.VMEM((s // tk, HEAD_WIDTH, tk), BF16),
            pltpu.VMEM((2, 1, tq), F32),
            pltpu.VMEM((2, 1, tq), F32),
            pltpu.VMEM((2, HEAD_WIDTH, tq), F32),
            pltpu.VMEM((tk, tq // 2), F32),
        ],
        compiler_params=_params("parallel", "parallel", "arbitrary"),
        name="diff_attention",
    )(qkv, qkv, qkv, bias_tiles, lam_vec, subln_w.reshape(1, HEAD_WIDTH))


def _tail_kernel(attn_ref, u_ref, halo_ref, ga_ref, gp_ref, x_ref, pw_ref, ps_ref, wba_ref, wbp_ref, wout_ref,
                 g_ref, b_ref, of_ref, ob_ref, ubuf, *, tiles_per_seq):
    i = pl.program_id(0)
    tm = u_ref.shape[0]
    tile_in_seq = i % tiles_per_seq
    halo = jnp.where(tile_in_seq == 0, 0.0, halo_ref[...])
    ubuf[0:MAX_WINDOW, :] = halo
    ubuf[MAX_WINDOW:MAX_WINDOW + tm, :] = u_ref[...]
    t = tile_in_seq * tm + lax.broadcasted_iota(jnp.int32, (tm, 1), 0)

    pooled = []
    for g, w in enumerate(POOL_WINDOWS):
        cs = slice(g * POOL_GROUP, (g + 1) * POOL_GROUP)
        cur = ubuf[MAX_WINDOW:MAX_WINDOW + tm, cs]
        win_sum = cur
        for j in range(1, w):
            win_sum = win_sum + ubuf[MAX_WINDOW - j:MAX_WINDOW - j + tm, cs]
        cnt = jnp.minimum(t + 1, w).astype(F32)
        z = (win_sum / cnt - cur).astype(BF16)
        y = jnp.dot(z, pw_ref[g], preferred_element_type=F32)
        pooled.append((y * ps_ref[:, cs]).astype(BF16))
    pool = jnp.concatenate(pooled, axis=-1)

    a = jnp.dot(attn_ref[...], wba_ref[...], preferred_element_type=F32)
    p = jnp.dot(pool, wbp_ref[...], preferred_element_type=F32)
    merged = jax.nn.sigmoid(ga_ref[...]) * a + jax.nn.sigmoid(gp_ref[...]) * p
    mix = jnp.dot(merged.astype(BF16), wout_ref[...], preferred_element_type=F32)
    out = _layer_norm(ALPHA * x_ref[...] + mix, g_ref[...], b_ref[...])
    of_ref[...] = out
    ob_ref[...] = out.astype(BF16)


def _resident(shape):
    zeros = (0,) * len(shape)
    return pl.BlockSpec(shape, lambda i: zeros, pipeline_mode=pl.Buffered(1))


def _mixer_tail(attn, u, gates, x, pool_w, pool_scale, w_ba, w_bp, w_out, ln_g, ln_b, seq):
    t = x.shape[0]
    tm = TAIL_TM
    assert seq % tm == 0 and tm % MAX_WINDOW == 0
    halo_blocks = tm // MAX_WINDOW
    kernel = functools.partial(_tail_kernel, tiles_per_seq=seq // tm)
    return pl.pallas_call(
        kernel,
        grid=(t // tm,),
        in_specs=[
            pl.BlockSpec((tm, ATTN_WIDTH), lambda i: (i, 0)),
            pl.BlockSpec((tm, POOL_WIDTH), lambda i: (i, 0)),
            pl.BlockSpec((MAX_WINDOW, POOL_WIDTH), lambda i: (jnp.maximum(i * halo_blocks - 1, 0), 0)),
            pl.BlockSpec((tm, D_MODEL), lambda i: (i, 0)),
            pl.BlockSpec((tm, D_MODEL), lambda i: (i, 1)),
            pl.BlockSpec((tm, D_MODEL), lambda i: (i, 0)),
            _resident(pool_w.shape),
            _resident((1, POOL_WIDTH)),
            _resident(w_ba.shape),
            _resident(w_bp.shape),
            _resident(w_out.shape),
            _resident((1, D_MODEL)),
            _resident((1, D_MODEL)),
        ],
        out_specs=[
            pl.BlockSpec((tm, D_MODEL), lambda i: (i, 0)),
            pl.BlockSpec((tm, D_MODEL), lambda i: (i, 0)),
        ],
        out_shape=[
            jax.ShapeDtypeStruct((t, D_MODEL), F32),
            jax.ShapeDtypeStruct((t, D_MODEL), BF16),
        ],
        scratch_shapes=[pltpu.VMEM((MAX_WINDOW + tm, POOL_WIDTH), F32)],
        compiler_params=_params("parallel"),
        name="mixer_tail",
    )(attn, u, u, gates, gates, x, pool_w, pool_scale.reshape(1, POOL_WIDTH), w_ba, w_bp, w_out,
      ln_g.reshape(1, D_MODEL), ln_b.reshape(1, D_MODEL))


def _swiglu(hb, wg, wu):
    g = jnp.dot(hb, wg, preferred_element_type=F32)
    u = jnp.dot(hb, wu, preferred_element_type=F32)
    return (g * jax.nn.sigmoid(g) * u).astype(BF16)


def _dense_ffn_kernel(hb_ref, hf_ref, wg_ref, wu_ref, wd_ref, g_ref, b_ref, of_ref, ob_ref, acc_ref):
    f = pl.program_id(1)

    @pl.when(f == 0)
    def _():
        acc_ref[...] = jnp.zeros(acc_ref.shape, F32)

    a = _swiglu(hb_ref[...], wg_ref[...], wu_ref[...])
    acc_ref[...] += jnp.dot(a, wd_ref[...], preferred_element_type=F32)

    @pl.when(f == pl.num_programs(1) - 1)
    def _():
        out = _layer_norm(ALPHA * hf_ref[...] + acc_ref[...], g_ref[...], b_ref[...])
        of_ref[...] = out
        ob_ref[...] = out.astype(BF16)


def _dense_ffn(hb, hf, wg, wu, wd, ln_g, ln_b):
    t = hb.shape[0]
    d_ff = wg.shape[1]
    tm, tf = FFN_TM, FFN_TF
    assert d_ff % tf == 0
    return pl.pallas_call(
        _dense_ffn_kernel,
        grid=(t // tm, d_ff // tf),
        in_specs=[
            pl.BlockSpec((tm, D_MODEL), lambda i, f: (i, 0)),
            pl.BlockSpec((tm, D_MODEL), lambda i, f: (i, 0)),
            pl.BlockSpec((D_MODEL, tf), lambda i, f: (0, f)),
            pl.BlockSpec((D_MODEL, tf), lambda i, f: (0, f)),
            pl.BlockSpec((tf, D_MODEL), lambda i, f: (f, 0)),
            pl.BlockSpec((1, D_MODEL), lambda i, f: (0, 0)),
            pl.BlockSpec((1, D_MODEL), lambda i, f: (0, 0)),
        ],
        out_specs=[
            pl.BlockSpec((tm, D_MODEL), lambda i, f: (i, 0)),
            pl.BlockSpec((tm, D_MODEL), lambda i, f: (i, 0)),
        ],
        out_shape=[
            jax.ShapeDtypeStruct((t, D_MODEL), F32),
            jax.ShapeDtypeStruct((t, D_MODEL), BF16),
        ],
        scratch_shapes=[pltpu.VMEM((tm, D_MODEL), F32)],
        compiler_params=_params("parallel", "arbitrary"),
        name="dense_ffn",
    )(hb, hf, wg, wu, wd, ln_g.reshape(1, D_MODEL), ln_b.reshape(1, D_MODEL))


ROUTE_W0, ROUTE_W1, ROUTE_E0, ROUTE_E1, ROUTE_R0, ROUTE_R1 = range(6)


def _router_kernel(h_ref, rw_ref, route_ref, counts_ref, carry_ref):
    i = pl.program_id(0)
    tm = h_ref.shape[0]

    @pl.when(i == 0)
    def _():
        carry_ref[...] = jnp.zeros(carry_ref.shape, F32)

    logits = jnp.dot(h_ref[...], rw_ref[...], preferred_element_type=F32, precision=lax.Precision.HIGHEST)
    lane = lax.broadcasted_iota(jnp.int32, (tm, LANES), 1)
    lg = jnp.where(lane < N_EXPERTS, logits, -jnp.inf)
    m1 = jnp.max(lg, axis=-1, keepdims=True)
    e1 = jnp.min(jnp.where(lg == m1, lane, LANES), axis=-1, keepdims=True)
    lg2 = jnp.where(lane == e1, -jnp.inf, lg)
    m2 = jnp.max(lg2, axis=-1, keepdims=True)
    e2 = jnp.min(jnp.where(lg2 == m2, lane, LANES), axis=-1, keepdims=True)
    x2 = jnp.exp(m2 - m1)
    w1 = 1.0 / (1.0 + x2)
    w2 = x2 / (1.0 + x2)

    sel1 = lane == e1
    sel2 = lane == e2
    mask = jnp.where(sel1 | sel2, 1.0, 0.0)
    r = lax.broadcasted_iota(jnp.int32, (tm, tm), 0)
    c = lax.broadcasted_iota(jnp.int32, (tm, tm), 1)
    tri = jnp.where(r >= c, 1.0, 0.0).astype(BF16)
    incl = jnp.dot(tri, mask.astype(BF16), preferred_element_type=F32)
    rank = incl - mask + carry_ref[...]
    total = carry_ref[...] + jnp.sum(mask, axis=0, keepdims=True)
    carry_ref[...] = total
    counts_ref[...] = total

    r1 = jnp.sum(jnp.where(sel1, rank, 0.0), axis=-1, keepdims=True)
    r2 = jnp.sum(jnp.where(sel2, rank, 0.0), axis=-1, keepdims=True)
    cols = (w1, w2, e1.astype(F32), e2.astype(F32), r1, r2)
    route = jnp.zeros((tm, LANES), F32)
    for idx, val in enumerate(cols):
        route = jnp.where(lane == idx, val, route)
    route_ref[...] = route


def _router(hf, router_w):
    t = hf.shape[0]
    tm = ROUTER_TM
    rw = jnp.pad(router_w, ((0, 0), (0, LANES - N_EXPERTS)))
    return pl.pallas_call(
        _router_kernel,
        grid=(t // tm,),
        in_specs=[
            pl.BlockSpec((tm, D_MODEL), lambda i: (i, 0)),
            pl.BlockSpec((D_MODEL, LANES), lambda i: (0, 0)),
        ],
        out_specs=[
            pl.BlockSpec((tm, LANES), lambda i: (i, 0)),
            pl.BlockSpec((1, LANES), lambda i: (0, 0)),
        ],
        out_shape=[
            jax.ShapeDtypeStruct((t, LANES), F32),
            jax.ShapeDtypeStruct((1, LANES), F32),
        ],
        scratch_shapes=[pltpu.VMEM((1, LANES), F32)],
        compiler_params=_params("arbitrary"),
        name="moe_router",
    )(hf, rw)


def _dispatch_kernel(dest_ref, h_ref, xs_in_hbm, xs_hbm, sem):
    del xs_in_hbm
    n = h_ref.shape[0]

    def row_copy(r, k):
        return pltpu.make_async_copy(
            h_ref.at[pl.ds(r, 1)], xs_hbm.at[pl.ds(dest_ref[0, 0, TOP_K * r + k], 1)], sem)

    def start(r, carry):
        for k in range(TOP_K):
            row_copy(r, k).start()
        return carry

    def wait(r, carry):
        for k in range(TOP_K):
            row_copy(r, k).wait()
        return carry

    lax.fori_loop(0, n, start, 0)
    lax.fori_loop(0, n, wait, 0)


def _dispatch(hf, dest, n_rows):
    t = hf.shape[0]
    tm = DISPATCH_TM
    xs0 = jnp.zeros((n_rows, D_MODEL), F32)
    return pl.pallas_call(
        _dispatch_kernel,
        grid=(t // tm,),
        in_specs=[
            pl.BlockSpec((1, 1, TOP_K * tm), lambda i: (i, 0, 0), memory_space=pltpu.SMEM),
            pl.BlockSpec((tm, D_MODEL), lambda i: (i, 0)),
            pl.BlockSpec(memory_space=pl.ANY),
        ],
        out_specs=pl.BlockSpec(memory_space=pl.ANY),
        out_shape=jax.ShapeDtypeStruct((n_rows, D_MODEL), F32),
        scratch_shapes=[pltpu.SemaphoreType.DMA(())],
        input_output_aliases={2: 0},
        compiler_params=_params("arbitrary"),
        name="moe_dispatch",
    )(dest.reshape(t // tm, 1, TOP_K * tm), hf, xs0)


def _moe_ffn_kernel(te_ref, ok_ref, xs_ref, wg_ref, wu_ref, wd_ref, ys_ref, xb_ref, acc_ref):
    i = pl.program_id(0)
    f = pl.program_id(1)
    ok = ok_ref[i] == 1

    @pl.when(ok & (f == 0))
    def _():
        xb_ref[...] = xs_ref[...].astype(BF16)
        acc_ref[...] = jnp.zeros(acc_ref.shape, F32)

    @pl.when(ok)
    def _():
        a = _swiglu(xb_ref[...], wg_ref[...], wu_ref[...])
        acc_ref[...] += jnp.dot(a, wd_ref[...], preferred_element_type=F32)

    @pl.when(f == pl.num_programs(1) - 1)
    def _():
        ys_ref[...] = jnp.where(ok, acc_ref[...], 0.0)


def _moe_ffn(xs, tile_expert, tile_ok, wg, wu, wd):
    n_rows = xs.shape[0]
    d_ff = wg.shape[2]
    tm, tf = MOE_TM, FFN_TF
    n_f = d_ff // tf
    n_tiles = n_rows // tm

    def f_idx(i, f, ok):
        return jnp.where(ok[i] == 1, f, n_f - 1)

    return pl.pallas_call(
        _moe_ffn_kernel,
        grid_spec=pltpu.PrefetchScalarGridSpec(
            num_scalar_prefetch=2,
            grid=(n_tiles, n_f),
            in_specs=[
                pl.BlockSpec((tm, D_MODEL), lambda i, f, te, ok: (i, 0)),
                pl.BlockSpec((None, D_MODEL, tf), lambda i, f, te, ok: (te[i], 0, f_idx(i, f, ok))),
                pl.BlockSpec((None, D_MODEL, tf), lambda i, f, te, ok: (te[i], 0, f_idx(i, f, ok))),
                pl.BlockSpec((None, tf, D_MODEL), lambda i, f, te, ok: (te[i], f_idx(i, f, ok), 0)),
            ],
            out_specs=pl.BlockSpec((tm, D_MODEL), lambda i, f, te, ok: (i, 0)),
            scratch_shapes=[pltpu.VMEM((tm, D_MODEL), BF16), pltpu.VMEM((tm, D_MODEL), F32)],
        ),
        out_shape=jax.ShapeDtypeStruct((n_rows, D_MODEL), F32),
        compiler_params=_params("arbitrary", "arbitrary"),
        name="moe_ffn",
    )(tile_expert, tile_ok, xs, wg, wu, wd)


def _combine_kernel(dest_ref, h_ref, route_ref, ys_hbm, g_ref, b_ref, o_ref, y_ref, sem):
    tm = h_ref.shape[0]

    def row_copy(r, k):
        return pltpu.make_async_copy(
            ys_hbm.at[pl.ds(dest_ref[0, 0, TOP_K * r + k], 1)], y_ref.at[k, pl.ds(r, 1)], sem)

    def start(r, carry):
        for k in range(TOP_K):
            row_copy(r, k).start()
        return carry

    def wait(r, carry):
        for k in range(TOP_K):
            row_copy(r, k).wait()
        return carry

    lax.fori_loop(0, tm, start, 0)
    lax.fori_loop(0, tm, wait, 0)

    route = route_ref[...]
    w1 = route[:, ROUTE_W0:ROUTE_W0 + 1]
    w2 = route[:, ROUTE_W1:ROUTE_W1 + 1]
    y = w1 * y_ref[0] + w2 * y_ref[1]
    o_ref[...] = _layer_norm(ALPHA * h_ref[...] + y, g_ref[...], b_ref[...])


def _combine(hf, route, dest, ys, ln_g, ln_b):
    t = hf.shape[0]
    tm = COMBINE_TM
    return pl.pallas_call(
        _combine_kernel,
        grid=(t // tm,),
        in_specs=[
            pl.BlockSpec((1, 1, TOP_K * tm), lambda i: (i, 0, 0), memory_space=pltpu.SMEM),
            pl.BlockSpec((tm, D_MODEL), lambda i: (i, 0)),
            pl.BlockSpec((tm, LANES), lambda i: (i, 0)),
            pl.BlockSpec(memory_space=pl.ANY),
            pl.BlockSpec((1, D_MODEL), lambda i: (0, 0)),
            pl.BlockSpec((1, D_MODEL), lambda i: (0, 0)),
        ],
        out_specs=pl.BlockSpec((tm, D_MODEL), lambda i: (i, 0)),
        out_shape=jax.ShapeDtypeStruct((t, D_MODEL), F32),
        scratch_shapes=[pltpu.VMEM((TOP_K, tm, D_MODEL), F32), pltpu.SemaphoreType.DMA(())],
        compiler_params=_params("arbitrary"),
        name="moe_combine",
    )(dest.reshape(t // tm, 1, TOP_K * tm), hf, route, ys, ln_g.reshape(1, D_MODEL), ln_b.reshape(1, D_MODEL))


def _moe(hf, router_w, wg, wu, wd, ln_g, ln_b):
    t = hf.shape[0]
    tm = MOE_TM
    n_tiles = (TOP_K * t) // tm + N_EXPERTS
    route, counts = _router(hf, router_w)

    counts = counts[0, :N_EXPERTS].astype(jnp.int32)
    tiles_per_expert = (counts + tm - 1) // tm
    tile_end = jnp.cumsum(tiles_per_expert)
    group_start = (tile_end - tiles_per_expert) * tm
    tile_ids = jnp.arange(n_tiles, dtype=jnp.int32)
    n_used = tile_end[-1]
    last_used = jnp.maximum(n_used - 1, 0)
    expert_of_tile = jnp.sum(tile_ids[:, None] >= tile_end[None, :], axis=1).astype(jnp.int32)
    tile_ok = (tile_ids < n_used).astype(jnp.int32)
    tile_expert = jnp.where(tile_ok == 1, expert_of_tile, expert_of_tile[last_used])

    experts = route[:, ROUTE_E0:ROUTE_E1 + 1].astype(jnp.int32)
    ranks = route[:, ROUTE_R0:ROUTE_R1 + 1].astype(jnp.int32)
    dest = (group_start[experts] + ranks).reshape(-1)

    xs = _dispatch(hf, dest, n_tiles * tm)
    ys = _moe_ffn(xs, tile_expert, tile_ok, wg, wu, wd)
    return _combine(hf, route, dest, ys, ln_g, ln_b)


def kernel(x, w_in, lambdas, subln_w, pool_w, pool_scale, w_branch_attn, w_branch_pool, w_out, rel_bias,
           ln1_g, ln1_b, dense_w_gate, dense_w_up, dense_w_down, router_w, moe_w_gate, moe_w_up, moe_w_down,
           ln2_g, ln2_b):
    b, s, d = x.shape
    t = b * s
    xf = x.reshape(t, d)
    xin = xf
    bias_tiles = _bias_tiles(rel_bias)
    qkv_blocks = 3 * ATTN_WIDTH // PROJ_TN
    u_blocks = POOL_WIDTH // PROJ_TN
    gate_blocks = 2 * D_MODEL // PROJ_TN
    for l in range(DEPTH):
        lambda_init = 0.8 - 0.6 * math.exp(-0.3 * l)
        w_in_l = w_in[l].astype(BF16)
        qkv = _proj(xin, w_in_l, 0, qkv_blocks, BF16)
        u = _proj(xin, w_in_l, qkv_blocks, u_blocks, F32)
        gates = _proj(xin, w_in_l, qkv_blocks + u_blocks, gate_blocks, F32)
        attn = _attention(qkv.reshape(b, s, 3 * ATTN_WIDTH), bias_tiles, lambdas[l], subln_w[l], lambda_init)
        hf, hb = _mixer_tail(attn.reshape(t, ATTN_WIDTH), u, gates, xf, pool_w[l].astype(BF16), pool_scale[l],
                             w_branch_attn[l].astype(BF16), w_branch_pool[l].astype(BF16), w_out[l].astype(BF16),
                             ln1_g[l], ln1_b[l], s)
        if l % 2 == 0:
            xf, xin = _dense_ffn(hb, hf, dense_w_gate[l // 2].astype(BF16), dense_w_up[l // 2].astype(BF16),
                                 dense_w_down[l // 2].astype(BF16), ln2_g[l], ln2_b[l])
        else:
            xf = _moe(hf, router_w[l // 2], moe_w_gate[l // 2].astype(BF16), moe_w_up[l // 2].astype(BF16),
                      moe_w_down[l // 2].astype(BF16), ln2_g[l], ln2_b[l])
            xin = xf
    return xf.reshape(b, s, d)
```

```python
import functools
import math

import jax
import jax.numpy as jnp
from jax import lax
from jax.experimental import pallas as pl
from jax.experimental.pallas import tpu as pltpu

D_MODEL = 2048
DEPTH = 2
N_HEADS = 8
HEAD_DIM = 64
HEAD_WIDTH = 2 * HEAD_DIM
ATTN_WIDTH = N_HEADS * HEAD_WIDTH
POOL_WINDOWS = (2, 4, 8, 16)
POOL_GROUP = 256
POOL_WIDTH = POOL_GROUP * len(POOL_WINDOWS)
MAX_WINDOW = max(POOL_WINDOWS)
N_BUCKETS = 32
MAX_DISTANCE = 128
N_EXPERTS = 8
TOP_K = 2
ALPHA = (2.0 * DEPTH) ** 0.25
LN_EPS = 1e-5
LOG2E = math.log2(math.e)
Q_SCALE = HEAD_DIM ** -0.5 * LOG2E
ONES_ROWS = 16

LANES = 128
MASK_VALUE = -1e30
VMEM_LIMIT = 56 * 1024 * 1024

PROJ_TM = 1024
PROJ_TN = 1024
ATTN_TQ = 512
ATTN_TK = 512
TAIL_TM = 256
FFN_TM = 512
FFN_TF = 512
ROUTER_TM = 512
DISPATCH_TM = 512
MOE_TM = 512
COMBINE_TM = 256

F32 = jnp.float32
BF16 = jnp.bfloat16


def _params(*sem):
    return pltpu.CompilerParams(dimension_semantics=sem, vmem_limit_bytes=VMEM_LIMIT)


def _layer_norm(r, g, b):
    mu = jnp.mean(r, axis=-1, keepdims=True)
    c = r - mu
    var = jnp.mean(c * c, axis=-1, keepdims=True)
    return c * lax.rsqrt(var + LN_EPS) * g + b


def _proj_kernel(x_ref, w_ref, o_ref, *, scaled_blocks, scale):
    x = x_ref[...].astype(BF16)
    acc = jnp.dot(x, w_ref[...], preferred_element_type=F32)
    if scaled_blocks:
        acc = acc * jnp.where(pl.program_id(1) < scaled_blocks, scale, 1.0)
    o_ref[...] = acc.astype(o_ref.dtype)


def _proj(x, w, col_block0, n_col_blocks, out_dtype, scaled_blocks=0, scale=1.0):
    t, k = x.shape
    tm, tn = PROJ_TM, PROJ_TN
    return pl.pallas_call(
        functools.partial(_proj_kernel, scaled_blocks=scaled_blocks, scale=scale),
        grid=(t // tm, n_col_blocks),
        in_specs=[
            pl.BlockSpec((tm, k), lambda i, j: (i, 0)),
            pl.BlockSpec((k, tn), lambda i, j: (0, j + col_block0)),
        ],
        out_specs=pl.BlockSpec((tm, tn), lambda i, j: (i, j)),
        out_shape=jax.ShapeDtypeStruct((t, n_col_blocks * tn), out_dtype),
        compiler_params=_params("parallel", "arbitrary"),
        name="in_proj",
    )(x, w)


def _bias_kernel(rb_ref, o_ref):
    h = pl.program_id(0)
    tk, tq = o_ref.shape[2], o_ref.shape[3]
    kpos = lax.broadcasted_iota(jnp.int32, (tk, tq), 0)
    qpos = lax.broadcasted_iota(jnp.int32, (tk, tq), 1)
    max_exact = N_BUCKETS // 2
    far = rb_ref[h * N_BUCKETS + N_BUCKETS - 1]
    for blk in range(2):
        dist = qpos - kpos + blk * tk
        n = jnp.maximum(dist, 0)
        large = max_exact + (
            jnp.log(jnp.maximum(n, 1).astype(F32) / max_exact) / math.log(MAX_DISTANCE / max_exact)
            * (N_BUCKETS - max_exact)
        ).astype(jnp.int32)
        large = jnp.minimum(large, N_BUCKETS - 1)
        bucket = jnp.where(n < max_exact, n, large)
        bias = jnp.zeros((tk, tq), F32)
        for b in range(N_BUCKETS):
            bias = jnp.where(bucket == b, rb_ref[h * N_BUCKETS + b], bias)
        bias = (bias - far) * LOG2E
        if blk == 0:
            bias = jnp.where(dist >= 0, bias, MASK_VALUE)
        o_ref[0, blk] = bias


def _bias_tiles(rel_bias):
    assert ATTN_TK >= MAX_DISTANCE
    rb = jnp.transpose(rel_bias).reshape(-1)
    return pl.pallas_call(
        _bias_kernel,
        grid_spec=pltpu.PrefetchScalarGridSpec(
            num_scalar_prefetch=1,
            grid=(N_HEADS,),
            in_specs=[],
            out_specs=pl.BlockSpec((1, 2, ATTN_TK, ATTN_TQ), lambda h, rb: (h, 0, 0, 0)),
        ),
        out_shape=jax.ShapeDtypeStruct((N_HEADS, 2, ATTN_TK, ATTN_TQ), F32),
        compiler_params=_params("arbitrary"),
        name="bias_tiles",
    )(rb)


def _attn_kernel(q_ref, k_ref, v_ref, bias_ref, lam_ref, sub_ref, o_ref, qt_ref, vt_ref, m_ref, acc_ref,
                 s0_ref, pp_ref, pa_ref, *, lambda_init):
    qi = pl.program_id(2)
    tq = q_ref.shape[1]
    n_kb, _, tk = vt_ref.shape

    @pl.when(qi == 0)
    def _():
        row = lax.broadcasted_iota(jnp.int32, (ONES_ROWS, tk), 0)
        ones_rows = jnp.where(row == 0, 1.0, 0.0).astype(BF16)
        for kb in range(n_kb):
            vt_ref[kb, 0:HEAD_WIDTH, :] = v_ref[0, kb * tk:(kb + 1) * tk, :].astype(F32).T.astype(BF16)
            vt_ref[kb, HEAD_WIDTH:HEAD_WIDTH + ONES_ROWS, :] = ones_rows

    qt = q_ref[0].astype(F32).T
    dim = lax.broadcasted_iota(jnp.int32, (HEAD_WIDTH, tq), 0)
    qt_ref[0] = jnp.where(dim < HEAD_DIM, qt, 0.0).astype(BF16)
    qt_ref[1] = jnp.where(dim >= HEAD_DIM, qt, 0.0).astype(BF16)

    m_ref[...] = jnp.full(m_ref.shape, MASK_VALUE, F32)
    acc_ref[...] = jnp.zeros(acc_ref.shape, F32)

    half = tq // 2
    chunks = [(c, slice(hf * half, (hf + 1) * half)) for c in range(2) for hf in range(2)]

    def keys(kb):
        return k_ref[0, pl.ds(pl.multiple_of(kb * tk, tk), tk), :]

    def key_rows(kind, chunk):
        return tk // 2 if kind == "diag" and chunk[1].start == 0 else tk

    def scores(kk, chunk, kind="far"):
        c, cols = chunk
        return jnp.dot(kk[:key_rows(kind, chunk)], qt_ref[c, :, cols], preferred_element_type=F32)

    def add_bias(s, chunk, kind):
        cols = chunk[1]
        if kind == "diag":
            return s + bias_ref[0, 0, 0:s.shape[0], cols]
        if kind == "near" and cols.start == 0:
            r0 = tk - MAX_DISTANCE
            return jnp.concatenate([s[:r0], s[r0:] + bias_ref[0, 1, r0:tk, cols]], axis=0)
        return s

    def softmax_update(s, chunk, kind):
        c, cols = chunk
        s = add_bias(s, chunk, kind)
        m_prev = m_ref[c, :, cols]
        m_new = jnp.maximum(m_prev, jnp.max(s, axis=0, keepdims=True))
        alpha = jnp.exp2(m_prev - m_new)
        p = jnp.exp2(s - m_new)
        m_ref[c, :, cols] = m_new
        return p.astype(BF16), alpha

    def accumulate(vt, p, alpha, chunk):
        c, cols = chunk
        pv = jnp.dot(vt[:, :p.shape[0]], p, preferred_element_type=F32)
        acc_ref[c, :, cols] = alpha * acc_ref[c, :, cols] + pv

    def accumulate_pending(kb):
        accumulate(vt_ref[kb], pp_ref[...], pa_ref[...], chunks[3])

    def process(kb, kind):
        kk = keys(kb)
        vt = vt_ref[kb]
        s1 = scores(kk, chunks[1], kind)
        accumulate_pending(jnp.maximum(kb - 1, 0))
        p0, a0 = softmax_update(s0_ref[0:key_rows(kind, chunks[0]), :], chunks[0], kind)
        s2 = scores(kk, chunks[2], kind)
        accumulate(vt, p0, a0, chunks[0])
        p1, a1 = softmax_update(s1, chunks[1], kind)
        s3 = scores(kk, chunks[3], kind)
        accumulate(vt, p1, a1, chunks[1])
        p2, a2 = softmax_update(s2, chunks[2], kind)
        if kind != "diag":
            s0_ref[...] = scores(keys(kb + 1), chunks[0])
        accumulate(vt, p2, a2, chunks[2])
        pp_ref[...], pa_ref[...] = softmax_update(s3, chunks[3], kind)

    s0_ref[...] = scores(keys(0), chunks[0])
    pp_ref[...] = jnp.zeros(pp_ref.shape, BF16)
    pa_ref[...] = jnp.ones(pa_ref.shape, F32)

    n_far = jnp.maximum(qi - 1, 0)

    def far_pair(j, carry):
        process(2 * j, "far")
        process(2 * j + 1, "far")
        return carry

    lax.fori_loop(0, n_far // 2, far_pair, 0)

    @pl.when(n_far % 2 == 1)
    def _():
        process(n_far - 1, "far")

    @pl.when(qi >= 1)
    def _():
        process(qi - 1, "near")

    process(qi, "diag")
    accumulate_pending(qi)

    lv = lam_ref[...]
    lam = (
        jnp.exp(jnp.sum(lv[0:1] * lv[1:2], axis=-1, keepdims=True))
        - jnp.exp(jnp.sum(lv[2:3] * lv[3:4], axis=-1, keepdims=True))
        + lambda_init
    )
    num = [acc_ref[c, 0:HEAD_WIDTH, :] for c in range(2)]
    den = [acc_ref[c, HEAD_WIDTH:HEAD_WIDTH + 1, :] for c in range(2)]
    ot = num[0] / den[0] - lam * (num[1] / den[1])
    ot = ot * lax.rsqrt(jnp.mean(ot * ot, axis=0, keepdims=True) + LN_EPS)
    o = ot.T * (sub_ref[...] * (1.0 - lambda_init))
    o_ref[0] = o.astype(o_ref.dtype)


def _attention(qkv, bias_tiles, lam_vec, subln_w, lambda_init):
    b, s, _ = qkv.shape
    tq, tk = ATTN_TQ, ATTN_TK
    assert tq == tk and s % tq == 0
    assert tq // 2 >= MAX_DISTANCE and (tk - MAX_DISTANCE) % 8 == 0
    kernel = functools.partial(_attn_kernel, lambda_init=lambda_init)
    return pl.pallas_call(
        kernel,
        grid=(b, N_HEADS, s // tq),
        in_specs=[
            pl.BlockSpec((1, tq, HEAD_WIDTH), lambda bi, h, qi: (bi, qi, h)),
            pl.BlockSpec((1, s, HEAD_WIDTH), lambda bi, h, qi: (bi, 0, N_HEADS + h)),
            pl.BlockSpec((1, s, HEAD_WIDTH), lambda bi, h, qi: (bi, 0, 2 * N_HEADS + h)),
            pl.BlockSpec((1, 2, tk, tq), lambda bi, h, qi: (h, 0, 0, 0)),
            pl.BlockSpec((4, HEAD_DIM), lambda bi, h, qi: (0, 0)),
            pl.BlockSpec((1, HEAD_WIDTH), lambda bi, h, qi: (0, 0)),
        ],
        out_specs=pl.BlockSpec((1, tq, HEAD_WIDTH), lambda bi, h, qi: (bi, qi, h)),
        out_shape=jax.ShapeDtypeStruct((b, s, ATTN_WIDTH), BF16),
        scratch_shapes=[
            pltpu.VMEM((2, HEAD_WIDTH, tq), BF16),
            pltpu.VMEM((s // tk, HEAD_WIDTH + ONES_ROWS, tk), BF16),
            pltpu.VMEM((2, 1, tq), F32),
            pltpu.VMEM((2, HEAD_WIDTH + ONES_ROWS, tq), F32),
            pltpu.VMEM((tk, tq // 2), F32),
            pltpu.VMEM((tk, tq // 2), BF16),
            pltpu.VMEM((1, tq // 2), F32),
        ],
        compiler_params=_params("parallel", "parallel", "arbitrary"),
        name="diff_attention",
    )(qkv, qkv, qkv, bias_tiles, lam_vec, subln_w.reshape(1, HEAD_WIDTH))


def _tail_kernel(attn_ref, u_ref, halo_ref, ga_ref, gp_ref, x_ref, pw_ref, ps_ref, wba_ref, wbp_ref, wout_ref,
                 g_ref, b_ref, of_ref, ob_ref, ubuf, *, tiles_per_seq):
    i = pl.program_id(0)
    tm = u_ref.shape[0]
    tile_in_seq = i % tiles_per_seq
    halo = jnp.where(tile_in_seq == 0, 0.0, halo_ref[...])
    ubuf[0:MAX_WINDOW, :] = halo
    ubuf[MAX_WINDOW:MAX_WINDOW + tm, :] = u_ref[...]
    t = tile_in_seq * tm + lax.broadcasted_iota(jnp.int32, (tm, 1), 0)

    pooled = []
    for g, w in enumerate(POOL_WINDOWS):
        cs = slice(g * POOL_GROUP, (g + 1) * POOL_GROUP)
        cur = ubuf[MAX_WINDOW:MAX_WINDOW + tm, cs]
        win_sum = cur
        for j in range(1, w):
            win_sum = win_sum + ubuf[MAX_WINDOW - j:MAX_WINDOW - j + tm, cs]
        cnt = jnp.minimum(t + 1, w).astype(F32)
        z = (win_sum / cnt - cur).astype(BF16)
        y = jnp.dot(z, pw_ref[g], preferred_element_type=F32)
        pooled.append((y * ps_ref[:, cs]).astype(BF16))
    pool = jnp.concatenate(pooled, axis=-1)

    a = jnp.dot(attn_ref[...], wba_ref[...], preferred_element_type=F32)
    p = jnp.dot(pool, wbp_ref[...], preferred_element_type=F32)
    merged = jax.nn.sigmoid(ga_ref[...]) * a + jax.nn.sigmoid(gp_ref[...]) * p
    mix = jnp.dot(merged.astype(BF16), wout_ref[...], preferred_element_type=F32)
    out = _layer_norm(ALPHA * x_ref[...] + mix, g_ref[...], b_ref[...])
    of_ref[...] = out
    ob_ref[...] = out.astype(BF16)


def _resident(shape):
    zeros = (0,) * len(shape)
    return pl.BlockSpec(shape, lambda i: zeros, pipeline_mode=pl.Buffered(1))


def _mixer_tail(attn, u, gates, x, pool_w, pool_scale, w_ba, w_bp, w_out, ln_g, ln_b, seq):
    t = x.shape[0]
    tm = TAIL_TM
    assert seq % tm == 0 and tm % MAX_WINDOW == 0
    halo_blocks = tm // MAX_WINDOW
    kernel = functools.partial(_tail_kernel, tiles_per_seq=seq // tm)
    return pl.pallas_call(
        kernel,
        grid=(t // tm,),
        in_specs=[
            pl.BlockSpec((tm, ATTN_WIDTH), lambda i: (i, 0)),
            pl.BlockSpec((tm, POOL_WIDTH), lambda i: (i, 0)),
            pl.BlockSpec((MAX_WINDOW, POOL_WIDTH), lambda i: (jnp.maximum(i * halo_blocks - 1, 0), 0)),
            pl.BlockSpec((tm, D_MODEL), lambda i: (i, 0)),
            pl.BlockSpec((tm, D_MODEL), lambda i: (i, 1)),
            pl.BlockSpec((tm, D_MODEL), lambda i: (i, 0)),
            _resident(pool_w.shape),
            _resident((1, POOL_WIDTH)),
            _resident(w_ba.shape),
            _resident(w_bp.shape),
            _resident(w_out.shape),
            _resident((1, D_MODEL)),
            _resident((1, D_MODEL)),
        ],
        out_specs=[
            pl.BlockSpec((tm, D_MODEL), lambda i: (i, 0)),
            pl.BlockSpec((tm, D_MODEL), lambda i: (i, 0)),
        ],
        out_shape=[
            jax.ShapeDtypeStruct((t, D_MODEL), F32),
            jax.ShapeDtypeStruct((t, D_MODEL), BF16),
        ],
        scratch_shapes=[pltpu.VMEM((MAX_WINDOW + tm, POOL_WIDTH), F32)],
        compiler_params=_params("parallel"),
        name="mixer_tail",
    )(attn, u, u, gates, gates, x, pool_w, pool_scale.reshape(1, POOL_WIDTH), w_ba, w_bp, w_out,
      ln_g.reshape(1, D_MODEL), ln_b.reshape(1, D_MODEL))


def _swiglu(hb, wg, wu):
    g = jnp.dot(hb, wg, preferred_element_type=F32)
    u = jnp.dot(hb, wu, preferred_element_type=F32)
    return (g * jax.nn.sigmoid(g) * u).astype(BF16)


def _dense_ffn_kernel(hb_ref, hf_ref, wg_ref, wu_ref, wd_ref, g_ref, b_ref, of_ref, ob_ref, acc_ref):
    f = pl.program_id(1)

    @pl.when(f == 0)
    def _():
        acc_ref[...] = jnp.zeros(acc_ref.shape, F32)

    a = _swiglu(hb_ref[...], wg_ref[...], wu_ref[...])
    acc_ref[...] += jnp.dot(a, wd_ref[...], preferred_element_type=F32)

    @pl.when(f == pl.num_programs(1) - 1)
    def _():
        out = _layer_norm(ALPHA * hf_ref[...] + acc_ref[...], g_ref[...], b_ref[...])
        of_ref[...] = out
        ob_ref[...] = out.astype(BF16)


def _dense_ffn(hb, hf, wg, wu, wd, ln_g, ln_b):
    t = hb.shape[0]
    d_ff = wg.shape[1]
    tm, tf = FFN_TM, FFN_TF
    assert d_ff % tf == 0
    return pl.pallas_call(
        _dense_ffn_kernel,
        grid=(t // tm, d_ff // tf),
        in_specs=[
            pl.BlockSpec((tm, D_MODEL), lambda i, f: (i, 0)),
            pl.BlockSpec((tm, D_MODEL), lambda i, f: (i, 0)),
            pl.BlockSpec((D_MODEL, tf), lambda i, f: (0, f)),
            pl.BlockSpec((D_MODEL, tf), lambda i, f: (0, f)),
            pl.BlockSpec((tf, D_MODEL), lambda i, f: (f, 0)),
            pl.BlockSpec((1, D_MODEL), lambda i, f: (0, 0)),
            pl.BlockSpec((1, D_MODEL), lambda i, f: (0, 0)),
        ],
        out_specs=[
            pl.BlockSpec((tm, D_MODEL), lambda i, f: (i, 0)),
            pl.BlockSpec((tm, D_MODEL), lambda i, f: (i, 0)),
        ],
        out_shape=[
            jax.ShapeDtypeStruct((t, D_MODEL), F32),
            jax.ShapeDtypeStruct((t, D_MODEL), BF16),
        ],
        scratch_shapes=[pltpu.VMEM((tm, D_MODEL), F32)],
        compiler_params=_params("parallel", "arbitrary"),
        name="dense_ffn",
    )(hb, hf, wg, wu, wd, ln_g.reshape(1, D_MODEL), ln_b.reshape(1, D_MODEL))


ROUTE_W0, ROUTE_W1, ROUTE_E0, ROUTE_E1, ROUTE_R0, ROUTE_R1 = range(6)


def _router_kernel(h_ref, rw_ref, route_ref, counts_ref, carry_ref):
    i = pl.program_id(0)
    tm = h_ref.shape[0]

    @pl.when(i == 0)
    def _():
        carry_ref[...] = jnp.zeros(carry_ref.shape, F32)

    logits = jnp.dot(h_ref[...], rw_ref[...], preferred_element_type=F32, precision=lax.Precision.HIGHEST)
    lane = lax.broadcasted_iota(jnp.int32, (tm, LANES), 1)
    lg = jnp.where(lane < N_EXPERTS, logits, -jnp.inf)
    m1 = jnp.max(lg, axis=-1, keepdims=True)
    e1 = jnp.min(jnp.where(lg == m1, lane, LANES), axis=-1, keepdims=True)
    lg2 = jnp.where(lane == e1, -jnp.inf, lg)
    m2 = jnp.max(lg2, axis=-1, keepdims=True)
    e2 = jnp.min(jnp.where(lg2 == m2, lane, LANES), axis=-1, keepdims=True)
    x2 = jnp.exp(m2 - m1)
    w1 = 1.0 / (1.0 + x2)
    w2 = x2 / (1.0 + x2)

    sel1 = lane == e1
    sel2 = lane == e2
    mask = jnp.where(sel1 | sel2, 1.0, 0.0)
    r = lax.broadcasted_iota(jnp.int32, (tm, tm), 0)
    c = lax.broadcasted_iota(jnp.int32, (tm, tm), 1)
    tri = jnp.where(r >= c, 1.0, 0.0).astype(BF16)
    incl = jnp.dot(tri, mask.astype(BF16), preferred_element_type=F32)
    rank = incl - mask + carry_ref[...]
    total = carry_ref[...] + jnp.sum(mask, axis=0, keepdims=True)
    carry_ref[...] = total
    counts_ref[...] = total

    r1 = jnp.sum(jnp.where(sel1, rank, 0.0), axis=-1, keepdims=True)
    r2 = jnp.sum(jnp.where(sel2, rank, 0.0), axis=-1, keepdims=True)
    cols = (w1, w2, e1.astype(F32), e2.astype(F32), r1, r2)
    route = jnp.zeros((tm, LANES), F32)
    for idx, val in enumerate(cols):
        route = jnp.where(lane == idx, val, route)
    route_ref[...] = route


def _router(hf, router_w):
    t = hf.shape[0]
    tm = ROUTER_TM
    rw = jnp.pad(router_w, ((0, 0), (0, LANES - N_EXPERTS)))
    return pl.pallas_call(
        _router_kernel,
        grid=(t // tm,),
        in_specs=[
            pl.BlockSpec((tm, D_MODEL), lambda i: (i, 0)),
            pl.BlockSpec((D_MODEL, LANES), lambda i: (0, 0)),
        ],
        out_specs=[
            pl.BlockSpec((tm, LANES), lambda i: (i, 0)),
            pl.BlockSpec((1, LANES), lambda i: (0, 0)),
        ],
        out_shape=[
            jax.ShapeDtypeStruct((t, LANES), F32),
            jax.ShapeDtypeStruct((1, LANES), F32),
        ],
        scratch_shapes=[pltpu.VMEM((1, LANES), F32)],
        compiler_params=_params("arbitrary"),
        name="moe_router",
    )(hf, rw)


def _dispatch_kernel(dest_ref, h_ref, xs_in_hbm, xs_hbm, sem):
    del xs_in_hbm
    n = h_ref.shape[0]

    def row_copy(r, k):
        return pltpu.make_async_copy(
            h_ref.at[pl.ds(r, 1)], xs_hbm.at[pl.ds(dest_ref[0, 0, TOP_K * r + k], 1)], sem)

    def start(r, carry):
        for k in range(TOP_K):
            row_copy(r, k).start()
        return carry

    def wait(r, carry):
        for k in range(TOP_K):
            row_copy(r, k).wait()
        return carry

    lax.fori_loop(0, n, start, 0)
    lax.fori_loop(0, n, wait, 0)


def _dispatch(hf, dest, n_rows):
    t = hf.shape[0]
    tm = DISPATCH_TM
    xs0 = jnp.zeros((n_rows, D_MODEL), F32)
    return pl.pallas_call(
        _dispatch_kernel,
        grid=(t // tm,),
        in_specs=[
            pl.BlockSpec((1, 1, TOP_K * tm), lambda i: (i, 0, 0), memory_space=pltpu.SMEM),
            pl.BlockSpec((tm, D_MODEL), lambda i: (i, 0)),
            pl.BlockSpec(memory_space=pl.ANY),
        ],
        out_specs=pl.BlockSpec(memory_space=pl.ANY),
        out_shape=jax.ShapeDtypeStruct((n_rows, D_MODEL), F32),
        scratch_shapes=[pltpu.SemaphoreType.DMA(())],
        input_output_aliases={2: 0},
        compiler_params=_params("arbitrary"),
        name="moe_dispatch",
    )(dest.reshape(t // tm, 1, TOP_K * tm), hf, xs0)


def _moe_ffn_kernel(te_ref, ok_ref, xs_ref, wg_ref, wu_ref, wd_ref, ys_ref, xb_ref, acc_ref):
    i = pl.program_id(0)
    f = pl.program_id(1)
    ok = ok_ref[i] == 1

    @pl.when(ok & (f == 0))
    def _():
        xb_ref[...] = xs_ref[...].astype(BF16)
        acc_ref[...] = jnp.zeros(acc_ref.shape, F32)

    @pl.when(ok)
    def _():
        a = _swiglu(xb_ref[...], wg_ref[...], wu_ref[...])
        acc_ref[...] += jnp.dot(a, wd_ref[...], preferred_element_type=F32)

    @pl.when(f == pl.num_programs(1) - 1)
    def _():
        ys_ref[...] = jnp.where(ok, acc_ref[...], 0.0)


def _moe_ffn(xs, tile_expert, tile_ok, wg, wu, wd):
    n_rows = xs.shape[0]
    d_ff = wg.shape[2]
    tm, tf = MOE_TM, FFN_TF
    n_f = d_ff // tf
    n_tiles = n_rows // tm

    def f_idx(i, f, ok):
        return jnp.where(ok[i] == 1, f, n_f - 1)

    return pl.pallas_call(
        _moe_ffn_kernel,
        grid_spec=pltpu.PrefetchScalarGridSpec(
            num_scalar_prefetch=2,
            grid=(n_tiles, n_f),
            in_specs=[
                pl.BlockSpec((tm, D_MODEL), lambda i, f, te, ok: (i, 0)),
                pl.BlockSpec((None, D_MODEL, tf), lambda i, f, te, ok: (te[i], 0, f_idx(i, f, ok))),
                pl.BlockSpec((None, D_MODEL, tf), lambda i, f, te, ok: (te[i], 0, f_idx(i, f, ok))),
                pl.BlockSpec((None, tf, D_MODEL), lambda i, f, te, ok: (te[i], f_idx(i, f, ok), 0)),
            ],
            out_specs=pl.BlockSpec((tm, D_MODEL), lambda i, f, te, ok: (i, 0)),
            scratch_shapes=[pltpu.VMEM((tm, D_MODEL), BF16), pltpu.VMEM((tm, D_MODEL), F32)],
        ),
        out_shape=jax.ShapeDtypeStruct((n_rows, D_MODEL), F32),
        compiler_params=_params("arbitrary", "arbitrary"),
        name="moe_ffn",
    )(tile_expert, tile_ok, xs, wg, wu, wd)


def _combine_kernel(dest_ref, h_ref, route_ref, ys_hbm, g_ref, b_ref, o_ref, y_ref, sem):
    tm = h_ref.shape[0]

    def row_copy(r, k):
        return pltpu.make_async_copy(
            ys_hbm.at[pl.ds(dest_ref[0, 0, TOP_K * r + k], 1)], y_ref.at[k, pl.ds(r, 1)], sem)

    def start(r, carry):
        for k in range(TOP_K):
            row_copy(r, k).start()
        return carry

    def wait(r, carry):
        for k in range(TOP_K):
            row_copy(r, k).wait()
        return carry

    lax.fori_loop(0, tm, start, 0)
    lax.fori_loop(0, tm, wait, 0)

    route = route_ref[...]
    w1 = route[:, ROUTE_W0:ROUTE_W0 + 1]
    w2 = route[:, ROUTE_W1:ROUTE_W1 + 1]
    y = w1 * y_ref[0] + w2 * y_ref[1]
    o_ref[...] = _layer_norm(ALPHA * h_ref[...] + y, g_ref[...], b_ref[...])


def _combine(hf, route, dest, ys, ln_g, ln_b):
    t = hf.shape[0]
    tm = COMBINE_TM
    return pl.pallas_call(
        _combine_kernel,
        grid=(t // tm,),
        in_specs=[
            pl.BlockSpec((1, 1, TOP_K * tm), lambda i: (i, 0, 0), memory_space=pltpu.SMEM),
            pl.BlockSpec((tm, D_MODEL), lambda i: (i, 0)),
            pl.BlockSpec((tm, LANES), lambda i: (i, 0)),
            pl.BlockSpec(memory_space=pl.ANY),
            pl.BlockSpec((1, D_MODEL), lambda i: (0, 0)),
            pl.BlockSpec((1, D_MODEL), lambda i: (0, 0)),
        ],
        out_specs=pl.BlockSpec((tm, D_MODEL), lambda i: (i, 0)),
        out_shape=jax.ShapeDtypeStruct((t, D_MODEL), F32),
        scratch_shapes=[pltpu.VMEM((TOP_K, tm, D_MODEL), F32), pltpu.SemaphoreType.DMA(())],
        compiler_params=_params("arbitrary"),
        name="moe_combine",
    )(dest.reshape(t // tm, 1, TOP_K * tm), hf, route, ys, ln_g.reshape(1, D_MODEL), ln_b.reshape(1, D_MODEL))


def _moe(hf, router_w, wg, wu, wd, ln_g, ln_b):
    t = hf.shape[0]
    tm = MOE_TM
    n_tiles = (TOP_K * t) // tm + N_EXPERTS
    route, counts = _router(hf, router_w)

    counts = counts[0, :N_EXPERTS].astype(jnp.int32)
    tiles_per_expert = (counts + tm - 1) // tm
    tile_end = jnp.cumsum(tiles_per_expert)
    group_start = (tile_end - tiles_per_expert) * tm
    tile_ids = jnp.arange(n_tiles, dtype=jnp.int32)
    n_used = tile_end[-1]
    last_used = jnp.maximum(n_used - 1, 0)
    expert_of_tile = jnp.sum(tile_ids[:, None] >= tile_end[None, :], axis=1).astype(jnp.int32)
    tile_ok = (tile_ids < n_used).astype(jnp.int32)
    tile_expert = jnp.where(tile_ok == 1, expert_of_tile, expert_of_tile[last_used])
    tile_expert = jnp.minimum(tile_expert, N_EXPERTS - 1)

    experts = route[:, ROUTE_E0:ROUTE_E1 + 1].astype(jnp.int32)
    ranks = route[:, ROUTE_R0:ROUTE_R1 + 1].astype(jnp.int32)
    dest = (group_start[experts] + ranks).reshape(-1)

    xs = _dispatch(hf, dest, n_tiles * tm)
    ys = _moe_ffn(xs, tile_expert, tile_ok, wg, wu, wd)
    return _combine(hf, route, dest, ys, ln_g, ln_b)


def kernel(x, w_in, lambdas, subln_w, pool_w, pool_scale, w_branch_attn, w_branch_pool, w_out, rel_bias,
           ln1_g, ln1_b, dense_w_gate, dense_w_up, dense_w_down, router_w, moe_w_gate, moe_w_up, moe_w_down,
           ln2_g, ln2_b):
    b, s, d = x.shape
    t = b * s
    xf = x.reshape(t, d)
    xin = xf
    bias_tiles = _bias_tiles(rel_bias)
    qkv_blocks = 3 * ATTN_WIDTH // PROJ_TN
    u_blocks = POOL_WIDTH // PROJ_TN
    gate_blocks = 2 * D_MODEL // PROJ_TN
    for l in range(DEPTH):
        lambda_init = 0.8 - 0.6 * math.exp(-0.3 * l)
        w_in_l = w_in[l].astype(BF16)
        qkv = _proj(xin, w_in_l, 0, qkv_blocks, BF16, scaled_blocks=ATTN_WIDTH // PROJ_TN, scale=Q_SCALE)
        u = _proj(xin, w_in_l, qkv_blocks, u_blocks, F32)
        gates = _proj(xin, w_in_l, qkv_blocks + u_blocks, gate_blocks, F32)
        attn = _attention(qkv.reshape(b, s, 3 * ATTN_WIDTH), bias_tiles, lambdas[l], subln_w[l], lambda_init)
        hf, hb = _mixer_tail(attn.reshape(t, ATTN_WIDTH), u, gates, xf, pool_w[l].astype(BF16), pool_scale[l],
                             w_branch_attn[l].astype(BF16), w_branch_pool[l].astype(BF16), w_out[l].astype(BF16),
                             ln1_g[l], ln1_b[l], s)
        if l % 2 == 0:
            xf, xin = _dense_ffn(hb, hf, dense_w_gate[l // 2].astype(BF16), dense_w_up[l // 2].astype(BF16),
                                 dense_w_down[l // 2].astype(BF16), ln2_g[l], ln2_b[l])
        else:
            xf = _moe(hf, router_w[l // 2], moe_w_gate[l // 2].astype(BF16), moe_w_up[l // 2].astype(BF16),
                      moe_w_down[l // 2].astype(BF16), ln2_g[l], ln2_b[l])
            xin = xf
    return xf.reshape(b, s, d)
```

```python
import functools
import math

import jax
import jax.numpy as jnp
from jax import lax
from jax.experimental import pallas as pl
from jax.experimental.pallas import tpu as pltpu

D_MODEL = 2048
DEPTH = 2
N_HEADS = 8
HEAD_DIM = 64
HEAD_WIDTH = 2 * HEAD_DIM
ATTN_WIDTH = N_HEADS * HEAD_WIDTH
POOL_WINDOWS = (2, 4, 8, 16)
POOL_GROUP = 256
POOL_WIDTH = POOL_GROUP * len(POOL_WINDOWS)
MAX_WINDOW = max(POOL_WINDOWS)
N_BUCKETS = 32
MAX_DISTANCE = 128
N_EXPERTS = 8
TOP_K = 2
ALPHA = (2.0 * DEPTH) ** 0.25
LN_EPS = 1e-5
LOG2E = math.log2(math.e)
Q_SCALE = HEAD_DIM ** -0.5 * LOG2E
ONES_ROWS = 16

LANES = 128
MASK_VALUE = -1e30
VMEM_LIMIT = 56 * 1024 * 1024

PROJ_TM = 512
PROJ_TN = 1024
ATTN_TQ = 512
ATTN_TK = 512
TAIL_TM = 256
FFN_TM = 512
FFN_TF = 512
ROUTER_TM = 512
DISPATCH_TM = 512
MOE_TM = 512
COMBINE_TM = 256

F32 = jnp.float32
BF16 = jnp.bfloat16


def _params(*sem):
    return pltpu.CompilerParams(dimension_semantics=sem, vmem_limit_bytes=VMEM_LIMIT)


def _layer_norm(r, g, b):
    mu = jnp.mean(r, axis=-1, keepdims=True)
    c = r - mu
    var = jnp.mean(c * c, axis=-1, keepdims=True)
    return c * lax.rsqrt(var + LN_EPS) * g + b


PROJ_HALF = 2 * D_MODEL
assert 3 * ATTN_WIDTH + POOL_WIDTH == PROJ_HALF


def _proj_mix_kernel(x_ref, w_ref, qkv_ref, u_ref):
    x = x_ref[...].astype(BF16)
    for j in range(PROJ_HALF // PROJ_TN):
        cols = slice(j * PROJ_TN, (j + 1) * PROJ_TN)
        acc = jnp.dot(x, w_ref[:, cols], preferred_element_type=F32)
        if cols.stop <= ATTN_WIDTH:
            qkv_ref[:, cols] = (acc * Q_SCALE).astype(BF16)
        elif cols.stop <= 3 * ATTN_WIDTH:
            qkv_ref[:, cols] = acc.astype(BF16)
        else:
            u_ref[:, j * PROJ_TN - 3 * ATTN_WIDTH:(j + 1) * PROJ_TN - 3 * ATTN_WIDTH] = acc


def _proj_gates_kernel(x_ref, w_ref, g_ref):
    x = x_ref[...].astype(BF16)
    for j in range(PROJ_HALF // PROJ_TN):
        cols = slice(j * PROJ_TN, (j + 1) * PROJ_TN)
        g_ref[:, cols] = jax.nn.sigmoid(jnp.dot(x, w_ref[:, cols], preferred_element_type=F32))


def _in_proj(x, w):
    t, k = x.shape
    tm = PROJ_TM
    assert ATTN_WIDTH % PROJ_TN == 0 and POOL_WIDTH % PROJ_TN == 0
    x_spec = pl.BlockSpec((tm, k), lambda i: (i, 0))

    def w_spec(half):
        return pl.BlockSpec((k, PROJ_HALF), lambda i: (0, half), pipeline_mode=pl.Buffered(1))

    qkv, u = pl.pallas_call(
        _proj_mix_kernel,
        grid=(t // tm,),
        in_specs=[x_spec, w_spec(0)],
        out_specs=[
            pl.BlockSpec((tm, 3 * ATTN_WIDTH), lambda i: (i, 0)),
            pl.BlockSpec((tm, POOL_WIDTH), lambda i: (i, 0)),
        ],
        out_shape=[
            jax.ShapeDtypeStruct((t, 3 * ATTN_WIDTH), BF16),
            jax.ShapeDtypeStruct((t, POOL_WIDTH), F32),
        ],
        compiler_params=_params("parallel"),
        name="in_proj_mix",
    )(x, w)
    gates = pl.pallas_call(
        _proj_gates_kernel,
        grid=(t // tm,),
        in_specs=[x_spec, w_spec(1)],
        out_specs=pl.BlockSpec((tm, PROJ_HALF), lambda i: (i, 0)),
        out_shape=jax.ShapeDtypeStruct((t, PROJ_HALF), F32),
        compiler_params=_params("parallel"),
        name="in_proj_gates",
    )(x, w)
    return qkv, u, gates


def _bias_kernel(rb_ref, o_ref):
    h = pl.program_id(0)
    tk, tq = o_ref.shape[2], o_ref.shape[3]
    kpos = lax.broadcasted_iota(jnp.int32, (tk, tq), 0)
    qpos = lax.broadcasted_iota(jnp.int32, (tk, tq), 1)
    max_exact = N_BUCKETS // 2
    far = rb_ref[h * N_BUCKETS + N_BUCKETS - 1]
    for blk in range(2):
        dist = qpos - kpos + blk * tk
        n = jnp.maximum(dist, 0)
        large = max_exact + (
            jnp.log(jnp.maximum(n, 1).astype(F32) / max_exact) / math.log(MAX_DISTANCE / max_exact)
            * (N_BUCKETS - max_exact)
        ).astype(jnp.int32)
        large = jnp.minimum(large, N_BUCKETS - 1)
        bucket = jnp.where(n < max_exact, n, large)
        bias = jnp.zeros((tk, tq), F32)
        for b in range(N_BUCKETS):
            bias = jnp.where(bucket == b, rb_ref[h * N_BUCKETS + b], bias)
        bias = (bias - far) * LOG2E
        if blk == 0:
            bias = jnp.where(dist >= 0, bias, MASK_VALUE)
        o_ref[0, blk] = bias


def _bias_tiles(rel_bias):
    assert ATTN_TK >= MAX_DISTANCE
    rb = jnp.transpose(rel_bias).reshape(-1)
    return pl.pallas_call(
        _bias_kernel,
        grid_spec=pltpu.PrefetchScalarGridSpec(
            num_scalar_prefetch=1,
            grid=(N_HEADS,),
            in_specs=[],
            out_specs=pl.BlockSpec((1, 2, ATTN_TK, ATTN_TQ), lambda h, rb: (h, 0, 0, 0)),
        ),
        out_shape=jax.ShapeDtypeStruct((N_HEADS, 2, ATTN_TK, ATTN_TQ), F32),
        compiler_params=_params("arbitrary"),
        name="bias_tiles",
    )(rb)


def _attn_kernel(q_ref, k_ref, v_ref, bias_ref, lam_ref, sub_ref, o_ref, qt_ref, vt_ref, m_ref, acc_ref,
                 s0_ref, pp_ref, pa_ref, *, lambda_init):
    qi = pl.program_id(2)
    tq = q_ref.shape[1]
    n_kb, _, tk = vt_ref.shape

    @pl.when(qi == 0)
    def _():
        row = lax.broadcasted_iota(jnp.int32, (ONES_ROWS, tk), 0)
        ones_rows = jnp.where(row == 0, 1.0, 0.0).astype(BF16)
        for kb in range(n_kb):
            vt_ref[kb, 0:HEAD_WIDTH, :] = v_ref[0, kb * tk:(kb + 1) * tk, :].astype(F32).T.astype(BF16)
            vt_ref[kb, HEAD_WIDTH:HEAD_WIDTH + ONES_ROWS, :] = ones_rows

    qt = q_ref[0].astype(F32).T
    dim = lax.broadcasted_iota(jnp.int32, (HEAD_WIDTH, tq), 0)
    qt_ref[0] = jnp.where(dim < HEAD_DIM, qt, 0.0).astype(BF16)
    qt_ref[1] = jnp.where(dim >= HEAD_DIM, qt, 0.0).astype(BF16)

    m_ref[...] = jnp.full(m_ref.shape, MASK_VALUE, F32)
    acc_ref[...] = jnp.zeros(acc_ref.shape, F32)

    half = tq // 2
    chunks = [(c, slice(hf * half, (hf + 1) * half)) for c in range(2) for hf in range(2)]

    def keys(kb):
        return k_ref[0, pl.ds(pl.multiple_of(kb * tk, tk), tk), :]

    def key_rows(kind, chunk):
        return tk // 2 if kind == "diag" and chunk[1].start == 0 else tk

    def scores(kk, chunk, kind="far"):
        c, cols = chunk
        return jnp.dot(kk[:key_rows(kind, chunk)], qt_ref[c, :, cols], preferred_element_type=F32)

    def add_bias(s, chunk, kind):
        cols = chunk[1]
        if kind == "diag":
            return s + bias_ref[0, 0, 0:s.shape[0], cols]
        if kind == "near" and cols.start == 0:
            r0 = tk - MAX_DISTANCE
            return jnp.concatenate([s[:r0], s[r0:] + bias_ref[0, 1, r0:tk, cols]], axis=0)
        return s

    def softmax_update(s, chunk, kind):
        c, cols = chunk
        s = add_bias(s, chunk, kind)
        m_prev = m_ref[c, :, cols]
        m_new = jnp.maximum(m_prev, jnp.max(s, axis=0, keepdims=True))
        alpha = jnp.exp2(m_prev - m_new)
        p = jnp.exp2(s - m_new)
        m_ref[c, :, cols] = m_new
        return p.astype(BF16), alpha

    def accumulate(vt, p, alpha, chunk):
        c, cols = chunk
        pv = jnp.dot(vt[:, :p.shape[0]], p, preferred_element_type=F32)
        acc_ref[c, :, cols] = alpha * acc_ref[c, :, cols] + pv

    def accumulate_pending(kb):
        accumulate(vt_ref[kb], pp_ref[...], pa_ref[...], chunks[3])

    def process(kb, kind):
        kk = keys(kb)
        vt = vt_ref[kb]
        s1 = scores(kk, chunks[1], kind)
        accumulate_pending(jnp.maximum(kb - 1, 0))
        p0, a0 = softmax_update(s0_ref[0:key_rows(kind, chunks[0]), :], chunks[0], kind)
        s2 = scores(kk, chunks[2], kind)
        accumulate(vt, p0, a0, chunks[0])
        p1, a1 = softmax_update(s1, chunks[1], kind)
        s3 = scores(kk, chunks[3], kind)
        accumulate(vt, p1, a1, chunks[1])
        p2, a2 = softmax_update(s2, chunks[2], kind)
        if kind != "diag":
            s0_ref[...] = scores(keys(kb + 1), chunks[0])
        accumulate(vt, p2, a2, chunks[2])
        pp_ref[...], pa_ref[...] = softmax_update(s3, chunks[3], kind)

    s0_ref[...] = scores(keys(0), chunks[0])
    pp_ref[...] = jnp.zeros(pp_ref.shape, BF16)
    pa_ref[...] = jnp.ones(pa_ref.shape, F32)

    n_far = jnp.maximum(qi - 1, 0)

    def far_pair(j, carry):
        process(2 * j, "far")
        process(2 * j + 1, "far")
        return carry

    lax.fori_loop(0, n_far // 2, far_pair, 0)

    @pl.when(n_far % 2 == 1)
    def _():
        process(n_far - 1, "far")

    @pl.when(qi >= 1)
    def _():
        process(qi - 1, "near")

    process(qi, "diag")
    accumulate_pending(qi)

    lv = lam_ref[...]
    lam = (
        jnp.exp(jnp.sum(lv[0:1] * lv[1:2], axis=-1, keepdims=True))
        - jnp.exp(jnp.sum(lv[2:3] * lv[3:4], axis=-1, keepdims=True))
        + lambda_init
    )
    num = [acc_ref[c, 0:HEAD_WIDTH, :] for c in range(2)]
    den = [acc_ref[c, HEAD_WIDTH:HEAD_WIDTH + 1, :] for c in range(2)]
    ot = num[0] / den[0] - lam * (num[1] / den[1])
    ot = ot * lax.rsqrt(jnp.mean(ot * ot, axis=0, keepdims=True) + LN_EPS)
    o = ot.T * (sub_ref[...] * (1.0 - lambda_init))
    o_ref[0] = o.astype(o_ref.dtype)


def _attention(qkv, bias_tiles, lam_vec, subln_w, lambda_init):
    b, s, _ = qkv.shape
    tq, tk = ATTN_TQ, ATTN_TK
    assert tq == tk and s % tq == 0
    assert tq // 2 >= MAX_DISTANCE and (tk - MAX_DISTANCE) % 8 == 0
    kernel = functools.partial(_attn_kernel, lambda_init=lambda_init)
    return pl.pallas_call(
        kernel,
        grid=(b, N_HEADS, s // tq),
        in_specs=[
            pl.BlockSpec((1, tq, HEAD_WIDTH), lambda bi, h, qi: (bi, qi, h)),
            pl.BlockSpec((1, s, HEAD_WIDTH), lambda bi, h, qi: (bi, 0, N_HEADS + h)),
            pl.BlockSpec((1, s, HEAD_WIDTH), lambda bi, h, qi: (bi, 0, 2 * N_HEADS + h)),
            pl.BlockSpec((1, 2, tk, tq), lambda bi, h, qi: (h, 0, 0, 0)),
            pl.BlockSpec((4, HEAD_DIM), lambda bi, h, qi: (0, 0)),
            pl.BlockSpec((1, HEAD_WIDTH), lambda bi, h, qi: (0, 0)),
        ],
        out_specs=pl.BlockSpec((1, tq, HEAD_WIDTH), lambda bi, h, qi: (bi, qi, h)),
        out_shape=jax.ShapeDtypeStruct((b, s, ATTN_WIDTH), BF16),
        scratch_shapes=[
            pltpu.VMEM((2, HEAD_WIDTH, tq), BF16),
            pltpu.VMEM((s // tk, HEAD_WIDTH + ONES_ROWS, tk), BF16),
            pltpu.VMEM((2, 1, tq), F32),
            pltpu.VMEM((2, HEAD_WIDTH + ONES_ROWS, tq), F32),
            pltpu.VMEM((tk, tq // 2), F32),
            pltpu.VMEM((tk, tq // 2), BF16),
            pltpu.VMEM((1, tq // 2), F32),
        ],
        compiler_params=_params("parallel", "parallel", "arbitrary"),
        name="diff_attention",
    )(qkv, qkv, qkv, bias_tiles, lam_vec, subln_w.reshape(1, HEAD_WIDTH))


def _tail_kernel(attn_ref, u_ref, halo_ref, ga_ref, gp_ref, x_ref, pw_ref, ps_ref, wba_ref, wbp_ref, wout_ref,
                 g_ref, b_ref, of_ref, ob_ref, ubuf, *, tiles_per_seq):
    i = pl.program_id(0)
    tm = u_ref.shape[0]
    tile_in_seq = i % tiles_per_seq
    halo = jnp.where(tile_in_seq == 0, 0.0, halo_ref[...])
    ubuf[0:MAX_WINDOW, :] = halo
    ubuf[MAX_WINDOW:MAX_WINDOW + tm, :] = u_ref[...]
    t = tile_in_seq * tm + lax.broadcasted_iota(jnp.int32, (tm, 1), 0)

    pooled = []
    for g, w in enumerate(POOL_WINDOWS):
        cs = slice(g * POOL_GROUP, (g + 1) * POOL_GROUP)
        cur = ubuf[MAX_WINDOW:MAX_WINDOW + tm, cs]
        win_sum = cur
        for j in range(1, w):
            win_sum = win_sum + ubuf[MAX_WINDOW - j:MAX_WINDOW - j + tm, cs]
        cnt = jnp.minimum(t + 1, w).astype(F32)
        z = (win_sum / cnt - cur).astype(BF16)
        y = jnp.dot(z, pw_ref[g], preferred_element_type=F32)
        pooled.append((y * ps_ref[:, cs]).astype(BF16))
    pool = jnp.concatenate(pooled, axis=-1)

    a = jnp.dot(attn_ref[...], wba_ref[...], preferred_element_type=F32)
    p = jnp.dot(pool, wbp_ref[...], preferred_element_type=F32)
    merged = ga_ref[...] * a + gp_ref[...] * p
    mix = jnp.dot(merged.astype(BF16), wout_ref[...], preferred_element_type=F32)
    out = _layer_norm(ALPHA * x_ref[...] + mix, g_ref[...], b_ref[...])
    of_ref[...] = out
    ob_ref[...] = out.astype(BF16)


def _resident(shape):
    zeros = (0,) * len(shape)
    return pl.BlockSpec(shape, lambda i: zeros, pipeline_mode=pl.Buffered(1))


def _mixer_tail(attn, u, gates, x, pool_w, pool_scale, w_ba, w_bp, w_out, ln_g, ln_b, seq):
    t = x.shape[0]
    tm = TAIL_TM
    assert seq % tm == 0 and tm % MAX_WINDOW == 0
    halo_blocks = tm // MAX_WINDOW
    kernel = functools.partial(_tail_kernel, tiles_per_seq=seq // tm)
    return pl.pallas_call(
        kernel,
        grid=(t // tm,),
        in_specs=[
            pl.BlockSpec((tm, ATTN_WIDTH), lambda i: (i, 0)),
            pl.BlockSpec((tm, POOL_WIDTH), lambda i: (i, 0)),
            pl.BlockSpec((MAX_WINDOW, POOL_WIDTH), lambda i: (jnp.maximum(i * halo_blocks - 1, 0), 0)),
            pl.BlockSpec((tm, D_MODEL), lambda i: (i, 0)),
            pl.BlockSpec((tm, D_MODEL), lambda i: (i, 1)),
            pl.BlockSpec((tm, D_MODEL), lambda i: (i, 0)),
            _resident(pool_w.shape),
            _resident((1, POOL_WIDTH)),
            _resident(w_ba.shape),
            _resident(w_bp.shape),
            _resident(w_out.shape),
            _resident((1, D_MODEL)),
            _resident((1, D_MODEL)),
        ],
        out_specs=[
            pl.BlockSpec((tm, D_MODEL), lambda i: (i, 0)),
            pl.BlockSpec((tm, D_MODEL), lambda i: (i, 0)),
        ],
        out_shape=[
            jax.ShapeDtypeStruct((t, D_MODEL), F32),
            jax.ShapeDtypeStruct((t, D_MODEL), BF16),
        ],
        scratch_shapes=[pltpu.VMEM((MAX_WINDOW + tm, POOL_WIDTH), F32)],
        compiler_params=_params("parallel"),
        name="mixer_tail",
    )(attn, u, u, gates, gates, x, pool_w, pool_scale.reshape(1, POOL_WIDTH), w_ba, w_bp, w_out,
      ln_g.reshape(1, D_MODEL), ln_b.reshape(1, D_MODEL))


def _swiglu(hb, wg, wu):
    g = jnp.dot(hb, wg, preferred_element_type=F32)
    u = jnp.dot(hb, wu, preferred_element_type=F32)
    return (g * jax.nn.sigmoid(g) * u).astype(BF16)


def _ffn_pipeline_steps(f, n_f, enabled, load_x, wg_ref, wu_ref, wd_ref, a_ref, acc_ref):
    def up(slot):
        a_ref[slot] = _swiglu(load_x(), wg_ref[...], wu_ref[...])

    def down(slot):
        acc_ref[...] += jnp.dot(a_ref[slot], wd_ref[...], preferred_element_type=F32)

    @pl.when(enabled & (f == 0))
    def _():
        acc_ref[...] = jnp.zeros(acc_ref.shape, F32)
        up(0)

    for parity in range(2):
        @pl.when(enabled & (f > 0) & (f < n_f) & (f % 2 == parity))
        def _():
            up(parity)
            down(1 - parity)


def _ffn_pipeline_last(n_f, wd_ref, a_ref, acc_ref):
    return acc_ref[...] + jnp.dot(a_ref[(n_f - 1) % 2], wd_ref[...], preferred_element_type=F32)


def _dense_ffn_kernel(hb_ref, hf_ref, wg_ref, wu_ref, wd_ref, g_ref, b_ref, of_ref, ob_ref, a_ref, acc_ref):
    f = pl.program_id(1)
    n_f = pl.num_programs(1) - 1
    _ffn_pipeline_steps(f, n_f, True, lambda: hb_ref[...], wg_ref, wu_ref, wd_ref, a_ref, acc_ref)

    @pl.when(f == n_f)
    def _():
        y = _ffn_pipeline_last(n_f, wd_ref, a_ref, acc_ref)
        out = _layer_norm(ALPHA * hf_ref[...] + y, g_ref[...], b_ref[...])
        of_ref[...] = out
        ob_ref[...] = out.astype(BF16)


def _dense_ffn(hb, hf, wg, wu, wd, ln_g, ln_b):
    t = hb.shape[0]
    d_ff = wg.shape[1]
    tm, tf = FFN_TM, FFN_TF
    assert d_ff % tf == 0
    n_f = d_ff // tf
    return pl.pallas_call(
        _dense_ffn_kernel,
        grid=(t // tm, n_f + 1),
        in_specs=[
            pl.BlockSpec((tm, D_MODEL), lambda i, f: (i, 0)),
            pl.BlockSpec((tm, D_MODEL), lambda i, f: (i, 0)),
            pl.BlockSpec((D_MODEL, tf), lambda i, f: (0, jnp.minimum(f, n_f - 1))),
            pl.BlockSpec((D_MODEL, tf), lambda i, f: (0, jnp.minimum(f, n_f - 1))),
            pl.BlockSpec((tf, D_MODEL), lambda i, f: (jnp.maximum(f - 1, 0), 0)),
            pl.BlockSpec((1, D_MODEL), lambda i, f: (0, 0)),
            pl.BlockSpec((1, D_MODEL), lambda i, f: (0, 0)),
        ],
        out_specs=[
            pl.BlockSpec((tm, D_MODEL), lambda i, f: (i, 0)),
            pl.BlockSpec((tm, D_MODEL), lambda i, f: (i, 0)),
        ],
        out_shape=[
            jax.ShapeDtypeStruct((t, D_MODEL), F32),
            jax.ShapeDtypeStruct((t, D_MODEL), BF16),
        ],
        scratch_shapes=[pltpu.VMEM((2, tm, tf), BF16), pltpu.VMEM((tm, D_MODEL), F32)],
        compiler_params=_params("parallel", "arbitrary"),
        name="dense_ffn",
    )(hb, hf, wg, wu, wd, ln_g.reshape(1, D_MODEL), ln_b.reshape(1, D_MODEL))


ROUTE_W0, ROUTE_W1, ROUTE_E0, ROUTE_E1, ROUTE_R0, ROUTE_R1 = range(6)


def _router_kernel(h_ref, rw_ref, route_ref, counts_ref, carry_ref):
    i = pl.program_id(0)
    tm = h_ref.shape[0]

    @pl.when(i == 0)
    def _():
        carry_ref[...] = jnp.zeros(carry_ref.shape, F32)

    logits = jnp.dot(h_ref[...], rw_ref[...], preferred_element_type=F32, precision=lax.Precision.HIGHEST)
    lane = lax.broadcasted_iota(jnp.int32, (tm, LANES), 1)
    lg = jnp.where(lane < N_EXPERTS, logits, -jnp.inf)
    m1 = jnp.max(lg, axis=-1, keepdims=True)
    e1 = jnp.min(jnp.where(lg == m1, lane, LANES), axis=-1, keepdims=True)
    lg2 = jnp.where(lane == e1, -jnp.inf, lg)
    m2 = jnp.max(lg2, axis=-1, keepdims=True)
    e2 = jnp.min(jnp.where(lg2 == m2, lane, LANES), axis=-1, keepdims=True)
    x2 = jnp.exp(m2 - m1)
    w1 = 1.0 / (1.0 + x2)
    w2 = x2 / (1.0 + x2)

    sel1 = lane == e1
    sel2 = lane == e2
    mask = jnp.where(sel1 | sel2, 1.0, 0.0)
    r = lax.broadcasted_iota(jnp.int32, (tm, tm), 0)
    c = lax.broadcasted_iota(jnp.int32, (tm, tm), 1)
    tri = jnp.where(r >= c, 1.0, 0.0).astype(BF16)
    incl = jnp.dot(tri, mask.astype(BF16), preferred_element_type=F32)
    rank = incl - mask + carry_ref[...]
    total = carry_ref[...] + jnp.sum(mask, axis=0, keepdims=True)
    carry_ref[...] = total
    counts_ref[...] = total

    r1 = jnp.sum(jnp.where(sel1, rank, 0.0), axis=-1, keepdims=True)
    r2 = jnp.sum(jnp.where(sel2, rank, 0.0), axis=-1, keepdims=True)
    cols = (w1, w2, e1.astype(F32), e2.astype(F32), r1, r2)
    route = jnp.zeros((tm, LANES), F32)
    for idx, val in enumerate(cols):
        route = jnp.where(lane == idx, val, route)
    route_ref[...] = route


def _router(hf, router_w):
    t = hf.shape[0]
    tm = ROUTER_TM
    rw = jnp.pad(router_w, ((0, 0), (0, LANES - N_EXPERTS)))
    return pl.pallas_call(
        _router_kernel,
        grid=(t // tm,),
        in_specs=[
            pl.BlockSpec((tm, D_MODEL), lambda i: (i, 0)),
            pl.BlockSpec((D_MODEL, LANES), lambda i: (0, 0)),
        ],
        out_specs=[
            pl.BlockSpec((tm, LANES), lambda i: (i, 0)),
            pl.BlockSpec((1, LANES), lambda i: (0, 0)),
        ],
        out_shape=[
            jax.ShapeDtypeStruct((t, LANES), F32),
            jax.ShapeDtypeStruct((1, LANES), F32),
        ],
        scratch_shapes=[pltpu.VMEM((1, LANES), F32)],
        compiler_params=_params("arbitrary"),
        name="moe_router",
    )(hf, rw)


def _dispatch_kernel(dest_ref, h_ref, xs_in_hbm, xs_hbm, sem):
    del xs_in_hbm
    n = h_ref.shape[0]

    def row_copy(r, k):
        return pltpu.make_async_copy(
            h_ref.at[pl.ds(r, 1)], xs_hbm.at[pl.ds(dest_ref[0, 0, TOP_K * r + k], 1)], sem)

    def start(r, carry):
        for k in range(TOP_K):
            row_copy(r, k).start()
        return carry

    def wait(r, carry):
        for k in range(TOP_K):
            row_copy(r, k).wait()
        return carry

    lax.fori_loop(0, n, start, 0)
    lax.fori_loop(0, n, wait, 0)


def _dispatch(hf, dest, n_rows):
    t = hf.shape[0]
    tm = DISPATCH_TM
    xs0 = jnp.zeros((n_rows, D_MODEL), F32)
    return pl.pallas_call(
        _dispatch_kernel,
        grid=(t // tm,),
        in_specs=[
            pl.BlockSpec((1, 1, TOP_K * tm), lambda i: (i, 0, 0), memory_space=pltpu.SMEM),
            pl.BlockSpec((tm, D_MODEL), lambda i: (i, 0)),
            pl.BlockSpec(memory_space=pl.ANY),
        ],
        out_specs=pl.BlockSpec(memory_space=pl.ANY),
        out_shape=jax.ShapeDtypeStruct((n_rows, D_MODEL), F32),
        scratch_shapes=[pltpu.SemaphoreType.DMA(())],
        input_output_aliases={2: 0},
        compiler_params=_params("arbitrary"),
        name="moe_dispatch",
    )(dest.reshape(t // tm, 1, TOP_K * tm), hf, xs0)


def _moe_ffn_kernel(te_ref, ok_ref, xs_ref, wg_ref, wu_ref, wd_ref, ys_ref, xb_ref, a_ref, acc_ref):
    i = pl.program_id(0)
    f = pl.program_id(1)
    n_f = pl.num_programs(1) - 1
    ok = ok_ref[i] == 1

    @pl.when(ok & (f == 0))
    def _():
        xb_ref[...] = xs_ref[...].astype(BF16)

    _ffn_pipeline_steps(f, n_f, ok, lambda: xb_ref[...], wg_ref, wu_ref, wd_ref, a_ref, acc_ref)

    @pl.when(ok & (f == n_f))
    def _():
        ys_ref[...] = _ffn_pipeline_last(n_f, wd_ref, a_ref, acc_ref)

    @pl.when(jnp.logical_not(ok) & (f == n_f))
    def _():
        ys_ref[...] = jnp.zeros(ys_ref.shape, F32)


def _moe_ffn(xs, tile_expert, tile_ok, wg, wu, wd):
    n_rows = xs.shape[0]
    d_ff = wg.shape[2]
    tm, tf = MOE_TM, FFN_TF
    n_f = d_ff // tf
    n_tiles = n_rows // tm

    def up_idx(i, f, ok):
        return jnp.where(ok[i] == 1, jnp.minimum(f, n_f - 1), n_f - 1)

    def down_idx(i, f, ok):
        return jnp.where(ok[i] == 1, jnp.maximum(f - 1, 0), n_f - 1)

    return pl.pallas_call(
        _moe_ffn_kernel,
        grid_spec=pltpu.PrefetchScalarGridSpec(
            num_scalar_prefetch=2,
            grid=(n_tiles, n_f + 1),
            in_specs=[
                pl.BlockSpec((tm, D_MODEL), lambda i, f, te, ok: (i, 0)),
                pl.BlockSpec((None, D_MODEL, tf), lambda i, f, te, ok: (te[i], 0, up_idx(i, f, ok))),
                pl.BlockSpec((None, D_MODEL, tf), lambda i, f, te, ok: (te[i], 0, up_idx(i, f, ok))),
                pl.BlockSpec((None, tf, D_MODEL), lambda i, f, te, ok: (te[i], down_idx(i, f, ok), 0)),
            ],
            out_specs=pl.BlockSpec((tm, D_MODEL), lambda i, f, te, ok: (i, 0)),
            scratch_shapes=[
                pltpu.VMEM((tm, D_MODEL), BF16),
                pltpu.VMEM((2, tm, tf), BF16),
                pltpu.VMEM((tm, D_MODEL), F32),
            ],
        ),
        out_shape=jax.ShapeDtypeStruct((n_rows, D_MODEL), F32),
        compiler_params=_params("arbitrary", "arbitrary"),
        name="moe_ffn",
    )(tile_expert, tile_ok, xs, wg, wu, wd)


def _combine_kernel(dest_ref, h_ref, route_ref, ys_hbm, g_ref, b_ref, o_ref, y_ref, sem):
    tm = h_ref.shape[0]

    def row_copy(r, k):
        return pltpu.make_async_copy(
            ys_hbm.at[pl.ds(dest_ref[0, 0, TOP_K * r + k], 1)], y_ref.at[k, pl.ds(r, 1)], sem)

    def start(r, carry):
        for k in range(TOP_K):
            row_copy(r, k).start()
        return carry

    def wait(r, carry):
        for k in range(TOP_K):
            row_copy(r, k).wait()
        return carry

    lax.fori_loop(0, tm, start, 0)
    lax.fori_loop(0, tm, wait, 0)

    route = route_ref[...]
    w1 = route[:, ROUTE_W0:ROUTE_W0 + 1]
    w2 = route[:, ROUTE_W1:ROUTE_W1 + 1]
    y = w1 * y_ref[0] + w2 * y_ref[1]
    o_ref[...] = _layer_norm(ALPHA * h_ref[...] + y, g_ref[...], b_ref[...])


def _combine(hf, route, dest, ys, ln_g, ln_b):
    t = hf.shape[0]
    tm = COMBINE_TM
    return pl.pallas_call(
        _combine_kernel,
        grid=(t // tm,),
        in_specs=[
            pl.BlockSpec((1, 1, TOP_K * tm), lambda i: (i, 0, 0), memory_space=pltpu.SMEM),
            pl.BlockSpec((tm, D_MODEL), lambda i: (i, 0)),
            pl.BlockSpec((tm, LANES), lambda i: (i, 0)),
            pl.BlockSpec(memory_space=pl.ANY),
            pl.BlockSpec((1, D_MODEL), lambda i: (0, 0)),
            pl.BlockSpec((1, D_MODEL), lambda i: (0, 0)),
        ],
        out_specs=pl.BlockSpec((tm, D_MODEL), lambda i: (i, 0)),
        out_shape=jax.ShapeDtypeStruct((t, D_MODEL), F32),
        scratch_shapes=[pltpu.VMEM((TOP_K, tm, D_MODEL), F32), pltpu.SemaphoreType.DMA(())],
        compiler_params=_params("arbitrary"),
        name="moe_combine",
    )(dest.reshape(t // tm, 1, TOP_K * tm), hf, route, ys, ln_g.reshape(1, D_MODEL), ln_b.reshape(1, D_MODEL))


def _moe(hf, router_w, wg, wu, wd, ln_g, ln_b):
    t = hf.shape[0]
    tm = MOE_TM
    n_tiles = (TOP_K * t) // tm + N_EXPERTS
    route, counts = _router(hf, router_w)

    counts = counts[0, :N_EXPERTS].astype(jnp.int32)
    tiles_per_expert = (counts + tm - 1) // tm
    tile_end = jnp.cumsum(tiles_per_expert)
    group_start = (tile_end - tiles_per_expert) * tm
    tile_ids = jnp.arange(n_tiles, dtype=jnp.int32)
    n_used = tile_end[-1]
    last_used = jnp.maximum(n_used - 1, 0)
    expert_of_tile = jnp.sum(tile_ids[:, None] >= tile_end[None, :], axis=1).astype(jnp.int32)
    tile_ok = (tile_ids < n_used).astype(jnp.int32)
    tile_expert = jnp.where(tile_ok == 1, expert_of_tile, expert_of_tile[last_used])
    tile_expert = jnp.minimum(tile_expert, N_EXPERTS - 1)

    experts = route[:, ROUTE_E0:ROUTE_E1 + 1].astype(jnp.int32)
    ranks = route[:, ROUTE_R0:ROUTE_R1 + 1].astype(jnp.int32)
    dest = (group_start[experts] + ranks).reshape(-1)

    xs = _dispatch(hf, dest, n_tiles * tm)
    ys = _moe_ffn(xs, tile_expert, tile_ok, wg, wu, wd)
    return _combine(hf, route, dest, ys, ln_g, ln_b)


def kernel(x, w_in, lambdas, subln_w, pool_w, pool_scale, w_branch_attn, w_branch_pool, w_out, rel_bias,
           ln1_g, ln1_b, dense_w_gate, dense_w_up, dense_w_down, router_w, moe_w_gate, moe_w_up, moe_w_down,
           ln2_g, ln2_b):
    b, s, d = x.shape
    t = b * s
    xf = x.reshape(t, d)
    xin = xf
    bias_tiles = _bias_tiles(rel_bias)
    for l in range(DEPTH):
        lambda_init = 0.8 - 0.6 * math.exp(-0.3 * l)
        qkv, u, gates = _in_proj(xin, w_in[l].astype(BF16))
        attn = _attention(qkv.reshape(b, s, 3 * ATTN_WIDTH), bias_tiles, lambdas[l], subln_w[l], lambda_init)
        hf, hb = _mixer_tail(attn.reshape(t, ATTN_WIDTH), u, gates, xf, pool_w[l].astype(BF16), pool_scale[l],
                             w_branch_attn[l].astype(BF16), w_branch_pool[l].astype(BF16), w_out[l].astype(BF16),
                             ln1_g[l], ln1_b[l], s)
        if l % 2 == 0:
            xf, xin = _dense_ffn(hb, hf, dense_w_gate[l // 2].astype(BF16), dense_w_up[l // 2].astype(BF16),
                                 dense_w_down[l // 2].astype(BF16), ln2_g[l], ln2_b[l])
        else:
            xf = _moe(hf, router_w[l // 2], moe_w_gate[l // 2].astype(BF16), moe_w_up[l // 2].astype(BF16),
                      moe_w_down[l // 2].astype(BF16), ln2_g[l], ln2_b[l])
            xin = xf
    return xf.reshape(b, s, d)
```

```python
import functools
import math

import jax
import jax.numpy as jnp
from jax import lax
from jax.experimental import pallas as pl
from jax.experimental.pallas import tpu as pltpu

D_MODEL = 2048
DEPTH = 2
N_HEADS = 8
HEAD_DIM = 64
HEAD_WIDTH = 2 * HEAD_DIM
ATTN_WIDTH = N_HEADS * HEAD_WIDTH
POOL_WINDOWS = (2, 4, 8, 16)
POOL_GROUP = 256
POOL_WIDTH = POOL_GROUP * len(POOL_WINDOWS)
MAX_WINDOW = max(POOL_WINDOWS)
N_BUCKETS = 32
MAX_DISTANCE = 128
N_EXPERTS = 8
TOP_K = 2
ALPHA = (2.0 * DEPTH) ** 0.25
LN_EPS = 1e-5
LOG2E = math.log2(math.e)
Q_SCALE = HEAD_DIM ** -0.5 * LOG2E
ONES_ROWS = 16

LANES = 128
MASK_VALUE = -1e30
VMEM_LIMIT = 56 * 1024 * 1024

PROJ_TM = 512
PROJ_TN = 1024
ATTN_TQ = 512
ATTN_TK = 512
TAIL_TM = 256
FFN_TM = 512
FFN_TF = 512
ROUTER_TM = 512
DISPATCH_TM = 512
MOE_TM = 512
MOE_TF = 1024
COMBINE_TM = 256
ROW_DMA_UNROLL = 8

F32 = jnp.float32
BF16 = jnp.bfloat16


def _params(*sem):
    return pltpu.CompilerParams(dimension_semantics=sem, vmem_limit_bytes=VMEM_LIMIT)


def _layer_norm(r, g, b):
    mu = jnp.mean(r, axis=-1, keepdims=True)
    c = r - mu
    var = jnp.mean(c * c, axis=-1, keepdims=True)
    return c * lax.rsqrt(var + LN_EPS) * g + b


PROJ_HALF = 2 * D_MODEL
assert 3 * ATTN_WIDTH + POOL_WIDTH == PROJ_HALF


def _proj_mix_kernel(x_ref, w_ref, qkv_ref, u_ref):
    x = x_ref[...].astype(BF16)
    for j in range(PROJ_HALF // PROJ_TN):
        cols = slice(j * PROJ_TN, (j + 1) * PROJ_TN)
        acc = jnp.dot(x, w_ref[:, cols], preferred_element_type=F32)
        if cols.stop <= ATTN_WIDTH:
            qkv_ref[:, cols] = (acc * Q_SCALE).astype(BF16)
        elif cols.stop <= 3 * ATTN_WIDTH:
            qkv_ref[:, cols] = acc.astype(BF16)
        else:
            u_ref[:, j * PROJ_TN - 3 * ATTN_WIDTH:(j + 1) * PROJ_TN - 3 * ATTN_WIDTH] = acc


def _proj_gates_kernel(x_ref, w_ref, g_ref):
    x = x_ref[...].astype(BF16)
    for j in range(PROJ_HALF // PROJ_TN):
        cols = slice(j * PROJ_TN, (j + 1) * PROJ_TN)
        g_ref[:, cols] = jax.nn.sigmoid(jnp.dot(x, w_ref[:, cols], preferred_element_type=F32))


def _in_proj(x, w):
    t, k = x.shape
    tm = PROJ_TM
    assert ATTN_WIDTH % PROJ_TN == 0 and POOL_WIDTH % PROJ_TN == 0
    x_spec = pl.BlockSpec((tm, k), lambda i: (i, 0))

    def w_spec(half):
        return pl.BlockSpec((k, PROJ_HALF), lambda i: (0, half), pipeline_mode=pl.Buffered(1))

    qkv, u = pl.pallas_call(
        _proj_mix_kernel,
        grid=(t // tm,),
        in_specs=[x_spec, w_spec(0)],
        out_specs=[
            pl.BlockSpec((tm, 3 * ATTN_WIDTH), lambda i: (i, 0)),
            pl.BlockSpec((tm, POOL_WIDTH), lambda i: (i, 0)),
        ],
        out_shape=[
            jax.ShapeDtypeStruct((t, 3 * ATTN_WIDTH), BF16),
            jax.ShapeDtypeStruct((t, POOL_WIDTH), F32),
        ],
        compiler_params=_params("parallel"),
        name="in_proj_mix",
    )(x, w)
    gates = pl.pallas_call(
        _proj_gates_kernel,
        grid=(t // tm,),
        in_specs=[x_spec, w_spec(1)],
        out_specs=pl.BlockSpec((tm, PROJ_HALF), lambda i: (i, 0)),
        out_shape=jax.ShapeDtypeStruct((t, PROJ_HALF), F32),
        compiler_params=_params("parallel"),
        name="in_proj_gates",
    )(x, w)
    return qkv, u, gates


def _bias_kernel(rb_ref, o_ref):
    h = pl.program_id(0)
    tk, tq = o_ref.shape[2], o_ref.shape[3]
    kpos = lax.broadcasted_iota(jnp.int32, (tk, tq), 0)
    qpos = lax.broadcasted_iota(jnp.int32, (tk, tq), 1)
    max_exact = N_BUCKETS // 2
    far = rb_ref[h * N_BUCKETS + N_BUCKETS - 1]
    for blk in range(2):
        dist = qpos - kpos + blk * tk
        n = jnp.maximum(dist, 0)
        large = max_exact + (
            jnp.log(jnp.maximum(n, 1).astype(F32) / max_exact) / math.log(MAX_DISTANCE / max_exact)
            * (N_BUCKETS - max_exact)
        ).astype(jnp.int32)
        large = jnp.minimum(large, N_BUCKETS - 1)
        bucket = jnp.where(n < max_exact, n, large)
        bias = jnp.zeros((tk, tq), F32)
        for b in range(N_BUCKETS):
            bias = jnp.where(bucket == b, rb_ref[h * N_BUCKETS + b], bias)
        bias = (bias - far) * LOG2E
        if blk == 0:
            bias = jnp.where(dist >= 0, bias, MASK_VALUE)
        o_ref[0, blk] = bias


def _bias_tiles(rel_bias):
    assert ATTN_TK >= MAX_DISTANCE
    rb = jnp.transpose(rel_bias).reshape(-1)
    return pl.pallas_call(
        _bias_kernel,
        grid_spec=pltpu.PrefetchScalarGridSpec(
            num_scalar_prefetch=1,
            grid=(N_HEADS,),
            in_specs=[],
            out_specs=pl.BlockSpec((1, 2, ATTN_TK, ATTN_TQ), lambda h, rb: (h, 0, 0, 0)),
        ),
        out_shape=jax.ShapeDtypeStruct((N_HEADS, 2, ATTN_TK, ATTN_TQ), F32),
        compiler_params=_params("arbitrary"),
        name="bias_tiles",
    )(rb)


def _attn_kernel(q_ref, k_ref, v_ref, bias_ref, lam_ref, sub_ref, o_ref, qt_ref, vt_ref, m_ref, acc_ref,
                 s0_ref, pp_ref, pa_ref, *, lambda_init):
    qi = pl.program_id(2)
    tq = q_ref.shape[1]
    n_kb, _, tk = vt_ref.shape

    @pl.when(qi == 0)
    def _():
        row = lax.broadcasted_iota(jnp.int32, (ONES_ROWS, tk), 0)
        ones_rows = jnp.where(row == 0, 1.0, 0.0).astype(BF16)
        for kb in range(n_kb):
            vt_ref[kb, 0:HEAD_WIDTH, :] = v_ref[0, kb * tk:(kb + 1) * tk, :].astype(F32).T.astype(BF16)
            vt_ref[kb, HEAD_WIDTH:HEAD_WIDTH + ONES_ROWS, :] = ones_rows

    qt = q_ref[0].astype(F32).T
    dim = lax.broadcasted_iota(jnp.int32, (HEAD_WIDTH, tq), 0)
    qt_ref[0] = jnp.where(dim < HEAD_DIM, qt, 0.0).astype(BF16)
    qt_ref[1] = jnp.where(dim >= HEAD_DIM, qt, 0.0).astype(BF16)

    m_ref[...] = jnp.full(m_ref.shape, MASK_VALUE, F32)
    acc_ref[...] = jnp.zeros(acc_ref.shape, F32)

    half = tq // 2
    chunks = [(c, slice(hf * half, (hf + 1) * half)) for c in range(2) for hf in range(2)]

    def keys(kb):
        return k_ref[0, pl.ds(pl.multiple_of(kb * tk, tk), tk), :]

    def key_rows(kind, chunk):
        return tk // 2 if kind == "diag" and chunk[1].start == 0 else tk

    def scores(kk, chunk, kind="far"):
        c, cols = chunk
        return jnp.dot(kk[:key_rows(kind, chunk)], qt_ref[c, :, cols], preferred_element_type=F32)

    def add_bias(s, chunk, kind):
        cols = chunk[1]
        if kind == "diag":
            return s + bias_ref[0, 0, 0:s.shape[0], cols]
        if kind == "near" and cols.start == 0:
            r0 = tk - MAX_DISTANCE
            return jnp.concatenate([s[:r0], s[r0:] + bias_ref[0, 1, r0:tk, cols]], axis=0)
        return s

    def softmax_update(s, chunk, kind):
        c, cols = chunk
        s = add_bias(s, chunk, kind)
        m_prev = m_ref[c, :, cols]
        m_new = jnp.maximum(m_prev, jnp.max(s, axis=0, keepdims=True))
        alpha = jnp.exp2(m_prev - m_new)
        p = jnp.exp2(s - m_new)
        m_ref[c, :, cols] = m_new
        return p.astype(BF16), alpha

    def accumulate(vt, p, alpha, chunk):
        c, cols = chunk
        pv = jnp.dot(vt[:, :p.shape[0]], p, preferred_element_type=F32)
        acc_ref[c, :, cols] = alpha * acc_ref[c, :, cols] + pv

    def accumulate_pending(kb):
        accumulate(vt_ref[kb], pp_ref[...], pa_ref[...], chunks[3])

    def process(kb, kind):
        kk = keys(kb)
        vt = vt_ref[kb]
        s1 = scores(kk, chunks[1], kind)
        accumulate_pending(jnp.maximum(kb - 1, 0))
        p0, a0 = softmax_update(s0_ref[0:key_rows(kind, chunks[0]), :], chunks[0], kind)
        s2 = scores(kk, chunks[2], kind)
        accumulate(vt, p0, a0, chunks[0])
        p1, a1 = softmax_update(s1, chunks[1], kind)
        s3 = scores(kk, chunks[3], kind)
        accumulate(vt, p1, a1, chunks[1])
        p2, a2 = softmax_update(s2, chunks[2], kind)
        if kind != "diag":
            s0_ref[...] = scores(keys(kb + 1), chunks[0])
        accumulate(vt, p2, a2, chunks[2])
        pp_ref[...], pa_ref[...] = softmax_update(s3, chunks[3], kind)

    s0_ref[...] = scores(keys(0), chunks[0])
    pp_ref[...] = jnp.zeros(pp_ref.shape, BF16)
    pa_ref[...] = jnp.ones(pa_ref.shape, F32)

    n_far = jnp.maximum(qi - 1, 0)

    def far_pair(j, carry):
        process(2 * j, "far")
        process(2 * j + 1, "far")
        return carry

    lax.fori_loop(0, n_far // 2, far_pair, 0)

    @pl.when(n_far % 2 == 1)
    def _():
        process(n_far - 1, "far")

    @pl.when(qi >= 1)
    def _():
        process(qi - 1, "near")
        process(qi, "diag")

    @pl.when(qi == 0)
    def _():
        process(0, "diag")

    accumulate_pending(qi)

    lv = lam_ref[...]
    lam = (
        jnp.exp(jnp.sum(lv[0:1] * lv[1:2], axis=-1, keepdims=True))
        - jnp.exp(jnp.sum(lv[2:3] * lv[3:4], axis=-1, keepdims=True))
        + lambda_init
    )
    num = [acc_ref[c, 0:HEAD_WIDTH, :] for c in range(2)]
    den = [acc_ref[c, HEAD_WIDTH:HEAD_WIDTH + 1, :] for c in range(2)]
    ot = num[0] / den[0] - lam * (num[1] / den[1])
    ot = ot * lax.rsqrt(jnp.mean(ot * ot, axis=0, keepdims=True) + LN_EPS)
    o = ot.T * (sub_ref[...] * (1.0 - lambda_init))
    o_ref[0] = o.astype(o_ref.dtype)


def _attention(qkv, bias_tiles, lam_vec, subln_w, lambda_init):
    b, s, _ = qkv.shape
    tq, tk = ATTN_TQ, ATTN_TK
    assert tq == tk and s % tq == 0
    assert tq // 2 >= MAX_DISTANCE and (tk - MAX_DISTANCE) % 8 == 0
    kernel = functools.partial(_attn_kernel, lambda_init=lambda_init)
    return pl.pallas_call(
        kernel,
        grid=(b, N_HEADS, s // tq),
        in_specs=[
            pl.BlockSpec((1, tq, HEAD_WIDTH), lambda bi, h, qi: (bi, qi, h)),
            pl.BlockSpec((1, s, HEAD_WIDTH), lambda bi, h, qi: (bi, 0, N_HEADS + h)),
            pl.BlockSpec((1, s, HEAD_WIDTH), lambda bi, h, qi: (bi, 0, 2 * N_HEADS + h)),
            pl.BlockSpec((1, 2, tk, tq), lambda bi, h, qi: (h, 0, 0, 0)),
            pl.BlockSpec((4, HEAD_DIM), lambda bi, h, qi: (0, 0)),
            pl.BlockSpec((1, HEAD_WIDTH), lambda bi, h, qi: (0, 0)),
        ],
        out_specs=pl.BlockSpec((1, tq, HEAD_WIDTH), lambda bi, h, qi: (bi, qi, h)),
        out_shape=jax.ShapeDtypeStruct((b, s, ATTN_WIDTH), BF16),
        scratch_shapes=[
            pltpu.VMEM((2, HEAD_WIDTH, tq), BF16),
            pltpu.VMEM((s // tk, HEAD_WIDTH + ONES_ROWS, tk), BF16),
            pltpu.VMEM((2, 1, tq), F32),
            pltpu.VMEM((2, HEAD_WIDTH + ONES_ROWS, tq), F32),
            pltpu.VMEM((tk, tq // 2), F32),
            pltpu.VMEM((tk, tq // 2), BF16),
            pltpu.VMEM((1, tq // 2), F32),
        ],
        compiler_params=_params("parallel", "parallel", "arbitrary"),
        name="diff_attention",
    )(qkv, qkv, qkv, bias_tiles, lam_vec, subln_w.reshape(1, HEAD_WIDTH))


def _tail_kernel(attn_ref, u_ref, halo_ref, ga_ref, gp_ref, x_ref, pw_ref, ps_ref, wba_ref, wbp_ref, wout_ref,
                 g_ref, b_ref, of_ref, ob_ref, ubuf, *, tiles_per_seq):
    i = pl.program_id(0)
    tm = u_ref.shape[0]
    tile_in_seq = i % tiles_per_seq
    halo = jnp.where(tile_in_seq == 0, 0.0, halo_ref[...])
    ubuf[0:MAX_WINDOW, :] = halo
    ubuf[MAX_WINDOW:MAX_WINDOW + tm, :] = u_ref[...]
    t = tile_in_seq * tm + lax.broadcasted_iota(jnp.int32, (tm, 1), 0)

    pooled = []
    for g, w in enumerate(POOL_WINDOWS):
        cs = slice(g * POOL_GROUP, (g + 1) * POOL_GROUP)
        cur = ubuf[MAX_WINDOW:MAX_WINDOW + tm, cs]
        win_sum = cur
        for j in range(1, w):
            win_sum = win_sum + ubuf[MAX_WINDOW - j:MAX_WINDOW - j + tm, cs]
        cnt = jnp.minimum(t + 1, w).astype(F32)
        z = (win_sum / cnt - cur).astype(BF16)
        y = jnp.dot(z, pw_ref[g], preferred_element_type=F32)
        pooled.append((y * ps_ref[:, cs]).astype(BF16))
    pool = jnp.concatenate(pooled, axis=-1)

    a = jnp.dot(attn_ref[...], wba_ref[...], preferred_element_type=F32)
    p = jnp.dot(pool, wbp_ref[...], preferred_element_type=F32)
    merged = ga_ref[...] * a + gp_ref[...] * p
    mix = jnp.dot(merged.astype(BF16), wout_ref[...], preferred_element_type=F32)
    out = _layer_norm(ALPHA * x_ref[...] + mix, g_ref[...], b_ref[...])
    of_ref[...] = out
    ob_ref[...] = out.astype(BF16)


def _resident(shape):
    zeros = (0,) * len(shape)
    return pl.BlockSpec(shape, lambda i: zeros, pipeline_mode=pl.Buffered(1))


def _mixer_tail(attn, u, gates, x, pool_w, pool_scale, w_ba, w_bp, w_out, ln_g, ln_b, seq):
    t = x.shape[0]
    tm = TAIL_TM
    assert seq % tm == 0 and tm % MAX_WINDOW == 0
    halo_blocks = tm // MAX_WINDOW
    kernel = functools.partial(_tail_kernel, tiles_per_seq=seq // tm)
    return pl.pallas_call(
        kernel,
        grid=(t // tm,),
        in_specs=[
            pl.BlockSpec((tm, ATTN_WIDTH), lambda i: (i, 0)),
            pl.BlockSpec((tm, POOL_WIDTH), lambda i: (i, 0)),
            pl.BlockSpec((MAX_WINDOW, POOL_WIDTH), lambda i: (jnp.maximum(i * halo_blocks - 1, 0), 0)),
            pl.BlockSpec((tm, D_MODEL), lambda i: (i, 0)),
            pl.BlockSpec((tm, D_MODEL), lambda i: (i, 1)),
            pl.BlockSpec((tm, D_MODEL), lambda i: (i, 0)),
            _resident(pool_w.shape),
            _resident((1, POOL_WIDTH)),
            _resident(w_ba.shape),
            _resident(w_bp.shape),
            _resident(w_out.shape),
            _resident((1, D_MODEL)),
            _resident((1, D_MODEL)),
        ],
        out_specs=[
            pl.BlockSpec((tm, D_MODEL), lambda i: (i, 0)),
            pl.BlockSpec((tm, D_MODEL), lambda i: (i, 0)),
        ],
        out_shape=[
            jax.ShapeDtypeStruct((t, D_MODEL), F32),
            jax.ShapeDtypeStruct((t, D_MODEL), BF16),
        ],
        scratch_shapes=[pltpu.VMEM((MAX_WINDOW + tm, POOL_WIDTH), F32)],
        compiler_params=_params("parallel"),
        name="mixer_tail",
    )(attn, u, u, gates, gates, x, pool_w, pool_scale.reshape(1, POOL_WIDTH), w_ba, w_bp, w_out,
      ln_g.reshape(1, D_MODEL), ln_b.reshape(1, D_MODEL))


def _swiglu(hb, wg, wu):
    g = jnp.dot(hb, wg, preferred_element_type=F32)
    u = jnp.dot(hb, wu, preferred_element_type=F32)
    return (g * jax.nn.sigmoid(g) * u).astype(BF16)


def _dense_ffn_kernel(hb_ref, hf_ref, wg_ref, wu_ref, wd_ref, g_ref, b_ref, of_ref, ob_ref, acc_ref):
    f = pl.program_id(1)

    @pl.when(f == 0)
    def _():
        acc_ref[...] = jnp.zeros(acc_ref.shape, F32)

    a = _swiglu(hb_ref[...], wg_ref[...], wu_ref[...])
    acc_ref[...] += jnp.dot(a, wd_ref[...], preferred_element_type=F32)

    @pl.when(f == pl.num_programs(1) - 1)
    def _():
        out = _layer_norm(ALPHA * hf_ref[...] + acc_ref[...], g_ref[...], b_ref[...])
        of_ref[...] = out
        ob_ref[...] = out.astype(BF16)


def _dense_ffn(hb, hf, wg, wu, wd, ln_g, ln_b):
    t = hb.shape[0]
    d_ff = wg.shape[1]
    tm, tf = FFN_TM, FFN_TF
    assert d_ff % tf == 0
    return pl.pallas_call(
        _dense_ffn_kernel,
        grid=(t // tm, d_ff // tf),
        in_specs=[
            pl.BlockSpec((tm, D_MODEL), lambda i, f: (i, 0)),
            pl.BlockSpec((tm, D_MODEL), lambda i, f: (i, 0)),
            pl.BlockSpec((D_MODEL, tf), lambda i, f: (0, f)),
            pl.BlockSpec((D_MODEL, tf), lambda i, f: (0, f)),
            pl.BlockSpec((tf, D_MODEL), lambda i, f: (f, 0)),
            pl.BlockSpec((1, D_MODEL), lambda i, f: (0, 0)),
            pl.BlockSpec((1, D_MODEL), lambda i, f: (0, 0)),
        ],
        out_specs=[
            pl.BlockSpec((tm, D_MODEL), lambda i, f: (i, 0)),
            pl.BlockSpec((tm, D_MODEL), lambda i, f: (i, 0)),
        ],
        out_shape=[
            jax.ShapeDtypeStruct((t, D_MODEL), F32),
            jax.ShapeDtypeStruct((t, D_MODEL), BF16),
        ],
        scratch_shapes=[pltpu.VMEM((tm, D_MODEL), F32)],
        compiler_params=_params("parallel", "arbitrary"),
        name="dense_ffn",
    )(hb, hf, wg, wu, wd, ln_g.reshape(1, D_MODEL), ln_b.reshape(1, D_MODEL))


ROUTE_W0, ROUTE_W1, ROUTE_E0, ROUTE_E1, ROUTE_R0, ROUTE_R1 = range(6)


def _router_kernel(h_ref, rw_ref, route_ref, counts_ref, carry_ref):
    i = pl.program_id(0)
    tm = h_ref.shape[0]

    @pl.when(i == 0)
    def _():
        carry_ref[...] = jnp.zeros(carry_ref.shape, F32)

    logits = jnp.dot(h_ref[...], rw_ref[...], preferred_element_type=F32, precision=lax.Precision.HIGHEST)
    lane = lax.broadcasted_iota(jnp.int32, (tm, LANES), 1)
    lg = jnp.where(lane < N_EXPERTS, logits, -jnp.inf)
    m1 = jnp.max(lg, axis=-1, keepdims=True)
    e1 = jnp.min(jnp.where(lg == m1, lane, LANES), axis=-1, keepdims=True)
    lg2 = jnp.where(lane == e1, -jnp.inf, lg)
    m2 = jnp.max(lg2, axis=-1, keepdims=True)
    e2 = jnp.min(jnp.where(lg2 == m2, lane, LANES), axis=-1, keepdims=True)
    x2 = jnp.exp(m2 - m1)
    w1 = 1.0 / (1.0 + x2)
    w2 = x2 / (1.0 + x2)

    sel1 = lane == e1
    sel2 = lane == e2
    mask = jnp.where(sel1 | sel2, 1.0, 0.0)
    r = lax.broadcasted_iota(jnp.int32, (tm, tm), 0)
    c = lax.broadcasted_iota(jnp.int32, (tm, tm), 1)
    tri = jnp.where(r >= c, 1.0, 0.0).astype(BF16)
    incl = jnp.dot(tri, mask.astype(BF16), preferred_element_type=F32)
    rank = incl - mask + carry_ref[...]
    total = carry_ref[...] + jnp.sum(mask, axis=0, keepdims=True)
    carry_ref[...] = total
    counts_ref[...] = total

    r1 = jnp.sum(jnp.where(sel1, rank, 0.0), axis=-1, keepdims=True)
    r2 = jnp.sum(jnp.where(sel2, rank, 0.0), axis=-1, keepdims=True)
    cols = (w1, w2, e1.astype(F32), e2.astype(F32), r1, r2)
    route = jnp.zeros((tm, LANES), F32)
    for idx, val in enumerate(cols):
        route = jnp.where(lane == idx, val, route)
    route_ref[...] = route


def _router(hf, router_w):
    t = hf.shape[0]
    tm = ROUTER_TM
    rw = jnp.pad(router_w, ((0, 0), (0, LANES - N_EXPERTS)))
    return pl.pallas_call(
        _router_kernel,
        grid=(t // tm,),
        in_specs=[
            pl.BlockSpec((tm, D_MODEL), lambda i: (i, 0)),
            pl.BlockSpec((D_MODEL, LANES), lambda i: (0, 0)),
        ],
        out_specs=[
            pl.BlockSpec((tm, LANES), lambda i: (i, 0)),
            pl.BlockSpec((1, LANES), lambda i: (0, 0)),
        ],
        out_shape=[
            jax.ShapeDtypeStruct((t, LANES), F32),
            jax.ShapeDtypeStruct((1, LANES), F32),
        ],
        scratch_shapes=[pltpu.VMEM((1, LANES), F32)],
        compiler_params=_params("arbitrary"),
        name="moe_router",
    )(hf, rw)


def _dispatch_kernel(dest_ref, h_ref, xs_in_hbm, xs_hbm, sem):
    del xs_in_hbm
    n = h_ref.shape[0]

    def row_copy(r, k):
        return pltpu.make_async_copy(
            h_ref.at[pl.ds(r, 1)], xs_hbm.at[pl.ds(dest_ref[0, 0, TOP_K * r + k], 1)], sem)

    def start(r, carry):
        for k in range(TOP_K):
            row_copy(r, k).start()
        return carry

    def wait(r, carry):
        for k in range(TOP_K):
            row_copy(r, k).wait()
        return carry

    lax.fori_loop(0, n, start, 0, unroll=ROW_DMA_UNROLL)
    lax.fori_loop(0, n, wait, 0, unroll=ROW_DMA_UNROLL)


def _dispatch(hf, dest, n_rows):
    t = hf.shape[0]
    tm = DISPATCH_TM
    xs0 = jnp.zeros((n_rows, D_MODEL), F32)
    return pl.pallas_call(
        _dispatch_kernel,
        grid=(t // tm,),
        in_specs=[
            pl.BlockSpec((1, 1, TOP_K * tm), lambda i: (i, 0, 0), memory_space=pltpu.SMEM),
            pl.BlockSpec((tm, D_MODEL), lambda i: (i, 0)),
            pl.BlockSpec(memory_space=pl.ANY),
        ],
        out_specs=pl.BlockSpec(memory_space=pl.ANY),
        out_shape=jax.ShapeDtypeStruct((n_rows, D_MODEL), F32),
        scratch_shapes=[pltpu.SemaphoreType.DMA(())],
        input_output_aliases={2: 0},
        compiler_params=_params("arbitrary"),
        name="moe_dispatch",
    )(dest.reshape(t // tm, 1, TOP_K * tm), hf, xs0)


def _moe_ffn_kernel(te_ref, ok_ref, xs_ref, wg_ref, wu_ref, wd_ref, ys_ref, xb_ref, acc_ref):
    i = pl.program_id(0)
    f = pl.program_id(1)
    ok = ok_ref[i] == 1

    @pl.when(ok & (f == 0))
    def _():
        xb_ref[...] = xs_ref[...].astype(BF16)
        acc_ref[...] = jnp.zeros(acc_ref.shape, F32)

    @pl.when(ok)
    def _():
        a = _swiglu(xb_ref[...], wg_ref[...], wu_ref[...])
        acc_ref[...] += jnp.dot(a, wd_ref[...], preferred_element_type=F32)

    @pl.when(f == pl.num_programs(1) - 1)
    def _():
        ys_ref[...] = jnp.where(ok, acc_ref[...], 0.0)


def _moe_ffn(xs, tile_expert, tile_ok, wg, wu, wd):
    n_rows = xs.shape[0]
    d_ff = wg.shape[2]
    tm, tf = MOE_TM, MOE_TF
    assert d_ff % tf == 0
    n_f = d_ff // tf
    n_tiles = n_rows // tm

    def f_idx(i, f, ok):
        return jnp.where(ok[i] == 1, f, n_f - 1)

    return pl.pallas_call(
        _moe_ffn_kernel,
        grid_spec=pltpu.PrefetchScalarGridSpec(
            num_scalar_prefetch=2,
            grid=(n_tiles, n_f),
            in_specs=[
                pl.BlockSpec((tm, D_MODEL), lambda i, f, te, ok: (i, 0)),
                pl.BlockSpec((None, D_MODEL, tf), lambda i, f, te, ok: (te[i], 0, f_idx(i, f, ok))),
                pl.BlockSpec((None, D_MODEL, tf), lambda i, f, te, ok: (te[i], 0, f_idx(i, f, ok))),
                pl.BlockSpec((None, tf, D_MODEL), lambda i, f, te, ok: (te[i], f_idx(i, f, ok), 0)),
            ],
            out_specs=pl.BlockSpec((tm, D_MODEL), lambda i, f, te, ok: (i, 0)),
            scratch_shapes=[pltpu.VMEM((tm, D_MODEL), BF16), pltpu.VMEM((tm, D_MODEL), F32)],
        ),
        out_shape=jax.ShapeDtypeStruct((n_rows, D_MODEL), F32),
        compiler_params=_params("arbitrary", "arbitrary"),
        name="moe_ffn",
    )(tile_expert, tile_ok, xs, wg, wu, wd)


def _combine_kernel(dest_ref, h_ref, route_ref, ys_hbm, g_ref, b_ref, o_ref, y_ref, sem):
    tm = h_ref.shape[0]

    def row_copy(r, k):
        return pltpu.make_async_copy(
            ys_hbm.at[pl.ds(dest_ref[0, 0, TOP_K * r + k], 1)], y_ref.at[k, pl.ds(r, 1)], sem)

    def start(r, carry):
        for k in range(TOP_K):
            row_copy(r, k).start()
        return carry

    def wait(r, carry):
        for k in range(TOP_K):
            row_copy(r, k).wait()
        return carry

    lax.fori_loop(0, tm, start, 0, unroll=ROW_DMA_UNROLL)
    lax.fori_loop(0, tm, wait, 0, unroll=ROW_DMA_UNROLL)

    route = route_ref[...]
    w1 = route[:, ROUTE_W0:ROUTE_W0 + 1]
    w2 = route[:, ROUTE_W1:ROUTE_W1 + 1]
    y = w1 * y_ref[0] + w2 * y_ref[1]
    o_ref[...] = _layer_norm(ALPHA * h_ref[...] + y, g_ref[...], b_ref[...])


def _combine(hf, route, dest, ys, ln_g, ln_b):
    t = hf.shape[0]
    tm = COMBINE_TM
    return pl.pallas_call(
        _combine_kernel,
        grid=(t // tm,),
        in_specs=[
            pl.BlockSpec((1, 1, TOP_K * tm), lambda i: (i, 0, 0), memory_space=pltpu.SMEM),
            pl.BlockSpec((tm, D_MODEL), lambda i: (i, 0)),
            pl.BlockSpec((tm, LANES), lambda i: (i, 0)),
            pl.BlockSpec(memory_space=pl.ANY),
            pl.BlockSpec((1, D_MODEL), lambda i: (0, 0)),
            pl.BlockSpec((1, D_MODEL), lambda i: (0, 0)),
        ],
        out_specs=pl.BlockSpec((tm, D_MODEL), lambda i: (i, 0)),
        out_shape=jax.ShapeDtypeStruct((t, D_MODEL), F32),
        scratch_shapes=[pltpu.VMEM((TOP_K, tm, D_MODEL), F32), pltpu.SemaphoreType.DMA(())],
        compiler_params=_params("arbitrary"),
        name="moe_combine",
    )(dest.reshape(t // tm, 1, TOP_K * tm), hf, route, ys, ln_g.reshape(1, D_MODEL), ln_b.reshape(1, D_MODEL))


def _moe(hf, router_w, wg, wu, wd, ln_g, ln_b):
    t = hf.shape[0]
    tm = MOE_TM
    n_tiles = (TOP_K * t) // tm + N_EXPERTS
    route, counts = _router(hf, router_w)

    counts = counts[0, :N_EXPERTS].astype(jnp.int32)
    tiles_per_expert = (counts + tm - 1) // tm
    tile_end = jnp.cumsum(tiles_per_expert)
    group_start = (tile_end - tiles_per_expert) * tm
    tile_ids = jnp.arange(n_tiles, dtype=jnp.int32)
    n_used = tile_end[-1]
    last_used = jnp.maximum(n_used - 1, 0)
    expert_of_tile = jnp.sum(tile_ids[:, None] >= tile_end[None, :], axis=1).astype(jnp.int32)
    tile_ok = (tile_ids < n_used).astype(jnp.int32)
    tile_expert = jnp.where(tile_ok == 1, expert_of_tile, expert_of_tile[last_used])
    tile_expert = jnp.minimum(tile_expert, N_EXPERTS - 1)

    experts = route[:, ROUTE_E0:ROUTE_E1 + 1].astype(jnp.int32)
    ranks = route[:, ROUTE_R0:ROUTE_R1 + 1].astype(jnp.int32)
    dest = (group_start[experts] + ranks).reshape(-1)

    xs = _dispatch(hf, dest, n_tiles * tm)
    ys = _moe_ffn(xs, tile_expert, tile_ok, wg, wu, wd)
    return _combine(hf, route, dest, ys, ln_g, ln_b)


def kernel(x, w_in, lambdas, subln_w, pool_w, pool_scale, w_branch_attn, w_branch_pool, w_out, rel_bias,
           ln1_g, ln1_b, dense_w_gate, dense_w_up, dense_w_down, router_w, moe_w_gate, moe_w_up, moe_w_down,
           ln2_g, ln2_b):
    b, s, d = x.shape
    t = b * s
    xf = x.reshape(t, d)
    xin = xf
    bias_tiles = _bias_tiles(rel_bias)
    for l in range(DEPTH):
        lambda_init = 0.8 - 0.6 * math.exp(-0.3 * l)
        qkv, u, gates = _in_proj(xin, w_in[l].astype(BF16))
        attn = _attention(qkv.reshape(b, s, 3 * ATTN_WIDTH), bias_tiles, lambdas[l], subln_w[l], lambda_init)
        hf, hb = _mixer_tail(attn.reshape(t, ATTN_WIDTH), u, gates, xf, pool_w[l].astype(BF16), pool_scale[l],
                             w_branch_attn[l].astype(BF16), w_branch_pool[l].astype(BF16), w_out[l].astype(BF16),
                             ln1_g[l], ln1_b[l], s)
        if l % 2 == 0:
            xf, xin = _dense_ffn(hb, hf, dense_w_gate[l // 2].astype(BF16), dense_w_up[l // 2].astype(BF16),
                                 dense_w_down[l // 2].astype(BF16), ln2_g[l], ln2_b[l])
        else:
            xf = _moe(hf, router_w[l // 2], moe_w_gate[l // 2].astype(BF16), moe_w_up[l // 2].astype(BF16),
                      moe_w_down[l // 2].astype(BF16), ln2_g[l], ln2_b[l])
            xin = xf
    return xf.reshape(b, s, d)
```

```python
import functools
import math

import jax
import jax.numpy as jnp
from jax import lax
from jax.experimental import pallas as pl
from jax.experimental.pallas import tpu as pltpu

D_MODEL = 2048
DEPTH = 2
N_HEADS = 8
HEAD_DIM = 64
HEAD_WIDTH = 2 * HEAD_DIM
ATTN_WIDTH = N_HEADS * HEAD_WIDTH
POOL_WINDOWS = (2, 4, 8, 16)
POOL_GROUP = 256
POOL_WIDTH = POOL_GROUP * len(POOL_WINDOWS)
MAX_WINDOW = max(POOL_WINDOWS)
N_BUCKETS = 32
MAX_DISTANCE = 128
N_EXPERTS = 8
TOP_K = 2
ALPHA = (2.0 * DEPTH) ** 0.25
LN_EPS = 1e-5
LOG2E = math.log2(math.e)
Q_SCALE = HEAD_DIM ** -0.5 * LOG2E
ONES_ROWS = 16

LANES = 128
MASK_VALUE = -1e30
VMEM_LIMIT = 56 * 1024 * 1024

PROJ_TM = 512
PROJ_TN = 1024
ATTN_TQ = 512
ATTN_TK = 512
TAIL_TM = 256
FFN_TM = 512
FFN_TF = 512
ROUTER_TM = 512
DISPATCH_TM = 512
MOE_TM = 512
MOE_TF = 1024
COMBINE_TM = 256
ROW_DMA_UNROLL = 8

F32 = jnp.float32
BF16 = jnp.bfloat16


def _params(*sem):
    return pltpu.CompilerParams(dimension_semantics=sem, vmem_limit_bytes=VMEM_LIMIT)


def _layer_norm(r, g, b):
    mu = jnp.mean(r, axis=-1, keepdims=True)
    c = r - mu
    var = jnp.mean(c * c, axis=-1, keepdims=True)
    return c * lax.rsqrt(var + LN_EPS) * g + b


PROJ_HALF = 2 * D_MODEL
assert 3 * ATTN_WIDTH + POOL_WIDTH == PROJ_HALF


def _proj_mix_kernel(x_ref, w_ref, qkv_ref, u_ref):
    x = x_ref[...].astype(BF16)
    for j in range(PROJ_HALF // PROJ_TN):
        cols = slice(j * PROJ_TN, (j + 1) * PROJ_TN)
        acc = jnp.dot(x, w_ref[:, cols], preferred_element_type=F32)
        if cols.stop <= ATTN_WIDTH:
            qkv_ref[:, cols] = (acc * Q_SCALE).astype(BF16)
        elif cols.stop <= 3 * ATTN_WIDTH:
            qkv_ref[:, cols] = acc.astype(BF16)
        else:
            u_ref[:, j * PROJ_TN - 3 * ATTN_WIDTH:(j + 1) * PROJ_TN - 3 * ATTN_WIDTH] = acc


def _proj_gates_kernel(x_ref, w_ref, g_ref):
    x = x_ref[...].astype(BF16)
    for j in range(PROJ_HALF // PROJ_TN):
        cols = slice(j * PROJ_TN, (j + 1) * PROJ_TN)
        g_ref[:, cols] = jax.nn.sigmoid(jnp.dot(x, w_ref[:, cols], preferred_element_type=F32))


def _in_proj(x, w):
    t, k = x.shape
    tm = PROJ_TM
    assert ATTN_WIDTH % PROJ_TN == 0 and POOL_WIDTH % PROJ_TN == 0
    x_spec = pl.BlockSpec((tm, k), lambda i: (i, 0))

    def w_spec(half):
        return pl.BlockSpec((k, PROJ_HALF), lambda i: (0, half), pipeline_mode=pl.Buffered(1))

    qkv, u = pl.pallas_call(
        _proj_mix_kernel,
        grid=(t // tm,),
        in_specs=[x_spec, w_spec(0)],
        out_specs=[
            pl.BlockSpec((tm, 3 * ATTN_WIDTH), lambda i: (i, 0)),
            pl.BlockSpec((tm, POOL_WIDTH), lambda i: (i, 0)),
        ],
        out_shape=[
            jax.ShapeDtypeStruct((t, 3 * ATTN_WIDTH), BF16),
            jax.ShapeDtypeStruct((t, POOL_WIDTH), F32),
        ],
        compiler_params=_params("parallel"),
        name="in_proj_mix",
    )(x, w)
    gates = pl.pallas_call(
        _proj_gates_kernel,
        grid=(t // tm,),
        in_specs=[x_spec, w_spec(1)],
        out_specs=pl.BlockSpec((tm, PROJ_HALF), lambda i: (i, 0)),
        out_shape=jax.ShapeDtypeStruct((t, PROJ_HALF), F32),
        compiler_params=_params("parallel"),
        name="in_proj_gates",
    )(x, w)
    return qkv, u, gates


def _bias_kernel(rb_ref, o_ref):
    h = pl.program_id(0)
    tk, tq = o_ref.shape[2], o_ref.shape[3]
    kpos = lax.broadcasted_iota(jnp.int32, (tk, tq), 0)
    qpos = lax.broadcasted_iota(jnp.int32, (tk, tq), 1)
    max_exact = N_BUCKETS // 2
    far = rb_ref[h * N_BUCKETS + N_BUCKETS - 1]
    for blk in range(2):
        dist = qpos - kpos + blk * tk
        n = jnp.maximum(dist, 0)
        large = max_exact + (
            jnp.log(jnp.maximum(n, 1).astype(F32) / max_exact) / math.log(MAX_DISTANCE / max_exact)
            * (N_BUCKETS - max_exact)
        ).astype(jnp.int32)
        large = jnp.minimum(large, N_BUCKETS - 1)
        bucket = jnp.where(n < max_exact, n, large)
        bias = jnp.zeros((tk, tq), F32)
        for b in range(N_BUCKETS):
            bias = jnp.where(bucket == b, rb_ref[h * N_BUCKETS + b], bias)
        bias = (bias - far) * LOG2E
        if blk == 0:
            bias = jnp.where(dist >= 0, bias, MASK_VALUE)
        o_ref[0, blk] = bias


def _bias_tiles(rel_bias):
    assert ATTN_TK >= MAX_DISTANCE
    rb = jnp.transpose(rel_bias).reshape(-1)
    return pl.pallas_call(
        _bias_kernel,
        grid_spec=pltpu.PrefetchScalarGridSpec(
            num_scalar_prefetch=1,
            grid=(N_HEADS,),
            in_specs=[],
            out_specs=pl.BlockSpec((1, 2, ATTN_TK, ATTN_TQ), lambda h, rb: (h, 0, 0, 0)),
        ),
        out_shape=jax.ShapeDtypeStruct((N_HEADS, 2, ATTN_TK, ATTN_TQ), F32),
        compiler_params=_params("arbitrary"),
        name="bias_tiles",
    )(rb)


def _attn_kernel(q_ref, k_ref, v_ref, bias_ref, lam_ref, sub_ref, o_ref, qt_ref, vt_ref, m_ref, acc_ref,
                 s0_ref, pp_ref, pa_ref, *, lambda_init):
    qi = pl.program_id(2)
    tq = q_ref.shape[1]
    n_kb, _, tk = vt_ref.shape

    @pl.when(qi == 0)
    def _():
        row = lax.broadcasted_iota(jnp.int32, (ONES_ROWS, tk), 0)
        ones_rows = jnp.where(row == 0, 1.0, 0.0).astype(BF16)
        for kb in range(n_kb):
            vt_ref[kb, 0:HEAD_WIDTH, :] = v_ref[0, kb * tk:(kb + 1) * tk, :].astype(F32).T.astype(BF16)
            vt_ref[kb, HEAD_WIDTH:HEAD_WIDTH + ONES_ROWS, :] = ones_rows

    qt = q_ref[0].astype(F32).T
    dim = lax.broadcasted_iota(jnp.int32, (HEAD_WIDTH, tq), 0)
    qt_ref[0] = jnp.where(dim < HEAD_DIM, qt, 0.0).astype(BF16)
    qt_ref[1] = jnp.where(dim >= HEAD_DIM, qt, 0.0).astype(BF16)

    m_ref[...] = jnp.full(m_ref.shape, MASK_VALUE, F32)
    acc_ref[...] = jnp.zeros(acc_ref.shape, F32)

    half = tq // 2
    chunks = [(c, slice(hf * half, (hf + 1) * half)) for c in range(2) for hf in range(2)]

    def keys(kb):
        return k_ref[0, pl.ds(pl.multiple_of(kb * tk, tk), tk), :]

    def key_rows(kind, chunk):
        return tk // 2 if kind == "diag" and chunk[1].start == 0 else tk

    def scores(kk, chunk, kind="far"):
        c, cols = chunk
        return jnp.dot(kk[:key_rows(kind, chunk)], qt_ref[c, :, cols], preferred_element_type=F32)

    def add_bias(s, chunk, kind):
        cols = chunk[1]
        if kind == "diag":
            return s + bias_ref[0, 0, 0:s.shape[0], cols]
        if kind == "near" and cols.start == 0:
            r0 = tk - MAX_DISTANCE
            return jnp.concatenate([s[:r0], s[r0:] + bias_ref[0, 1, r0:tk, cols]], axis=0)
        return s

    def softmax_update(s, chunk, kind):
        c, cols = chunk
        s = add_bias(s, chunk, kind)
        m_prev = m_ref[c, :, cols]
        m_new = jnp.maximum(m_prev, jnp.max(s, axis=0, keepdims=True))
        alpha = jnp.exp2(m_prev - m_new)
        p = jnp.exp2(s - m_new)
        m_ref[c, :, cols] = m_new
        return p.astype(BF16), alpha

    def accumulate(vt, p, alpha, chunk):
        c, cols = chunk
        pv = jnp.dot(vt[:, :p.shape[0]], p, preferred_element_type=F32)
        acc_ref[c, :, cols] = alpha * acc_ref[c, :, cols] + pv

    def accumulate_pending(kb):
        accumulate(vt_ref[kb], pp_ref[...], pa_ref[...], chunks[3])

    def process(kb, kind):
        kk = keys(kb)
        vt = vt_ref[kb]
        s1 = scores(kk, chunks[1], kind)
        accumulate_pending(jnp.maximum(kb - 1, 0))
        p0, a0 = softmax_update(s0_ref[0:key_rows(kind, chunks[0]), :], chunks[0], kind)
        s2 = scores(kk, chunks[2], kind)
        accumulate(vt, p0, a0, chunks[0])
        p1, a1 = softmax_update(s1, chunks[1], kind)
        s3 = scores(kk, chunks[3], kind)
        accumulate(vt, p1, a1, chunks[1])
        p2, a2 = softmax_update(s2, chunks[2], kind)
        if kind != "diag":
            s0_ref[...] = scores(keys(kb + 1), chunks[0])
        accumulate(vt, p2, a2, chunks[2])
        pp_ref[...], pa_ref[...] = softmax_update(s3, chunks[3], kind)

    s0_ref[...] = scores(keys(0), chunks[0])
    pp_ref[...] = jnp.zeros(pp_ref.shape, BF16)
    pa_ref[...] = jnp.ones(pa_ref.shape, F32)

    n_far = jnp.maximum(qi - 1, 0)

    def far_run(first, count):
        for j in range(count):
            process(first + j, "far")

    def far_quad(j, carry):
        far_run(4 * j, 4)
        return carry

    lax.fori_loop(0, n_far // 4, far_quad, 0)

    @pl.when(n_far % 4 >= 2)
    def _():
        far_run(n_far // 4 * 4, 2)

    @pl.when(n_far % 2 == 1)
    def _():
        far_run(n_far - 1, 1)

    @pl.when(qi >= 1)
    def _():
        process(qi - 1, "near")
        process(qi, "diag")

    @pl.when(qi == 0)
    def _():
        process(0, "diag")

    accumulate_pending(qi)

    lv = lam_ref[...]
    lam = (
        jnp.exp(jnp.sum(lv[0:1] * lv[1:2], axis=-1, keepdims=True))
        - jnp.exp(jnp.sum(lv[2:3] * lv[3:4], axis=-1, keepdims=True))
        + lambda_init
    )
    num = [acc_ref[c, 0:HEAD_WIDTH, :] for c in range(2)]
    den = [acc_ref[c, HEAD_WIDTH:HEAD_WIDTH + 1, :] for c in range(2)]
    ot = num[0] / den[0] - lam * (num[1] / den[1])
    ot = ot * lax.rsqrt(jnp.mean(ot * ot, axis=0, keepdims=True) + LN_EPS)
    o = ot.T * (sub_ref[...] * (1.0 - lambda_init))
    o_ref[0] = o.astype(o_ref.dtype)


def _attention(qkv, bias_tiles, lam_vec, subln_w, lambda_init):
    b, s, _ = qkv.shape
    tq, tk = ATTN_TQ, ATTN_TK
    assert tq == tk and s % tq == 0
    assert tq // 2 >= MAX_DISTANCE and (tk - MAX_DISTANCE) % 8 == 0
    kernel = functools.partial(_attn_kernel, lambda_init=lambda_init)
    return pl.pallas_call(
        kernel,
        grid=(b, N_HEADS, s // tq),
        in_specs=[
            pl.BlockSpec((1, tq, HEAD_WIDTH), lambda bi, h, qi: (bi, qi, h)),
            pl.BlockSpec((1, s, HEAD_WIDTH), lambda bi, h, qi: (bi, 0, N_HEADS + h)),
            pl.BlockSpec((1, s, HEAD_WIDTH), lambda bi, h, qi: (bi, 0, 2 * N_HEADS + h)),
            pl.BlockSpec((1, 2, tk, tq), lambda bi, h, qi: (h, 0, 0, 0)),
            pl.BlockSpec((4, HEAD_DIM), lambda bi, h, qi: (0, 0)),
            pl.BlockSpec((1, HEAD_WIDTH), lambda bi, h, qi: (0, 0)),
        ],
        out_specs=pl.BlockSpec((1, tq, HEAD_WIDTH), lambda bi, h, qi: (bi, qi, h)),
        out_shape=jax.ShapeDtypeStruct((b, s, ATTN_WIDTH), BF16),
        scratch_shapes=[
            pltpu.VMEM((2, HEAD_WIDTH, tq), BF16),
            pltpu.VMEM((s // tk, HEAD_WIDTH + ONES_ROWS, tk), BF16),
            pltpu.VMEM((2, 1, tq), F32),
            pltpu.VMEM((2, HEAD_WIDTH + ONES_ROWS, tq), F32),
            pltpu.VMEM((tk, tq // 2), F32),
            pltpu.VMEM((tk, tq // 2), BF16),
            pltpu.VMEM((1, tq // 2), F32),
        ],
        compiler_params=_params("parallel", "parallel", "arbitrary"),
        name="diff_attention",
    )(qkv, qkv, qkv, bias_tiles, lam_vec, subln_w.reshape(1, HEAD_WIDTH))


def _tail_kernel(attn_ref, u_ref, halo_ref, ga_ref, gp_ref, x_ref, pw_ref, ps_ref, wba_ref, wbp_ref, wout_ref,
                 g_ref, b_ref, of_ref, ob_ref, ubuf, *, tiles_per_seq):
    i = pl.program_id(0)
    tm = u_ref.shape[0]
    tile_in_seq = i % tiles_per_seq
    halo = jnp.where(tile_in_seq == 0, 0.0, halo_ref[...])
    ubuf[0:MAX_WINDOW, :] = halo
    ubuf[MAX_WINDOW:MAX_WINDOW + tm, :] = u_ref[...]
    t = tile_in_seq * tm + lax.broadcasted_iota(jnp.int32, (tm, 1), 0)

    pooled = []
    for g, w in enumerate(POOL_WINDOWS):
        cs = slice(g * POOL_GROUP, (g + 1) * POOL_GROUP)
        cur = ubuf[MAX_WINDOW:MAX_WINDOW + tm, cs]
        win_sum = cur
        for j in range(1, w):
            win_sum = win_sum + ubuf[MAX_WINDOW - j:MAX_WINDOW - j + tm, cs]
        cnt = jnp.minimum(t + 1, w).astype(F32)
        z = (win_sum / cnt - cur).astype(BF16)
        y = jnp.dot(z, pw_ref[g], preferred_element_type=F32)
        pooled.append((y * ps_ref[:, cs]).astype(BF16))
    pool = jnp.concatenate(pooled, axis=-1)

    a = jnp.dot(attn_ref[...], wba_ref[...], preferred_element_type=F32)
    p = jnp.dot(pool, wbp_ref[...], preferred_element_type=F32)
    merged = ga_ref[...] * a + gp_ref[...] * p
    mix = jnp.dot(merged.astype(BF16), wout_ref[...], preferred_element_type=F32)
    out = _layer_norm(ALPHA * x_ref[...] + mix, g_ref[...], b_ref[...])
    of_ref[...] = out
    ob_ref[...] = out.astype(BF16)


def _resident(shape):
    zeros = (0,) * len(shape)
    return pl.BlockSpec(shape, lambda i: zeros, pipeline_mode=pl.Buffered(1))


def _mixer_tail(attn, u, gates, x, pool_w, pool_scale, w_ba, w_bp, w_out, ln_g, ln_b, seq):
    t = x.shape[0]
    tm = TAIL_TM
    assert seq % tm == 0 and tm % MAX_WINDOW == 0
    halo_blocks = tm // MAX_WINDOW
    kernel = functools.partial(_tail_kernel, tiles_per_seq=seq // tm)
    return pl.pallas_call(
        kernel,
        grid=(t // tm,),
        in_specs=[
            pl.BlockSpec((tm, ATTN_WIDTH), lambda i: (i, 0)),
            pl.BlockSpec((tm, POOL_WIDTH), lambda i: (i, 0)),
            pl.BlockSpec((MAX_WINDOW, POOL_WIDTH), lambda i: (jnp.maximum(i * halo_blocks - 1, 0), 0)),
            pl.BlockSpec((tm, D_MODEL), lambda i: (i, 0)),
            pl.BlockSpec((tm, D_MODEL), lambda i: (i, 1)),
            pl.BlockSpec((tm, D_MODEL), lambda i: (i, 0)),
            _resident(pool_w.shape),
            _resident((1, POOL_WIDTH)),
            _resident(w_ba.shape),
            _resident(w_bp.shape),
            _resident(w_out.shape),
            _resident((1, D_MODEL)),
            _resident((1, D_MODEL)),
        ],
        out_specs=[
            pl.BlockSpec((tm, D_MODEL), lambda i: (i, 0)),
            pl.BlockSpec((tm, D_MODEL), lambda i: (i, 0)),
        ],
        out_shape=[
            jax.ShapeDtypeStruct((t, D_MODEL), F32),
            jax.ShapeDtypeStruct((t, D_MODEL), BF16),
        ],
        scratch_shapes=[pltpu.VMEM((MAX_WINDOW + tm, POOL_WIDTH), F32)],
        compiler_params=_params("parallel"),
        name="mixer_tail",
    )(attn, u, u, gates, gates, x, pool_w, pool_scale.reshape(1, POOL_WIDTH), w_ba, w_bp, w_out,
      ln_g.reshape(1, D_MODEL), ln_b.reshape(1, D_MODEL))


def _swiglu(hb, wg, wu):
    g = jnp.dot(hb, wg, preferred_element_type=F32)
    u = jnp.dot(hb, wu, preferred_element_type=F32)
    return (g * jax.nn.sigmoid(g) * u).astype(BF16)


def _dense_ffn_kernel(hb_ref, hf_ref, wg_ref, wu_ref, wd_ref, g_ref, b_ref, of_ref, ob_ref, acc_ref):
    f = pl.program_id(1)

    @pl.when(f == 0)
    def _():
        acc_ref[...] = jnp.zeros(acc_ref.shape, F32)

    a = _swiglu(hb_ref[...], wg_ref[...], wu_ref[...])
    acc_ref[...] += jnp.dot(a, wd_ref[...], preferred_element_type=F32)

    @pl.when(f == pl.num_programs(1) - 1)
    def _():
        out = _layer_norm(ALPHA * hf_ref[...] + acc_ref[...], g_ref[...], b_ref[...])
        of_ref[...] = out
        ob_ref[...] = out.astype(BF16)


def _dense_ffn(hb, hf, wg, wu, wd, ln_g, ln_b):
    t = hb.shape[0]
    d_ff = wg.shape[1]
    tm, tf = FFN_TM, FFN_TF
    assert d_ff % tf == 0
    return pl.pallas_call(
        _dense_ffn_kernel,
        grid=(t // tm, d_ff // tf),
        in_specs=[
            pl.BlockSpec((tm, D_MODEL), lambda i, f: (i, 0)),
            pl.BlockSpec((tm, D_MODEL), lambda i, f: (i, 0)),
            pl.BlockSpec((D_MODEL, tf), lambda i, f: (0, f)),
            pl.BlockSpec((D_MODEL, tf), lambda i, f: (0, f)),
            pl.BlockSpec((tf, D_MODEL), lambda i, f: (f, 0)),
            pl.BlockSpec((1, D_MODEL), lambda i, f: (0, 0)),
            pl.BlockSpec((1, D_MODEL), lambda i, f: (0, 0)),
        ],
        out_specs=[
            pl.BlockSpec((tm, D_MODEL), lambda i, f: (i, 0)),
            pl.BlockSpec((tm, D_MODEL), lambda i, f: (i, 0)),
        ],
        out_shape=[
            jax.ShapeDtypeStruct((t, D_MODEL), F32),
            jax.ShapeDtypeStruct((t, D_MODEL), BF16),
        ],
        scratch_shapes=[pltpu.VMEM((tm, D_MODEL), F32)],
        compiler_params=_params("parallel", "arbitrary"),
        name="dense_ffn",
    )(hb, hf, wg, wu, wd, ln_g.reshape(1, D_MODEL), ln_b.reshape(1, D_MODEL))


ROUTE_W0, ROUTE_W1, ROUTE_E0, ROUTE_E1, ROUTE_R0, ROUTE_R1 = range(6)


def _router_kernel(h_ref, rw_ref, route_ref, counts_ref, carry_ref):
    i = pl.program_id(0)
    tm = h_ref.shape[0]

    @pl.when(i == 0)
    def _():
        carry_ref[...] = jnp.zeros(carry_ref.shape, F32)

    logits = jnp.dot(h_ref[...], rw_ref[...], preferred_element_type=F32, precision=lax.Precision.HIGHEST)
    lane = lax.broadcasted_iota(jnp.int32, (tm, LANES), 1)
    lg = jnp.where(lane < N_EXPERTS, logits, -jnp.inf)
    m1 = jnp.max(lg, axis=-1, keepdims=True)
    e1 = jnp.min(jnp.where(lg == m1, lane, LANES), axis=-1, keepdims=True)
    lg2 = jnp.where(lane == e1, -jnp.inf, lg)
    m2 = jnp.max(lg2, axis=-1, keepdims=True)
    e2 = jnp.min(jnp.where(lg2 == m2, lane, LANES), axis=-1, keepdims=True)
    x2 = jnp.exp(m2 - m1)
    w1 = 1.0 / (1.0 + x2)
    w2 = x2 / (1.0 + x2)

    sel1 = lane == e1
    sel2 = lane == e2
    mask = jnp.where(sel1 | sel2, 1.0, 0.0)
    r = lax.broadcasted_iota(jnp.int32, (tm, tm), 0)
    c = lax.broadcasted_iota(jnp.int32, (tm, tm), 1)
    tri = jnp.where(r >= c, 1.0, 0.0).astype(BF16)
    incl = jnp.dot(tri, mask.astype(BF16), preferred_element_type=F32)
    rank = incl - mask + carry_ref[...]
    total = carry_ref[...] + jnp.sum(mask, axis=0, keepdims=True)
    carry_ref[...] = total
    counts_ref[...] = total

    r1 = jnp.sum(jnp.where(sel1, rank, 0.0), axis=-1, keepdims=True)
    r2 = jnp.sum(jnp.where(sel2, rank, 0.0), axis=-1, keepdims=True)
    cols = (w1, w2, e1.astype(F32), e2.astype(F32), r1, r2)
    route = jnp.zeros((tm, LANES), F32)
    for idx, val in enumerate(cols):
        route = jnp.where(lane == idx, val, route)
    route_ref[...] = route


def _router(hf, router_w):
    t = hf.shape[0]
    tm = ROUTER_TM
    rw = jnp.pad(router_w, ((0, 0), (0, LANES - N_EXPERTS)))
    return pl.pallas_call(
        _router_kernel,
        grid=(t // tm,),
        in_specs=[
            pl.BlockSpec((tm, D_MODEL), lambda i: (i, 0)),
            pl.BlockSpec((D_MODEL, LANES), lambda i: (0, 0)),
        ],
        out_specs=[
            pl.BlockSpec((tm, LANES), lambda i: (i, 0)),
            pl.BlockSpec((1, LANES), lambda i: (0, 0)),
        ],
        out_shape=[
            jax.ShapeDtypeStruct((t, LANES), F32),
            jax.ShapeDtypeStruct((1, LANES), F32),
        ],
        scratch_shapes=[pltpu.VMEM((1, LANES), F32)],
        compiler_params=_params("arbitrary"),
        name="moe_router",
    )(hf, rw)


def _dispatch_kernel(pad_start_ref, pad_count_ref, n_used_ref, dest_ref, h_ref, xs_hbm, zero_ref, sem, pad_sem):
    n = h_ref.shape[0]
    tile_rows = zero_ref.shape[0]
    n_tiles = xs_hbm.shape[0] // tile_rows

    @pl.when(pl.program_id(0) == 0)
    def _():
        zero_ref[...] = jnp.zeros(zero_ref.shape, F32)
        for e in range(N_EXPERTS):
            def pad_copy(j, e=e):
                return pltpu.make_async_copy(
                    zero_ref.at[pl.ds(0, 1)], xs_hbm.at[pl.ds(pad_start_ref[e] + j, 1)], pad_sem)

            lax.fori_loop(0, pad_count_ref[e], lambda j, c: (pad_copy(j).start(), c)[1], 0)
            lax.fori_loop(0, pad_count_ref[e], lambda j, c: (pad_copy(j).wait(), c)[1], 0)

        def tile_copy(j):
            return pltpu.make_async_copy(
                zero_ref, xs_hbm.at[pl.ds(pl.multiple_of(j * tile_rows, tile_rows), tile_rows)], pad_sem)

        lax.fori_loop(n_used_ref[0], n_tiles, lambda j, c: (tile_copy(j).start(), c)[1], 0)
        lax.fori_loop(n_used_ref[0], n_tiles, lambda j, c: (tile_copy(j).wait(), c)[1], 0)

    def row_copy(r, k):
        return pltpu.make_async_copy(
            h_ref.at[pl.ds(r, 1)], xs_hbm.at[pl.ds(dest_ref[0, 0, TOP_K * r + k], 1)], sem)

    def start(r, carry):
        for k in range(TOP_K):
            row_copy(r, k).start()
        return carry

    def wait(r, carry):
        for k in range(TOP_K):
            row_copy(r, k).wait()
        return carry

    lax.fori_loop(0, n, start, 0, unroll=ROW_DMA_UNROLL)
    lax.fori_loop(0, n, wait, 0, unroll=ROW_DMA_UNROLL)


def _dispatch(hf, dest, pad_start, pad_count, n_used, n_tiles):
    t = hf.shape[0]
    tm = DISPATCH_TM
    return pl.pallas_call(
        _dispatch_kernel,
        grid_spec=pltpu.PrefetchScalarGridSpec(
            num_scalar_prefetch=3,
            grid=(t // tm,),
            in_specs=[
                pl.BlockSpec((1, 1, TOP_K * tm), lambda i, ps, pc, nu: (i, 0, 0), memory_space=pltpu.SMEM),
                pl.BlockSpec((tm, D_MODEL), lambda i, ps, pc, nu: (i, 0)),
            ],
            out_specs=pl.BlockSpec(memory_space=pl.ANY),
            scratch_shapes=[
                pltpu.VMEM((MOE_TM, D_MODEL), F32),
                pltpu.SemaphoreType.DMA(()),
                pltpu.SemaphoreType.DMA(()),
            ],
        ),
        out_shape=jax.ShapeDtypeStruct((n_tiles * MOE_TM, D_MODEL), F32),
        compiler_params=_params("arbitrary"),
        name="moe_dispatch",
    )(pad_start, pad_count, n_used, dest.reshape(t // tm, 1, TOP_K * tm), hf)


def _moe_ffn_kernel(te_ref, ok_ref, src_ref, xs_ref, wg_ref, wu_ref, wd_ref, ys_ref, xb_ref, acc_ref):
    del te_ref, src_ref
    i = pl.program_id(0)
    f = pl.program_id(1)
    ok = ok_ref[i] == 1

    @pl.when(ok & (f == 0))
    def _():
        xb_ref[...] = xs_ref[...].astype(BF16)
        acc_ref[...] = jnp.zeros(acc_ref.shape, F32)

    @pl.when(ok)
    def _():
        a = _swiglu(xb_ref[...], wg_ref[...], wu_ref[...])
        acc_ref[...] += jnp.dot(a, wd_ref[...], preferred_element_type=F32)

    @pl.when(f == pl.num_programs(1) - 1)
    def _():
        ys_ref[...] = jnp.where(ok, acc_ref[...], 0.0)


def _moe_ffn(xs, tile_expert, tile_ok, tile_src, wg, wu, wd):
    n_rows = xs.shape[0]
    d_ff = wg.shape[2]
    tm, tf = MOE_TM, MOE_TF
    assert d_ff % tf == 0
    n_f = d_ff // tf
    n_tiles = n_rows // tm

    def f_idx(i, f, ok):
        return jnp.where(ok[i] == 1, f, n_f - 1)

    return pl.pallas_call(
        _moe_ffn_kernel,
        grid_spec=pltpu.PrefetchScalarGridSpec(
            num_scalar_prefetch=3,
            grid=(n_tiles, n_f),
            in_specs=[
                pl.BlockSpec((tm, D_MODEL), lambda i, f, te, ok, src: (src[i], 0)),
                pl.BlockSpec((None, D_MODEL, tf), lambda i, f, te, ok, src: (te[i], 0, f_idx(i, f, ok))),
                pl.BlockSpec((None, D_MODEL, tf), lambda i, f, te, ok, src: (te[i], 0, f_idx(i, f, ok))),
                pl.BlockSpec((None, tf, D_MODEL), lambda i, f, te, ok, src: (te[i], f_idx(i, f, ok), 0)),
            ],
            out_specs=pl.BlockSpec((tm, D_MODEL), lambda i, f, te, ok, src: (i, 0)),
            scratch_shapes=[pltpu.VMEM((tm, D_MODEL), BF16), pltpu.VMEM((tm, D_MODEL), F32)],
        ),
        out_shape=jax.ShapeDtypeStruct((n_rows, D_MODEL), F32),
        compiler_params=_params("arbitrary", "arbitrary"),
        name="moe_ffn",
    )(tile_expert, tile_ok, tile_src, xs, wg, wu, wd)


def _combine_kernel(dest_ref, h_ref, route_ref, ys_hbm, g_ref, b_ref, o_ref, y_ref, sem):
    tm = h_ref.shape[0]

    def row_copy(r, k):
        return pltpu.make_async_copy(
            ys_hbm.at[pl.ds(dest_ref[0, 0, TOP_K * r + k], 1)], y_ref.at[k, pl.ds(r, 1)], sem)

    def start(r, carry):
        for k in range(TOP_K):
            row_copy(r, k).start()
        return carry

    def wait(r, carry):
        for k in range(TOP_K):
            row_copy(r, k).wait()
        return carry

    lax.fori_loop(0, tm, start, 0, unroll=ROW_DMA_UNROLL)
    lax.fori_loop(0, tm, wait, 0, unroll=ROW_DMA_UNROLL)

    route = route_ref[...]
    w1 = route[:, ROUTE_W0:ROUTE_W0 + 1]
    w2 = route[:, ROUTE_W1:ROUTE_W1 + 1]
    y = w1 * y_ref[0] + w2 * y_ref[1]
    o_ref[...] = _layer_norm(ALPHA * h_ref[...] + y, g_ref[...], b_ref[...])


def _combine(hf, route, dest, ys, ln_g, ln_b):
    t = hf.shape[0]
    tm = COMBINE_TM
    return pl.pallas_call(
        _combine_kernel,
        grid=(t // tm,),
        in_specs=[
            pl.BlockSpec((1, 1, TOP_K * tm), lambda i: (i, 0, 0), memory_space=pltpu.SMEM),
            pl.BlockSpec((tm, D_MODEL), lambda i: (i, 0)),
            pl.BlockSpec((tm, LANES), lambda i: (i, 0)),
            pl.BlockSpec(memory_space=pl.ANY),
            pl.BlockSpec((1, D_MODEL), lambda i: (0, 0)),
            pl.BlockSpec((1, D_MODEL), lambda i: (0, 0)),
        ],
        out_specs=pl.BlockSpec((tm, D_MODEL), lambda i: (i, 0)),
        out_shape=jax.ShapeDtypeStruct((t, D_MODEL), F32),
        scratch_shapes=[pltpu.VMEM((TOP_K, tm, D_MODEL), F32), pltpu.SemaphoreType.DMA(())],
        compiler_params=_params("arbitrary"),
        name="moe_combine",
    )(dest.reshape(t // tm, 1, TOP_K * tm), hf, route, ys, ln_g.reshape(1, D_MODEL), ln_b.reshape(1, D_MODEL))


def _moe(hf, router_w, wg, wu, wd, ln_g, ln_b):
    t = hf.shape[0]
    tm = MOE_TM
    n_tiles = (TOP_K * t) // tm + N_EXPERTS
    route, counts = _router(hf, router_w)

    counts = counts[0, :N_EXPERTS].astype(jnp.int32)
    tiles_per_expert = (counts + tm - 1) // tm
    tile_end = jnp.cumsum(tiles_per_expert)
    group_start = (tile_end - tiles_per_expert) * tm
    tile_ids = jnp.arange(n_tiles, dtype=jnp.int32)
    n_used = tile_end[-1]
    last_used = jnp.maximum(n_used - 1, 0)
    expert_of_tile = jnp.sum(tile_ids[:, None] >= tile_end[None, :], axis=1).astype(jnp.int32)
    tile_ok = (tile_ids < n_used).astype(jnp.int32)
    tile_expert = jnp.where(tile_ok == 1, expert_of_tile, expert_of_tile[last_used])
    tile_expert = jnp.minimum(tile_expert, N_EXPERTS - 1)
    tile_src = jnp.where(tile_ok == 1, tile_ids, last_used)

    experts = route[:, ROUTE_E0:ROUTE_E1 + 1].astype(jnp.int32)
    ranks = route[:, ROUTE_R0:ROUTE_R1 + 1].astype(jnp.int32)
    dest = (group_start[experts] + ranks).reshape(-1)

    xs = _dispatch(hf, dest, group_start + counts, tiles_per_expert * tm - counts, n_used.reshape(1), n_tiles)
    ys = _moe_ffn(xs, tile_expert, tile_ok, tile_src, wg, wu, wd)
    return _combine(hf, route, dest, ys, ln_g, ln_b)


def kernel(x, w_in, lambdas, subln_w, pool_w, pool_scale, w_branch_attn, w_branch_pool, w_out, rel_bias,
           ln1_g, ln1_b, dense_w_gate, dense_w_up, dense_w_down, router_w, moe_w_gate, moe_w_up, moe_w_down,
           ln2_g, ln2_b):
    b, s, d = x.shape
    t = b * s
    xf = x.reshape(t, d)
    xin = xf
    bias_tiles = _bias_tiles(rel_bias)
    for l in range(DEPTH):
        lambda_init = 0.8 - 0.6 * math.exp(-0.3 * l)
        qkv, u, gates = _in_proj(xin, w_in[l].astype(BF16))
        attn = _attention(qkv.reshape(b, s, 3 * ATTN_WIDTH), bias_tiles, lambdas[l], subln_w[l], lambda_init)
        hf, hb = _mixer_tail(attn.reshape(t, ATTN_WIDTH), u, gates, xf, pool_w[l].astype(BF16), pool_scale[l],
                             w_branch_attn[l].astype(BF16), w_branch_pool[l].astype(BF16), w_out[l].astype(BF16),
                             ln1_g[l], ln1_b[l], s)
        if l % 2 == 0:
            xf, xin = _dense_ffn(hb, hf, dense_w_gate[l // 2].astype(BF16), dense_w_up[l // 2].astype(BF16),
                                 dense_w_down[l // 2].astype(BF16), ln2_g[l], ln2_b[l])
        else:
            xf = _moe(hf, router_w[l // 2], moe_w_gate[l // 2].astype(BF16), moe_w_up[l // 2].astype(BF16),
                      moe_w_down[l // 2].astype(BF16), ln2_g[l], ln2_b[l])
            xin = xf
    return xf.reshape(b, s, d)
```

```python
import functools
import math

import jax
import jax.numpy as jnp
from jax import lax
from jax.experimental import pallas as pl
from jax.experimental.pallas import tpu as pltpu

D_MODEL = 2048
DEPTH = 2
N_HEADS = 8
HEAD_DIM = 64
HEAD_WIDTH = 2 * HEAD_DIM
ATTN_WIDTH = N_HEADS * HEAD_WIDTH
POOL_WINDOWS = (2, 4, 8, 16)
POOL_GROUP = 256
POOL_WIDTH = POOL_GROUP * len(POOL_WINDOWS)
MAX_WINDOW = max(POOL_WINDOWS)
N_BUCKETS = 32
MAX_DISTANCE = 128
N_EXPERTS = 8
TOP_K = 2
ALPHA = (2.0 * DEPTH) ** 0.25
LN_EPS = 1e-5
LOG2E = math.log2(math.e)
Q_SCALE = HEAD_DIM ** -0.5 * LOG2E
ONES_ROWS = 16

LANES = 128
SUBLANES = 8
MASK_VALUE = -1e30
VMEM_LIMIT = 56 * 1024 * 1024

PROJ_TM = 512
PROJ_TN = 1024
ATTN_TQ = 512
ATTN_TK = 512
TAIL_TM = 256
FFN_TM = 512
FFN_TF = 512
ROUTER_TM = 512
DISPATCH_TM = 512
MOE_TM = 512
MOE_TF = 1024
COMBINE_TM = 256
ROW_DMA_UNROLL = 8

F32 = jnp.float32
BF16 = jnp.bfloat16


def _params(*sem):
    return pltpu.CompilerParams(dimension_semantics=sem, vmem_limit_bytes=VMEM_LIMIT)


def _layer_norm(r, g, b):
    mu = jnp.mean(r, axis=-1, keepdims=True)
    c = r - mu
    var = jnp.mean(c * c, axis=-1, keepdims=True)
    return c * lax.rsqrt(var + LN_EPS) * g + b


PROJ_HALF = 2 * D_MODEL
assert 3 * ATTN_WIDTH + POOL_WIDTH == PROJ_HALF


def _proj_mix_kernel(x_ref, w_ref, qkv_ref, u_ref):
    x = x_ref[...].astype(BF16)
    for j in range(PROJ_HALF // PROJ_TN):
        cols = slice(j * PROJ_TN, (j + 1) * PROJ_TN)
        acc = jnp.dot(x, w_ref[:, cols], preferred_element_type=F32)
        if cols.stop <= ATTN_WIDTH:
            qkv_ref[:, cols] = (acc * Q_SCALE).astype(BF16)
        elif cols.stop <= 3 * ATTN_WIDTH:
            qkv_ref[:, cols] = acc.astype(BF16)
        else:
            u_ref[:, j * PROJ_TN - 3 * ATTN_WIDTH:(j + 1) * PROJ_TN - 3 * ATTN_WIDTH] = acc


def _proj_gates_kernel(x_ref, w_ref, g_ref):
    x = x_ref[...].astype(BF16)
    for j in range(PROJ_HALF // PROJ_TN):
        cols = slice(j * PROJ_TN, (j + 1) * PROJ_TN)
        g_ref[:, cols] = jax.nn.sigmoid(jnp.dot(x, w_ref[:, cols], preferred_element_type=F32))


def _in_proj(x, w):
    t, k = x.shape
    tm = PROJ_TM
    assert ATTN_WIDTH % PROJ_TN == 0 and POOL_WIDTH % PROJ_TN == 0
    x_spec = pl.BlockSpec((tm, k), lambda i: (i, 0))

    def w_spec(half):
        return pl.BlockSpec((k, PROJ_HALF), lambda i: (0, half), pipeline_mode=pl.Buffered(1))

    qkv, u = pl.pallas_call(
        _proj_mix_kernel,
        grid=(t // tm,),
        in_specs=[x_spec, w_spec(0)],
        out_specs=[
            pl.BlockSpec((tm, 3 * ATTN_WIDTH), lambda i: (i, 0)),
            pl.BlockSpec((tm, POOL_WIDTH), lambda i: (i, 0)),
        ],
        out_shape=[
            jax.ShapeDtypeStruct((t, 3 * ATTN_WIDTH), BF16),
            jax.ShapeDtypeStruct((t, POOL_WIDTH), F32),
        ],
        compiler_params=_params("parallel"),
        name="in_proj_mix",
    )(x, w)
    gates = pl.pallas_call(
        _proj_gates_kernel,
        grid=(t // tm,),
        in_specs=[x_spec, w_spec(1)],
        out_specs=pl.BlockSpec((tm, PROJ_HALF), lambda i: (i, 0)),
        out_shape=jax.ShapeDtypeStruct((t, PROJ_HALF), F32),
        compiler_params=_params("parallel"),
        name="in_proj_gates",
    )(x, w)
    return qkv, u, gates


def _bias_kernel(rb_ref, o_ref):
    h = pl.program_id(0)
    tk, tq = o_ref.shape[2], o_ref.shape[3]
    kpos = lax.broadcasted_iota(jnp.int32, (tk, tq), 0)
    qpos = lax.broadcasted_iota(jnp.int32, (tk, tq), 1)
    max_exact = N_BUCKETS // 2
    far = rb_ref[h * N_BUCKETS + N_BUCKETS - 1]
    for blk in range(2):
        dist = qpos - kpos + blk * tk
        n = jnp.maximum(dist, 0)
        large = max_exact + (
            jnp.log(jnp.maximum(n, 1).astype(F32) / max_exact) / math.log(MAX_DISTANCE / max_exact)
            * (N_BUCKETS - max_exact)
        ).astype(jnp.int32)
        large = jnp.minimum(large, N_BUCKETS - 1)
        bucket = jnp.where(n < max_exact, n, large)
        bias = jnp.zeros((tk, tq), F32)
        for b in range(N_BUCKETS):
            bias = jnp.where(bucket == b, rb_ref[h * N_BUCKETS + b], bias)
        bias = (bias - far) * LOG2E
        if blk == 0:
            bias = jnp.where(dist >= 0, bias, MASK_VALUE)
        o_ref[0, blk] = bias


def _bias_tiles(rel_bias):
    assert ATTN_TK >= MAX_DISTANCE
    rb = jnp.transpose(rel_bias).reshape(-1)
    return pl.pallas_call(
        _bias_kernel,
        grid_spec=pltpu.PrefetchScalarGridSpec(
            num_scalar_prefetch=1,
            grid=(N_HEADS,),
            in_specs=[],
            out_specs=pl.BlockSpec((1, 2, ATTN_TK, ATTN_TQ), lambda h, rb: (h, 0, 0, 0)),
        ),
        out_shape=jax.ShapeDtypeStruct((N_HEADS, 2, ATTN_TK, ATTN_TQ), F32),
        compiler_params=_params("arbitrary"),
        name="bias_tiles",
    )(rb)


def _attn_kernel(q_ref, k_ref, v_ref, bias_ref, lam_ref, sub_ref, o_ref, qt_ref, vt_ref, m_ref, acc_ref,
                 s0_ref, pp_ref, pa_ref, *, lambda_init):
    qi = pl.program_id(2)
    tq = q_ref.shape[1]
    n_kb, _, tk = vt_ref.shape

    @pl.when(qi == 0)
    def _():
        row = lax.broadcasted_iota(jnp.int32, (ONES_ROWS, tk), 0)
        ones_rows = jnp.where(row == 0, 1.0, 0.0).astype(BF16)
        for kb in range(n_kb):
            vt_ref[kb, 0:HEAD_WIDTH, :] = v_ref[0, kb * tk:(kb + 1) * tk, :].astype(F32).T.astype(BF16)
            vt_ref[kb, HEAD_WIDTH:HEAD_WIDTH + ONES_ROWS, :] = ones_rows

    qt = q_ref[0].astype(F32).T
    dim = lax.broadcasted_iota(jnp.int32, (HEAD_WIDTH, tq), 0)
    qt_ref[0] = jnp.where(dim < HEAD_DIM, qt, 0.0).astype(BF16)
    qt_ref[1] = jnp.where(dim >= HEAD_DIM, qt, 0.0).astype(BF16)

    m_ref[...] = jnp.full(m_ref.shape, MASK_VALUE, F32)
    acc_ref[...] = jnp.zeros(acc_ref.shape, F32)

    half = tq // 2
    chunks = [(c, slice(hf * half, (hf + 1) * half)) for c in range(2) for hf in range(2)]

    def keys(kb):
        return k_ref[0, pl.ds(pl.multiple_of(kb * tk, tk), tk), :]

    def key_rows(kind, chunk):
        return tk // 2 if kind == "diag" and chunk[1].start == 0 else tk

    def scores(kk, chunk, kind="far"):
        c, cols = chunk
        return jnp.dot(kk[:key_rows(kind, chunk)], qt_ref[c, :, cols], preferred_element_type=F32)

    def add_bias(s, chunk, kind):
        cols = chunk[1]
        if kind == "diag":
            return s + bias_ref[0, 0, 0:s.shape[0], cols]
        if kind == "near" and cols.start == 0:
            r0 = tk - MAX_DISTANCE
            return jnp.concatenate([s[:r0], s[r0:] + bias_ref[0, 1, r0:tk, cols]], axis=0)
        return s

    def softmax_update(s, chunk, kind):
        c, cols = chunk
        s = add_bias(s, chunk, kind)
        m_prev = m_ref[c, :, cols]
        m_new = jnp.maximum(m_prev, jnp.max(s, axis=0, keepdims=True))
        alpha = jnp.exp2(m_prev - m_new)
        p = jnp.exp2(s - m_new)
        m_ref[c, :, cols] = m_new
        return p.astype(BF16), alpha

    def accumulate(vt, p, alpha, chunk):
        c, cols = chunk
        pv = jnp.dot(vt[:, :p.shape[0]], p, preferred_element_type=F32)
        acc_ref[c, :, cols] = alpha * acc_ref[c, :, cols] + pv

    def accumulate_pending(kb):
        accumulate(vt_ref[kb], pp_ref[...], pa_ref[...], chunks[3])

    def process(kb, kind):
        kk = keys(kb)
        vt = vt_ref[kb]
        s1 = scores(kk, chunks[1], kind)
        accumulate_pending(jnp.maximum(kb - 1, 0))
        p0, a0 = softmax_update(s0_ref[0:key_rows(kind, chunks[0]), :], chunks[0], kind)
        s2 = scores(kk, chunks[2], kind)
        accumulate(vt, p0, a0, chunks[0])
        p1, a1 = softmax_update(s1, chunks[1], kind)
        s3 = scores(kk, chunks[3], kind)
        accumulate(vt, p1, a1, chunks[1])
        p2, a2 = softmax_update(s2, chunks[2], kind)
        if kind != "diag":
            s0_ref[...] = scores(keys(kb + 1), chunks[0])
        accumulate(vt, p2, a2, chunks[2])
        pp_ref[...], pa_ref[...] = softmax_update(s3, chunks[3], kind)

    s0_ref[...] = scores(keys(0), chunks[0])
    pp_ref[...] = jnp.zeros(pp_ref.shape, BF16)
    pa_ref[...] = jnp.ones(pa_ref.shape, F32)

    n_far = jnp.maximum(qi - 1, 0)

    def far_run(first, count):
        for j in range(count):
            process(first + j, "far")

    def far_quad(j, carry):
        far_run(4 * j, 4)
        return carry

    lax.fori_loop(0, n_far // 4, far_quad, 0)

    @pl.when(n_far % 4 >= 2)
    def _():
        far_run(n_far // 4 * 4, 2)

    @pl.when(n_far % 2 == 1)
    def _():
        far_run(n_far - 1, 1)

    @pl.when(qi >= 1)
    def _():
        process(qi - 1, "near")
        process(qi, "diag")

    @pl.when(qi == 0)
    def _():
        process(0, "diag")

    accumulate_pending(qi)

    lv = lam_ref[...]
    lam = (
        jnp.exp(jnp.sum(lv[0:1] * lv[1:2], axis=-1, keepdims=True))
        - jnp.exp(jnp.sum(lv[2:3] * lv[3:4], axis=-1, keepdims=True))
        + lambda_init
    )
    num = [acc_ref[c, 0:HEAD_WIDTH, :] for c in range(2)]
    den = [acc_ref[c, HEAD_WIDTH:HEAD_WIDTH + 1, :] for c in range(2)]
    ot = num[0] / den[0] - lam * (num[1] / den[1])
    ot = ot * lax.rsqrt(jnp.mean(ot * ot, axis=0, keepdims=True) + LN_EPS)
    o = ot.T * (sub_ref[...] * (1.0 - lambda_init))
    o_ref[0] = o.astype(o_ref.dtype)


def _attention(qkv, bias_tiles, lam_vec, subln_w, lambda_init):
    b, s, _ = qkv.shape
    tq, tk = ATTN_TQ, ATTN_TK
    assert tq == tk and s % tq == 0
    assert tq // 2 >= MAX_DISTANCE and (tk - MAX_DISTANCE) % 8 == 0
    kernel = functools.partial(_attn_kernel, lambda_init=lambda_init)
    return pl.pallas_call(
        kernel,
        grid=(b, N_HEADS, s // tq),
        in_specs=[
            pl.BlockSpec((1, tq, HEAD_WIDTH), lambda bi, h, qi: (bi, qi, h)),
            pl.BlockSpec((1, s, HEAD_WIDTH), lambda bi, h, qi: (bi, 0, N_HEADS + h)),
            pl.BlockSpec((1, s, HEAD_WIDTH), lambda bi, h, qi: (bi, 0, 2 * N_HEADS + h)),
            pl.BlockSpec((1, 2, tk, tq), lambda bi, h, qi: (h, 0, 0, 0)),
            pl.BlockSpec((4, HEAD_DIM), lambda bi, h, qi: (0, 0)),
            pl.BlockSpec((1, HEAD_WIDTH), lambda bi, h, qi: (0, 0)),
        ],
        out_specs=pl.BlockSpec((1, tq, HEAD_WIDTH), lambda bi, h, qi: (bi, qi, h)),
        out_shape=jax.ShapeDtypeStruct((b, s, ATTN_WIDTH), BF16),
        scratch_shapes=[
            pltpu.VMEM((2, HEAD_WIDTH, tq), BF16),
            pltpu.VMEM((s // tk, HEAD_WIDTH + ONES_ROWS, tk), BF16),
            pltpu.VMEM((2, 1, tq), F32),
            pltpu.VMEM((2, HEAD_WIDTH + ONES_ROWS, tq), F32),
            pltpu.VMEM((tk, tq // 2), F32),
            pltpu.VMEM((tk, tq // 2), BF16),
            pltpu.VMEM((1, tq // 2), F32),
        ],
        compiler_params=_params("parallel", "parallel", "arbitrary"),
        name="diff_attention",
    )(qkv, qkv, qkv, bias_tiles, lam_vec, subln_w.reshape(1, HEAD_WIDTH))


def _tail_kernel(attn_ref, u_ref, halo_ref, ga_ref, gp_ref, x_ref, pw_ref, ps_ref, wba_ref, wbp_ref, wout_ref,
                 g_ref, b_ref, of_ref, ob_ref, sums_ref, *, tiles_per_seq):
    i = pl.program_id(0)
    tm = u_ref.shape[0]
    tile_in_seq = i % tiles_per_seq
    r0 = SUBLANES + MAX_WINDOW
    n = r0 + tm
    sums_ref[:, 0:SUBLANES, :] = jnp.zeros((sums_ref.shape[0], SUBLANES, POOL_WIDTH), F32)
    sums_ref[0, SUBLANES:r0, :] = jnp.where(tile_in_seq == 0, 0.0, halo_ref[...])
    sums_ref[0, r0:n, :] = u_ref[...]
    t = tile_in_seq * tm + lax.broadcasted_iota(jnp.int32, (tm, 1), 0)

    def doubled(k, lo, cs):
        shift = 2 ** k
        return sums_ref[k, lo:n, cs] + sums_ref[k, lo - shift:n - shift, cs]

    for k in range(len(POOL_WINDOWS) - 1):
        cs = slice((k + 1) * POOL_GROUP, POOL_WIDTH)
        sums_ref[k + 1, SUBLANES:n, cs] = doubled(k, SUBLANES, cs)

    pooled = []
    for g, w in enumerate(POOL_WINDOWS):
        cs = slice(g * POOL_GROUP, (g + 1) * POOL_GROUP)
        cur = sums_ref[0, r0:n, cs]
        win_sum = doubled(g, r0, cs)
        cnt = jnp.minimum(t + 1, w).astype(F32)
        z = (win_sum / cnt - cur).astype(BF16)
        y = jnp.dot(z, pw_ref[g], preferred_element_type=F32)
        pooled.append((y * ps_ref[:, cs]).astype(BF16))
    pool = jnp.concatenate(pooled, axis=-1)

    a = jnp.dot(attn_ref[...], wba_ref[...], preferred_element_type=F32)
    p = jnp.dot(pool, wbp_ref[...], preferred_element_type=F32)
    merged = ga_ref[...] * a + gp_ref[...] * p
    mix = jnp.dot(merged.astype(BF16), wout_ref[...], preferred_element_type=F32)
    out = _layer_norm(ALPHA * x_ref[...] + mix, g_ref[...], b_ref[...])
    of_ref[...] = out
    ob_ref[...] = out.astype(BF16)


def _resident(shape):
    zeros = (0,) * len(shape)
    return pl.BlockSpec(shape, lambda i: zeros, pipeline_mode=pl.Buffered(1))


def _mixer_tail(attn, u, gates, x, pool_w, pool_scale, w_ba, w_bp, w_out, ln_g, ln_b, seq):
    t = x.shape[0]
    tm = TAIL_TM
    assert seq % tm == 0 and tm % MAX_WINDOW == 0
    assert POOL_WINDOWS == tuple(2 ** (g + 1) for g in range(len(POOL_WINDOWS))) and MAX_WINDOW >= SUBLANES
    halo_blocks = tm // MAX_WINDOW
    kernel = functools.partial(_tail_kernel, tiles_per_seq=seq // tm)
    return pl.pallas_call(
        kernel,
        grid=(t // tm,),
        in_specs=[
            pl.BlockSpec((tm, ATTN_WIDTH), lambda i: (i, 0)),
            pl.BlockSpec((tm, POOL_WIDTH), lambda i: (i, 0)),
            pl.BlockSpec((MAX_WINDOW, POOL_WIDTH), lambda i: (jnp.maximum(i * halo_blocks - 1, 0), 0)),
            pl.BlockSpec((tm, D_MODEL), lambda i: (i, 0)),
            pl.BlockSpec((tm, D_MODEL), lambda i: (i, 1)),
            pl.BlockSpec((tm, D_MODEL), lambda i: (i, 0)),
            _resident(pool_w.shape),
            _resident((1, POOL_WIDTH)),
            _resident(w_ba.shape),
            _resident(w_bp.shape),
            _resident(w_out.shape),
            _resident((1, D_MODEL)),
            _resident((1, D_MODEL)),
        ],
        out_specs=[
            pl.BlockSpec((tm, D_MODEL), lambda i: (i, 0)),
            pl.BlockSpec((tm, D_MODEL), lambda i: (i, 0)),
        ],
        out_shape=[
            jax.ShapeDtypeStruct((t, D_MODEL), F32),
            jax.ShapeDtypeStruct((t, D_MODEL), BF16),
        ],
        scratch_shapes=[pltpu.VMEM((len(POOL_WINDOWS), SUBLANES + MAX_WINDOW + tm, POOL_WIDTH), F32)],
        compiler_params=_params("parallel"),
        name="mixer_tail",
    )(attn, u, u, gates, gates, x, pool_w, pool_scale.reshape(1, POOL_WIDTH), w_ba, w_bp, w_out,
      ln_g.reshape(1, D_MODEL), ln_b.reshape(1, D_MODEL))


def _swiglu(hb, wg, wu):
    g = jnp.dot(hb, wg, preferred_element_type=F32)
    u = jnp.dot(hb, wu, preferred_element_type=F32)
    return (g * jax.nn.sigmoid(g) * u).astype(BF16)


def _dense_ffn_kernel(hb_ref, hf_ref, wg_ref, wu_ref, wd_ref, g_ref, b_ref, of_ref, ob_ref, acc_ref):
    f = pl.program_id(1)

    @pl.when(f == 0)
    def _():
        acc_ref[...] = jnp.zeros(acc_ref.shape, F32)

    a = _swiglu(hb_ref[...], wg_ref[...], wu_ref[...])
    acc_ref[...] += jnp.dot(a, wd_ref[...], preferred_element_type=F32)

    @pl.when(f == pl.num_programs(1) - 1)
    def _():
        out = _layer_norm(ALPHA * hf_ref[...] + acc_ref[...], g_ref[...], b_ref[...])
        of_ref[...] = out
        ob_ref[...] = out.astype(BF16)


def _dense_ffn(hb, hf, wg, wu, wd, ln_g, ln_b):
    t = hb.shape[0]
    d_ff = wg.shape[1]
    tm, tf = FFN_TM, FFN_TF
    assert d_ff % tf == 0
    return pl.pallas_call(
        _dense_ffn_kernel,
        grid=(t // tm, d_ff // tf),
        in_specs=[
            pl.BlockSpec((tm, D_MODEL), lambda i, f: (i, 0)),
            pl.BlockSpec((tm, D_MODEL), lambda i, f: (i, 0)),
            pl.BlockSpec((D_MODEL, tf), lambda i, f: (0, f)),
            pl.BlockSpec((D_MODEL, tf), lambda i, f: (0, f)),
            pl.BlockSpec((tf, D_MODEL), lambda i, f: (f, 0)),
            pl.BlockSpec((1, D_MODEL), lambda i, f: (0, 0)),
            pl.BlockSpec((1, D_MODEL), lambda i, f: (0, 0)),
        ],
        out_specs=[
            pl.BlockSpec((tm, D_MODEL), lambda i, f: (i, 0)),
            pl.BlockSpec((tm, D_MODEL), lambda i, f: (i, 0)),
        ],
        out_shape=[
            jax.ShapeDtypeStruct((t, D_MODEL), F32),
            jax.ShapeDtypeStruct((t, D_MODEL), BF16),
        ],
        scratch_shapes=[pltpu.VMEM((tm, D_MODEL), F32)],
        compiler_params=_params("parallel", "arbitrary"),
        name="dense_ffn",
    )(hb, hf, wg, wu, wd, ln_g.reshape(1, D_MODEL), ln_b.reshape(1, D_MODEL))


ROUTE_W0, ROUTE_W1, ROUTE_E0, ROUTE_E1, ROUTE_R0, ROUTE_R1 = range(6)


def _router_kernel(h_ref, rw_ref, route_ref, counts_ref, carry_ref):
    i = pl.program_id(0)
    tm = h_ref.shape[0]

    @pl.when(i == 0)
    def _():
        carry_ref[...] = jnp.zeros(carry_ref.shape, F32)

    logits = jnp.dot(h_ref[...], rw_ref[...], preferred_element_type=F32)
    lane = lax.broadcasted_iota(jnp.int32, (tm, LANES), 1)
    lg = jnp.where(lane < N_EXPERTS, logits, -jnp.inf)
    m1 = jnp.max(lg, axis=-1, keepdims=True)
    e1 = jnp.min(jnp.where(lg == m1, lane, LANES), axis=-1, keepdims=True)
    lg2 = jnp.where(lane == e1, -jnp.inf, lg)
    m2 = jnp.max(lg2, axis=-1, keepdims=True)
    e2 = jnp.min(jnp.where(lg2 == m2, lane, LANES), axis=-1, keepdims=True)
    x2 = jnp.exp(m2 - m1)
    w1 = 1.0 / (1.0 + x2)
    w2 = x2 / (1.0 + x2)

    sel1 = lane == e1
    sel2 = lane == e2
    mask = jnp.where(sel1 | sel2, 1.0, 0.0)
    r = lax.broadcasted_iota(jnp.int32, (tm, tm), 0)
    c = lax.broadcasted_iota(jnp.int32, (tm, tm), 1)
    tri = jnp.where(r >= c, 1.0, 0.0).astype(BF16)
    incl = jnp.dot(tri, mask.astype(BF16), preferred_element_type=F32)
    rank = incl - mask + carry_ref[...]
    total = carry_ref[...] + jnp.sum(mask, axis=0, keepdims=True)
    carry_ref[...] = total
    counts_ref[...] = total

    r1 = jnp.sum(jnp.where(sel1, rank, 0.0), axis=-1, keepdims=True)
    r2 = jnp.sum(jnp.where(sel2, rank, 0.0), axis=-1, keepdims=True)
    cols = (w1, w2, e1.astype(F32), e2.astype(F32), r1, r2)
    route = jnp.zeros((tm, LANES), F32)
    for idx, val in enumerate(cols):
        route = jnp.where(lane == idx, val, route)
    route_ref[...] = route


def _router(hb, router_w):
    t = hb.shape[0]
    tm = ROUTER_TM
    rw = jnp.pad(router_w, ((0, 0), (0, LANES - N_EXPERTS))).astype(BF16)
    return pl.pallas_call(
        _router_kernel,
        grid=(t // tm,),
        in_specs=[
            pl.BlockSpec((tm, D_MODEL), lambda i: (i, 0)),
            pl.BlockSpec((D_MODEL, LANES), lambda i: (0, 0)),
        ],
        out_specs=[
            pl.BlockSpec((tm, LANES), lambda i: (i, 0)),
            pl.BlockSpec((1, LANES), lambda i: (0, 0)),
        ],
        out_shape=[
            jax.ShapeDtypeStruct((t, LANES), F32),
            jax.ShapeDtypeStruct((1, LANES), F32),
        ],
        scratch_shapes=[pltpu.VMEM((1, LANES), F32)],
        compiler_params=_params("arbitrary"),
        name="moe_router",
    )(hb, rw)


def _dispatch_kernel(pad_start_ref, pad_count_ref, n_used_ref, dest_ref, h_ref, xs_hbm, zero_ref, sem, pad_sem):
    n = h_ref.shape[0]
    tile_rows = zero_ref.shape[0]
    n_tiles = xs_hbm.shape[0] // tile_rows

    @pl.when(pl.program_id(0) == 0)
    def _():
        zero_ref[...] = jnp.zeros(zero_ref.shape, F32)
        for e in range(N_EXPERTS):
            def pad_copy(j, e=e):
                return pltpu.make_async_copy(
                    zero_ref.at[pl.ds(0, 1)], xs_hbm.at[pl.ds(pad_start_ref[e] + j, 1)], pad_sem)

            lax.fori_loop(0, pad_count_ref[e], lambda j, c: (pad_copy(j).start(), c)[1], 0)
            lax.fori_loop(0, pad_count_ref[e], lambda j, c: (pad_copy(j).wait(), c)[1], 0)

        def tile_copy(j):
            return pltpu.make_async_copy(
                zero_ref, xs_hbm.at[pl.ds(pl.multiple_of(j * tile_rows, tile_rows), tile_rows)], pad_sem)

        lax.fori_loop(n_used_ref[0], n_tiles, lambda j, c: (tile_copy(j).start(), c)[1], 0)
        lax.fori_loop(n_used_ref[0], n_tiles, lambda j, c: (tile_copy(j).wait(), c)[1], 0)

    def row_copy(r, k):
        return pltpu.make_async_copy(
            h_ref.at[pl.ds(r, 1)], xs_hbm.at[pl.ds(dest_ref[0, 0, TOP_K * r + k], 1)], sem)

    def start(r, carry):
        for k in range(TOP_K):
            row_copy(r, k).start()
        return carry

    def wait(r, carry):
        for k in range(TOP_K):
            row_copy(r, k).wait()
        return carry

    lax.fori_loop(0, n, start, 0, unroll=ROW_DMA_UNROLL)
    lax.fori_loop(0, n, wait, 0, unroll=ROW_DMA_UNROLL)


def _dispatch(hf, dest, pad_start, pad_count, n_used, n_tiles):
    t = hf.shape[0]
    tm = DISPATCH_TM
    return pl.pallas_call(
        _dispatch_kernel,
        grid_spec=pltpu.PrefetchScalarGridSpec(
            num_scalar_prefetch=3,
            grid=(t // tm,),
            in_specs=[
                pl.BlockSpec((1, 1, TOP_K * tm), lambda i, ps, pc, nu: (i, 0, 0), memory_space=pltpu.SMEM),
                pl.BlockSpec((tm, D_MODEL), lambda i, ps, pc, nu: (i, 0)),
            ],
            out_specs=pl.BlockSpec(memory_space=pl.ANY),
            scratch_shapes=[
                pltpu.VMEM((MOE_TM, D_MODEL), F32),
                pltpu.SemaphoreType.DMA(()),
                pltpu.SemaphoreType.DMA(()),
            ],
        ),
        out_shape=jax.ShapeDtypeStruct((n_tiles * MOE_TM, D_MODEL), F32),
        compiler_params=_params("arbitrary"),
        name="moe_dispatch",
    )(pad_start, pad_count, n_used, dest.reshape(t // tm, 1, TOP_K * tm), hf)


def _moe_ffn_kernel(te_ref, ok_ref, src_ref, xs_ref, wg_ref, wu_ref, wd_ref, ys_ref, xb_ref, acc_ref):
    del te_ref, src_ref
    i = pl.program_id(0)
    f = pl.program_id(1)
    ok = ok_ref[i] == 1

    @pl.when(ok & (f == 0))
    def _():
        xb_ref[...] = xs_ref[...].astype(BF16)
        acc_ref[...] = jnp.zeros(acc_ref.shape, F32)

    @pl.when(ok)
    def _():
        a = _swiglu(xb_ref[...], wg_ref[...], wu_ref[...])
        acc_ref[...] += jnp.dot(a, wd_ref[...], preferred_element_type=F32)

    @pl.when(f == pl.num_programs(1) - 1)
    def _():
        ys_ref[...] = jnp.where(ok, acc_ref[...], 0.0)


def _moe_ffn(xs, tile_expert, tile_ok, tile_src, wg, wu, wd):
    n_rows = xs.shape[0]
    d_ff = wg.shape[2]
    tm, tf = MOE_TM, MOE_TF
    assert d_ff % tf == 0
    n_f = d_ff // tf
    n_tiles = n_rows // tm

    def f_idx(i, f, ok):
        return jnp.where(ok[i] == 1, f, n_f - 1)

    return pl.pallas_call(
        _moe_ffn_kernel,
        grid_spec=pltpu.PrefetchScalarGridSpec(
            num_scalar_prefetch=3,
            grid=(n_tiles, n_f),
            in_specs=[
                pl.BlockSpec((tm, D_MODEL), lambda i, f, te, ok, src: (src[i], 0)),
                pl.BlockSpec((None, D_MODEL, tf), lambda i, f, te, ok, src: (te[i], 0, f_idx(i, f, ok))),
                pl.BlockSpec((None, D_MODEL, tf), lambda i, f, te, ok, src: (te[i], 0, f_idx(i, f, ok))),
                pl.BlockSpec((None, tf, D_MODEL), lambda i, f, te, ok, src: (te[i], f_idx(i, f, ok), 0)),
            ],
            out_specs=pl.BlockSpec((tm, D_MODEL), lambda i, f, te, ok, src: (i, 0)),
            scratch_shapes=[pltpu.VMEM((tm, D_MODEL), BF16), pltpu.VMEM((tm, D_MODEL), F32)],
        ),
        out_shape=jax.ShapeDtypeStruct((n_rows, D_MODEL), F32),
        compiler_params=_params("arbitrary", "arbitrary"),
        name="moe_ffn",
    )(tile_expert, tile_ok, tile_src, xs, wg, wu, wd)


def _combine_kernel(dest_ref, h_ref, route_ref, ys_hbm, g_ref, b_ref, o_ref, y_ref, sem):
    tm = h_ref.shape[0]

    def row_copy(r, k):
        return pltpu.make_async_copy(
            ys_hbm.at[pl.ds(dest_ref[0, 0, TOP_K * r + k], 1)], y_ref.at[k, pl.ds(r, 1)], sem)

    def start(r, carry):
        for k in range(TOP_K):
            row_copy(r, k).start()
        return carry

    def wait(r, carry):
        for k in range(TOP_K):
            row_copy(r, k).wait()
        return carry

    lax.fori_loop(0, tm, start, 0, unroll=ROW_DMA_UNROLL)
    lax.fori_loop(0, tm, wait, 0, unroll=ROW_DMA_UNROLL)

    route = route_ref[...]
    w1 = route[:, ROUTE_W0:ROUTE_W0 + 1]
    w2 = route[:, ROUTE_W1:ROUTE_W1 + 1]
    y = w1 * y_ref[0] + w2 * y_ref[1]
    o_ref[...] = _layer_norm(ALPHA * h_ref[...] + y, g_ref[...], b_ref[...])


def _combine(hf, route, dest, ys, ln_g, ln_b):
    t = hf.shape[0]
    tm = COMBINE_TM
    return pl.pallas_call(
        _combine_kernel,
        grid=(t // tm,),
        in_specs=[
            pl.BlockSpec((1, 1, TOP_K * tm), lambda i: (i, 0, 0), memory_space=pltpu.SMEM),
            pl.BlockSpec((tm, D_MODEL), lambda i: (i, 0)),
            pl.BlockSpec((tm, LANES), lambda i: (i, 0)),
            pl.BlockSpec(memory_space=pl.ANY),
            pl.BlockSpec((1, D_MODEL), lambda i: (0, 0)),
            pl.BlockSpec((1, D_MODEL), lambda i: (0, 0)),
        ],
        out_specs=pl.BlockSpec((tm, D_MODEL), lambda i: (i, 0)),
        out_shape=jax.ShapeDtypeStruct((t, D_MODEL), F32),
        scratch_shapes=[pltpu.VMEM((TOP_K, tm, D_MODEL), F32), pltpu.SemaphoreType.DMA(())],
        compiler_params=_params("arbitrary"),
        name="moe_combine",
    )(dest.reshape(t // tm, 1, TOP_K * tm), hf, route, ys, ln_g.reshape(1, D_MODEL), ln_b.reshape(1, D_MODEL))


def _moe(hf, hb, router_w, wg, wu, wd, ln_g, ln_b):
    t = hf.shape[0]
    tm = MOE_TM
    n_tiles = (TOP_K * t) // tm + N_EXPERTS
    route, counts = _router(hb, router_w)

    counts = counts[0, :N_EXPERTS].astype(jnp.int32)
    tiles_per_expert = (counts + tm - 1) // tm
    tile_end = jnp.cumsum(tiles_per_expert)
    group_start = (tile_end - tiles_per_expert) * tm
    tile_ids = jnp.arange(n_tiles, dtype=jnp.int32)
    n_used = tile_end[-1]
    last_used = jnp.maximum(n_used - 1, 0)
    expert_of_tile = jnp.sum(tile_ids[:, None] >= tile_end[None, :], axis=1).astype(jnp.int32)
    tile_ok = (tile_ids < n_used).astype(jnp.int32)
    tile_expert = jnp.where(tile_ok == 1, expert_of_tile, expert_of_tile[last_used])
    tile_expert = jnp.minimum(tile_expert, N_EXPERTS - 1)
    tile_src = jnp.where(tile_ok == 1, tile_ids, last_used)

    experts = route[:, ROUTE_E0:ROUTE_E1 + 1].astype(jnp.int32)
    ranks = route[:, ROUTE_R0:ROUTE_R1 + 1].astype(jnp.int32)
    dest = (group_start[experts] + ranks).reshape(-1)

    xs = _dispatch(hf, dest, group_start + counts, tiles_per_expert * tm - counts, n_used.reshape(1), n_tiles)
    ys = _moe_ffn(xs, tile_expert, tile_ok, tile_src, wg, wu, wd)
    return _combine(hf, route, dest, ys, ln_g, ln_b)


def kernel(x, w_in, lambdas, subln_w, pool_w, pool_scale, w_branch_attn, w_branch_pool, w_out, rel_bias,
           ln1_g, ln1_b, dense_w_gate, dense_w_up, dense_w_down, router_w, moe_w_gate, moe_w_up, moe_w_down,
           ln2_g, ln2_b):
    b, s, d = x.shape
    t = b * s
    xf = x.reshape(t, d)
    xin = xf
    bias_tiles = _bias_tiles(rel_bias)
    for l in range(DEPTH):
        lambda_init = 0.8 - 0.6 * math.exp(-0.3 * l)
        qkv, u, gates = _in_proj(xin, w_in[l].astype(BF16))
        attn = _attention(qkv.reshape(b, s, 3 * ATTN_WIDTH), bias_tiles, lambdas[l], subln_w[l], lambda_init)
        hf, hb = _mixer_tail(attn.reshape(t, ATTN_WIDTH), u, gates, xf, pool_w[l].astype(BF16), pool_scale[l],
                             w_branch_attn[l].astype(BF16), w_branch_pool[l].astype(BF16), w_out[l].astype(BF16),
                             ln1_g[l], ln1_b[l], s)
        if l % 2 == 0:
            xf, xin = _dense_ffn(hb, hf, dense_w_gate[l // 2].astype(BF16), dense_w_up[l // 2].astype(BF16),
                                 dense_w_down[l // 2].astype(BF16), ln2_g[l], ln2_b[l])
        else:
            xf = _moe(hf, hb, router_w[l // 2], moe_w_gate[l // 2].astype(BF16), moe_w_up[l // 2].astype(BF16),
                      moe_w_down[l // 2].astype(BF16), ln2_g[l], ln2_b[l])
            xin = xf
    return xf.reshape(b, s, d)
```

```python
import functools
import math

import jax
import jax.numpy as jnp
from jax import lax
from jax.experimental import pallas as pl
from jax.experimental.pallas import tpu as pltpu

D_MODEL = 2048
DEPTH = 2
N_HEADS = 8
HEAD_DIM = 64
HEAD_WIDTH = 2 * HEAD_DIM
ATTN_WIDTH = N_HEADS * HEAD_WIDTH
POOL_WINDOWS = (2, 4, 8, 16)
POOL_GROUP = 256
POOL_WIDTH = POOL_GROUP * len(POOL_WINDOWS)
MAX_WINDOW = max(POOL_WINDOWS)
N_BUCKETS = 32
MAX_DISTANCE = 128
N_EXPERTS = 8
TOP_K = 2
ALPHA = (2.0 * DEPTH) ** 0.25
LN_EPS = 1e-5
LOG2E = math.log2(math.e)
Q_SCALE = HEAD_DIM ** -0.5 * LOG2E
ONES_ROWS = 16

LANES = 128
SUBLANES = 8
MASK_VALUE = -1e30
VMEM_LIMIT = 56 * 1024 * 1024

PROJ_TM = 512
PROJ_TN = 1024
ATTN_TQ = 512
ATTN_TK = 512
TAIL_TM = 256
FFN_TM = 512
FFN_TF = 512
ROUTER_TM = 512
DISPATCH_TM = 512
MOE_TM = 512
MOE_TF = 1024
COMBINE_TM = 256
ROW_DMA_UNROLL = 8

F32 = jnp.float32
BF16 = jnp.bfloat16


def _params(*sem):
    return pltpu.CompilerParams(dimension_semantics=sem, vmem_limit_bytes=VMEM_LIMIT)


def _layer_norm(r, g, b):
    mu = jnp.mean(r, axis=-1, keepdims=True)
    c = r - mu
    var = jnp.mean(c * c, axis=-1, keepdims=True)
    return c * lax.rsqrt(var + LN_EPS) * g + b


PROJ_HALF = 2 * D_MODEL
assert 3 * ATTN_WIDTH + POOL_WIDTH == PROJ_HALF


def _proj_mix_kernel(x_ref, w_ref, qkv_ref, u_ref):
    x = x_ref[...].astype(BF16)
    for j in range(PROJ_HALF // PROJ_TN):
        cols = slice(j * PROJ_TN, (j + 1) * PROJ_TN)
        acc = jnp.dot(x, w_ref[:, cols], preferred_element_type=F32)
        if cols.stop <= ATTN_WIDTH:
            qkv_ref[:, cols] = (acc * Q_SCALE).astype(BF16)
        elif cols.stop <= 3 * ATTN_WIDTH:
            qkv_ref[:, cols] = acc.astype(BF16)
        else:
            u_ref[:, j * PROJ_TN - 3 * ATTN_WIDTH:(j + 1) * PROJ_TN - 3 * ATTN_WIDTH] = acc


def _proj_gates_kernel(x_ref, w_ref, g_ref):
    x = x_ref[...].astype(BF16)
    for j in range(PROJ_HALF // PROJ_TN):
        cols = slice(j * PROJ_TN, (j + 1) * PROJ_TN)
        g_ref[:, cols] = jax.nn.sigmoid(jnp.dot(x, w_ref[:, cols], preferred_element_type=F32))


def _in_proj(x, w):
    t, k = x.shape
    tm = PROJ_TM
    assert ATTN_WIDTH % PROJ_TN == 0 and POOL_WIDTH % PROJ_TN == 0
    x_spec = pl.BlockSpec((tm, k), lambda i: (i, 0))

    def w_spec(half):
        return pl.BlockSpec((k, PROJ_HALF), lambda i: (0, half), pipeline_mode=pl.Buffered(1))

    qkv, u = pl.pallas_call(
        _proj_mix_kernel,
        grid=(t // tm,),
        in_specs=[x_spec, w_spec(0)],
        out_specs=[
            pl.BlockSpec((tm, 3 * ATTN_WIDTH), lambda i: (i, 0)),
            pl.BlockSpec((tm, POOL_WIDTH), lambda i: (i, 0)),
        ],
        out_shape=[
            jax.ShapeDtypeStruct((t, 3 * ATTN_WIDTH), BF16),
            jax.ShapeDtypeStruct((t, POOL_WIDTH), F32),
        ],
        compiler_params=_params("parallel"),
        name="in_proj_mix",
    )(x, w)
    gates = pl.pallas_call(
        _proj_gates_kernel,
        grid=(t // tm,),
        in_specs=[x_spec, w_spec(1)],
        out_specs=pl.BlockSpec((tm, PROJ_HALF), lambda i: (i, 0)),
        out_shape=jax.ShapeDtypeStruct((t, PROJ_HALF), F32),
        compiler_params=_params("parallel"),
        name="in_proj_gates",
    )(x, w)
    return qkv, u, gates


def _bias_kernel(rb_ref, o_ref):
    h = pl.program_id(0)
    tk, tq = o_ref.shape[2], o_ref.shape[3]
    kpos = lax.broadcasted_iota(jnp.int32, (tk, tq), 0)
    qpos = lax.broadcasted_iota(jnp.int32, (tk, tq), 1)
    max_exact = N_BUCKETS // 2
    far = rb_ref[h * N_BUCKETS + N_BUCKETS - 1]
    for blk in range(2):
        dist = qpos - kpos + blk * tk
        n = jnp.maximum(dist, 0)
        large = max_exact + (
            jnp.log(jnp.maximum(n, 1).astype(F32) / max_exact) / math.log(MAX_DISTANCE / max_exact)
            * (N_BUCKETS - max_exact)
        ).astype(jnp.int32)
        large = jnp.minimum(large, N_BUCKETS - 1)
        bucket = jnp.where(n < max_exact, n, large)
        bias = jnp.zeros((tk, tq), F32)
        for b in range(N_BUCKETS):
            bias = jnp.where(bucket == b, rb_ref[h * N_BUCKETS + b], bias)
        bias = (bias - far) * LOG2E
        if blk == 0:
            bias = jnp.where(dist >= 0, bias, MASK_VALUE)
        o_ref[0, blk] = bias


def _bias_tiles(rel_bias):
    assert ATTN_TK >= MAX_DISTANCE
    rb = jnp.transpose(rel_bias).reshape(-1)
    return pl.pallas_call(
        _bias_kernel,
        grid_spec=pltpu.PrefetchScalarGridSpec(
            num_scalar_prefetch=1,
            grid=(N_HEADS,),
            in_specs=[],
            out_specs=pl.BlockSpec((1, 2, ATTN_TK, ATTN_TQ), lambda h, rb: (h, 0, 0, 0)),
        ),
        out_shape=jax.ShapeDtypeStruct((N_HEADS, 2, ATTN_TK, ATTN_TQ), F32),
        compiler_params=_params("arbitrary"),
        name="bias_tiles",
    )(rb)


def _attn_kernel(q_ref, k_ref, v_ref, bias_ref, lam_ref, sub_ref, o_ref, qt_ref, vt_ref, m_ref, acc_ref,
                 s0_ref, pp_ref, pa_ref, *, lambda_init):
    qi = pl.program_id(2)
    tq = q_ref.shape[1]
    n_kb, _, tk = vt_ref.shape

    @pl.when(qi == 0)
    def _():
        row = lax.broadcasted_iota(jnp.int32, (ONES_ROWS, tk), 0)
        ones_rows = jnp.where(row == 0, 1.0, 0.0).astype(BF16)
        for kb in range(n_kb):
            vt_ref[kb, 0:HEAD_WIDTH, :] = v_ref[0, kb * tk:(kb + 1) * tk, :].astype(F32).T.astype(BF16)
            vt_ref[kb, HEAD_WIDTH:HEAD_WIDTH + ONES_ROWS, :] = ones_rows

    qt = q_ref[0].astype(F32).T
    dim = lax.broadcasted_iota(jnp.int32, (HEAD_WIDTH, tq), 0)
    qt_ref[0] = jnp.where(dim < HEAD_DIM, qt, 0.0).astype(BF16)
    qt_ref[1] = jnp.where(dim >= HEAD_DIM, qt, 0.0).astype(BF16)

    m_ref[...] = jnp.full(m_ref.shape, MASK_VALUE, F32)
    acc_ref[...] = jnp.zeros(acc_ref.shape, F32)

    half = tq // 2
    chunks = [(c, slice(hf * half, (hf + 1) * half)) for c in range(2) for hf in range(2)]

    def keys(kb):
        return k_ref[0, pl.ds(pl.multiple_of(kb * tk, tk), tk), :]

    def key_rows(kind, chunk):
        return tk // 2 if kind == "diag" and chunk[1].start == 0 else tk

    def scores(kk, chunk, kind="far"):
        c, cols = chunk
        return jnp.dot(kk[:key_rows(kind, chunk)], qt_ref[c, :, cols], preferred_element_type=F32)

    def add_bias(s, chunk, kind):
        cols = chunk[1]
        if kind == "diag":
            return s + bias_ref[0, 0, 0:s.shape[0], cols]
        if kind == "near" and cols.start == 0:
            r0 = tk - MAX_DISTANCE
            return jnp.concatenate([s[:r0], s[r0:] + bias_ref[0, 1, r0:tk, cols]], axis=0)
        return s

    def softmax_update(s, chunk, kind):
        c, cols = chunk
        s = add_bias(s, chunk, kind)
        m_prev = m_ref[c, :, cols]
        m_new = jnp.maximum(m_prev, jnp.max(s, axis=0, keepdims=True))
        alpha = jnp.exp2(m_prev - m_new)
        p = jnp.exp2(s - m_new)
        m_ref[c, :, cols] = m_new
        return p.astype(BF16), alpha

    def accumulate(vt, p, alpha, chunk):
        c, cols = chunk
        pv = jnp.dot(vt[:, :p.shape[0]], p, preferred_element_type=F32)
        acc_ref[c, :, cols] = alpha * acc_ref[c, :, cols] + pv

    def accumulate_pending(kb):
        accumulate(vt_ref[kb], pp_ref[...], pa_ref[...], chunks[3])

    def process(kb, kind):
        kk = keys(kb)
        vt = vt_ref[kb]
        s1 = scores(kk, chunks[1], kind)
        accumulate_pending(jnp.maximum(kb - 1, 0))
        p0, a0 = softmax_update(s0_ref[0:key_rows(kind, chunks[0]), :], chunks[0], kind)
        s2 = scores(kk, chunks[2], kind)
        accumulate(vt, p0, a0, chunks[0])
        p1, a1 = softmax_update(s1, chunks[1], kind)
        s3 = scores(kk, chunks[3], kind)
        accumulate(vt, p1, a1, chunks[1])
        p2, a2 = softmax_update(s2, chunks[2], kind)
        if kind != "diag":
            s0_ref[...] = scores(keys(kb + 1), chunks[0])
        accumulate(vt, p2, a2, chunks[2])
        pp_ref[...], pa_ref[...] = softmax_update(s3, chunks[3], kind)

    s0_ref[...] = scores(keys(0), chunks[0])
    pp_ref[...] = jnp.zeros(pp_ref.shape, BF16)
    pa_ref[...] = jnp.ones(pa_ref.shape, F32)

    n_far = jnp.maximum(qi - 1, 0)

    def far_run(first, count):
        for j in range(count):
            process(first + j, "far")

    def far_quad(j, carry):
        far_run(4 * j, 4)
        return carry

    lax.fori_loop(0, n_far // 4, far_quad, 0)

    @pl.when(n_far % 4 >= 2)
    def _():
        far_run(n_far // 4 * 4, 2)

    @pl.when(n_far % 2 == 1)
    def _():
        far_run(n_far - 1, 1)

    @pl.when(qi >= 1)
    def _():
        process(qi - 1, "near")
        process(qi, "diag")

    @pl.when(qi == 0)
    def _():
        process(0, "diag")

    accumulate_pending(qi)

    lv = lam_ref[...]
    lam = (
        jnp.exp(jnp.sum(lv[0:1] * lv[1:2], axis=-1, keepdims=True))
        - jnp.exp(jnp.sum(lv[2:3] * lv[3:4], axis=-1, keepdims=True))
        + lambda_init
    )
    num = [acc_ref[c, 0:HEAD_WIDTH, :] for c in range(2)]
    den = [acc_ref[c, HEAD_WIDTH:HEAD_WIDTH + 1, :] for c in range(2)]
    ot = num[0] / den[0] - lam * (num[1] / den[1])
    ot = ot * lax.rsqrt(jnp.mean(ot * ot, axis=0, keepdims=True) + LN_EPS)
    o = ot.T * (sub_ref[...] * (1.0 - lambda_init))
    o_ref[0] = o.astype(o_ref.dtype)


def _attention(qkv, bias_tiles, lam_vec, subln_w, lambda_init):
    b, s, _ = qkv.shape
    tq, tk = ATTN_TQ, ATTN_TK
    assert tq == tk and s % tq == 0
    assert tq // 2 >= MAX_DISTANCE and (tk - MAX_DISTANCE) % 8 == 0
    kernel = functools.partial(_attn_kernel, lambda_init=lambda_init)
    return pl.pallas_call(
        kernel,
        grid=(b, N_HEADS, s // tq),
        in_specs=[
            pl.BlockSpec((1, tq, HEAD_WIDTH), lambda bi, h, qi: (bi, qi, h)),
            pl.BlockSpec((1, s, HEAD_WIDTH), lambda bi, h, qi: (bi, 0, N_HEADS + h)),
            pl.BlockSpec((1, s, HEAD_WIDTH), lambda bi, h, qi: (bi, 0, 2 * N_HEADS + h)),
            pl.BlockSpec((1, 2, tk, tq), lambda bi, h, qi: (h, 0, 0, 0)),
            pl.BlockSpec((4, HEAD_DIM), lambda bi, h, qi: (0, 0)),
            pl.BlockSpec((1, HEAD_WIDTH), lambda bi, h, qi: (0, 0)),
        ],
        out_specs=pl.BlockSpec((1, tq, HEAD_WIDTH), lambda bi, h, qi: (bi, qi, h)),
        out_shape=jax.ShapeDtypeStruct((b, s, ATTN_WIDTH), BF16),
        scratch_shapes=[
            pltpu.VMEM((2, HEAD_WIDTH, tq), BF16),
            pltpu.VMEM((s // tk, HEAD_WIDTH + ONES_ROWS, tk), BF16),
            pltpu.VMEM((2, 1, tq), F32),
            pltpu.VMEM((2, HEAD_WIDTH + ONES_ROWS, tq), F32),
            pltpu.VMEM((tk, tq // 2), F32),
            pltpu.VMEM((tk, tq // 2), BF16),
            pltpu.VMEM((1, tq // 2), F32),
        ],
        compiler_params=_params("parallel", "parallel", "arbitrary"),
        name="diff_attention",
    )(qkv, qkv, qkv, bias_tiles, lam_vec, subln_w.reshape(1, HEAD_WIDTH))


def _tail_kernel(attn_ref, u_ref, halo_ref, ga_ref, gp_ref, x_ref, pw_ref, ps_ref, wba_ref, wbp_ref, wout_ref,
                 g_ref, b_ref, of_ref, ob_ref, sums_ref, *, tiles_per_seq):
    i = pl.program_id(0)
    tm = u_ref.shape[0]
    tile_in_seq = i % tiles_per_seq
    r0 = SUBLANES + MAX_WINDOW
    n = r0 + tm
    sums_ref[:, 0:SUBLANES, :] = jnp.zeros((sums_ref.shape[0], SUBLANES, POOL_WIDTH), F32)
    sums_ref[0, SUBLANES:r0, :] = jnp.where(tile_in_seq == 0, 0.0, halo_ref[...])
    sums_ref[0, r0:n, :] = u_ref[...]
    t = tile_in_seq * tm + lax.broadcasted_iota(jnp.int32, (tm, 1), 0)

    def doubled(k, lo, cs):
        shift = 2 ** k
        return sums_ref[k, lo:n, cs] + sums_ref[k, lo - shift:n - shift, cs]

    n_groups = len(POOL_WINDOWS)
    a_cols = D_MODEL // n_groups
    attn = attn_ref[...]

    def attn_branch_chunk(j):
        return jnp.dot(attn, wba_ref[:, j * a_cols:(j + 1) * a_cols], preferred_element_type=F32)

    a_chunks = [attn_branch_chunk(0)]
    for k in range(n_groups - 1):
        cs = slice((k + 1) * POOL_GROUP, POOL_WIDTH)
        sums_ref[k + 1, SUBLANES:n, cs] = doubled(k, SUBLANES, cs)

    pooled = []
    for g, w in enumerate(POOL_WINDOWS):
        if g > 0:
            a_chunks.append(attn_branch_chunk(g))
        cs = slice(g * POOL_GROUP, (g + 1) * POOL_GROUP)
        cur = sums_ref[0, r0:n, cs]
        win_sum = doubled(g, r0, cs)
        cnt = jnp.minimum(t + 1, w).astype(F32)
        z = (win_sum / cnt - cur).astype(BF16)
        y = jnp.dot(z, pw_ref[g], preferred_element_type=F32)
        pooled.append((y * ps_ref[:, cs]).astype(BF16))
    pool = jnp.concatenate(pooled, axis=-1)

    a = jnp.concatenate(a_chunks, axis=-1)
    p = jnp.dot(pool, wbp_ref[...], preferred_element_type=F32)
    merged = (ga_ref[...] * a + gp_ref[...] * p).astype(BF16)

    hm = tm // 2
    top, bottom = slice(0, hm), slice(hm, tm)
    o_cols = D_MODEL // 4

    def bottom_chunk(j):
        return jnp.dot(merged[bottom], wout_ref[:, j * o_cols:(j + 1) * o_cols], preferred_element_type=F32)

    def store(rows, out):
        of_ref[rows, :] = out
        ob_ref[rows, :] = out.astype(BF16)

    r_top = ALPHA * x_ref[top, :] + jnp.dot(merged[top], wout_ref[...], preferred_element_type=F32)
    chunks = [bottom_chunk(0)]
    centered = r_top - jnp.mean(r_top, axis=-1, keepdims=True)
    chunks.append(bottom_chunk(1))
    inv_std = lax.rsqrt(jnp.mean(centered * centered, axis=-1, keepdims=True) + LN_EPS)
    chunks.append(bottom_chunk(2))
    store(top, centered * inv_std * g_ref[...] + b_ref[...])
    chunks.append(bottom_chunk(3))
    r_bottom = ALPHA * x_ref[bottom, :] + jnp.concatenate(chunks, axis=-1)
    store(bottom, _layer_norm(r_bottom, g_ref[...], b_ref[...]))


def _resident(shape):
    zeros = (0,) * len(shape)
    return pl.BlockSpec(shape, lambda i: zeros, pipeline_mode=pl.Buffered(1))


def _mixer_tail(attn, u, gates, x, pool_w, pool_scale, w_ba, w_bp, w_out, ln_g, ln_b, seq):
    t = x.shape[0]
    tm = TAIL_TM
    assert seq % tm == 0 and tm % MAX_WINDOW == 0
    assert POOL_WINDOWS == tuple(2 ** (g + 1) for g in range(len(POOL_WINDOWS))) and MAX_WINDOW >= SUBLANES
    halo_blocks = tm // MAX_WINDOW
    kernel = functools.partial(_tail_kernel, tiles_per_seq=seq // tm)
    return pl.pallas_call(
        kernel,
        grid=(t // tm,),
        in_specs=[
            pl.BlockSpec((tm, ATTN_WIDTH), lambda i: (i, 0)),
            pl.BlockSpec((tm, POOL_WIDTH), lambda i: (i, 0)),
            pl.BlockSpec((MAX_WINDOW, POOL_WIDTH), lambda i: (jnp.maximum(i * halo_blocks - 1, 0), 0)),
            pl.BlockSpec((tm, D_MODEL), lambda i: (i, 0)),
            pl.BlockSpec((tm, D_MODEL), lambda i: (i, 1)),
            pl.BlockSpec((tm, D_MODEL), lambda i: (i, 0)),
            _resident(pool_w.shape),
            _resident((1, POOL_WIDTH)),
            _resident(w_ba.shape),
            _resident(w_bp.shape),
            _resident(w_out.shape),
            _resident((1, D_MODEL)),
            _resident((1, D_MODEL)),
        ],
        out_specs=[
            pl.BlockSpec((tm, D_MODEL), lambda i: (i, 0)),
            pl.BlockSpec((tm, D_MODEL), lambda i: (i, 0)),
        ],
        out_shape=[
            jax.ShapeDtypeStruct((t, D_MODEL), F32),
            jax.ShapeDtypeStruct((t, D_MODEL), BF16),
        ],
        scratch_shapes=[pltpu.VMEM((len(POOL_WINDOWS), SUBLANES + MAX_WINDOW + tm, POOL_WIDTH), F32)],
        compiler_params=_params("parallel"),
        name="mixer_tail",
    )(attn, u, u, gates, gates, x, pool_w, pool_scale.reshape(1, POOL_WIDTH), w_ba, w_bp, w_out,
      ln_g.reshape(1, D_MODEL), ln_b.reshape(1, D_MODEL))


def _swiglu(hb, wg, wu):
    g = jnp.dot(hb, wg, preferred_element_type=F32)
    u = jnp.dot(hb, wu, preferred_element_type=F32)
    return (g * jax.nn.sigmoid(g) * u).astype(BF16)


def _dense_ffn_kernel(hb_ref, hf_ref, wg_ref, wu_ref, wd_ref, g_ref, b_ref, of_ref, ob_ref, acc_ref):
    f = pl.program_id(1)

    @pl.when(f == 0)
    def _():
        acc_ref[...] = jnp.zeros(acc_ref.shape, F32)

    a = _swiglu(hb_ref[...], wg_ref[...], wu_ref[...])
    acc_ref[...] += jnp.dot(a, wd_ref[...], preferred_element_type=F32)

    @pl.when(f == pl.num_programs(1) - 1)
    def _():
        out = _layer_norm(ALPHA * hf_ref[...] + acc_ref[...], g_ref[...], b_ref[...])
        of_ref[...] = out
        ob_ref[...] = out.astype(BF16)


def _dense_ffn(hb, hf, wg, wu, wd, ln_g, ln_b):
    t = hb.shape[0]
    d_ff = wg.shape[1]
    tm, tf = FFN_TM, FFN_TF
    assert d_ff % tf == 0
    return pl.pallas_call(
        _dense_ffn_kernel,
        grid=(t // tm, d_ff // tf),
        in_specs=[
            pl.BlockSpec((tm, D_MODEL), lambda i, f: (i, 0)),
            pl.BlockSpec((tm, D_MODEL), lambda i, f: (i, 0)),
            pl.BlockSpec((D_MODEL, tf), lambda i, f: (0, f)),
            pl.BlockSpec((D_MODEL, tf), lambda i, f: (0, f)),
            pl.BlockSpec((tf, D_MODEL), lambda i, f: (f, 0)),
            pl.BlockSpec((1, D_MODEL), lambda i, f: (0, 0)),
            pl.BlockSpec((1, D_MODEL), lambda i, f: (0, 0)),
        ],
        out_specs=[
            pl.BlockSpec((tm, D_MODEL), lambda i, f: (i, 0)),
            pl.BlockSpec((tm, D_MODEL), lambda i, f: (i, 0)),
        ],
        out_shape=[
            jax.ShapeDtypeStruct((t, D_MODEL), F32),
            jax.ShapeDtypeStruct((t, D_MODEL), BF16),
        ],
        scratch_shapes=[pltpu.VMEM((tm, D_MODEL), F32)],
        compiler_params=_params("parallel", "arbitrary"),
        name="dense_ffn",
    )(hb, hf, wg, wu, wd, ln_g.reshape(1, D_MODEL), ln_b.reshape(1, D_MODEL))


ROUTE_W0, ROUTE_W1, ROUTE_E0, ROUTE_E1, ROUTE_R0, ROUTE_R1 = range(6)


def _router_kernel(h_ref, rw_ref, route_ref, counts_ref, carry_ref):
    i = pl.program_id(0)
    tm = h_ref.shape[0]

    @pl.when(i == 0)
    def _():
        carry_ref[...] = jnp.zeros(carry_ref.shape, F32)

    logits = jnp.dot(h_ref[...], rw_ref[...], preferred_element_type=F32)
    lane = lax.broadcasted_iota(jnp.int32, (tm, LANES), 1)
    lg = jnp.where(lane < N_EXPERTS, logits, -jnp.inf)
    m1 = jnp.max(lg, axis=-1, keepdims=True)
    e1 = jnp.min(jnp.where(lg == m1, lane, LANES), axis=-1, keepdims=True)
    lg2 = jnp.where(lane == e1, -jnp.inf, lg)
    m2 = jnp.max(lg2, axis=-1, keepdims=True)
    e2 = jnp.min(jnp.where(lg2 == m2, lane, LANES), axis=-1, keepdims=True)
    x2 = jnp.exp(m2 - m1)
    w1 = 1.0 / (1.0 + x2)
    w2 = x2 / (1.0 + x2)

    sel1 = lane == e1
    sel2 = lane == e2
    mask = jnp.where(sel1 | sel2, 1.0, 0.0)
    r = lax.broadcasted_iota(jnp.int32, (tm, tm), 0)
    c = lax.broadcasted_iota(jnp.int32, (tm, tm), 1)
    tri = jnp.where(r >= c, 1.0, 0.0).astype(BF16)
    incl = jnp.dot(tri, mask.astype(BF16), preferred_element_type=F32)
    rank = incl - mask + carry_ref[...]
    total = carry_ref[...] + jnp.sum(mask, axis=0, keepdims=True)
    carry_ref[...] = total
    counts_ref[...] = total

    r1 = jnp.sum(jnp.where(sel1, rank, 0.0), axis=-1, keepdims=True)
    r2 = jnp.sum(jnp.where(sel2, rank, 0.0), axis=-1, keepdims=True)
    cols = (w1, w2, e1.astype(F32), e2.astype(F32), r1, r2)
    route = jnp.zeros((tm, LANES), F32)
    for idx, val in enumerate(cols):
        route = jnp.where(lane == idx, val, route)
    route_ref[...] = route


def _router(hb, router_w):
    t = hb.shape[0]
    tm = ROUTER_TM
    rw = jnp.pad(router_w, ((0, 0), (0, LANES - N_EXPERTS))).astype(BF16)
    return pl.pallas_call(
        _router_kernel,
        grid=(t // tm,),
        in_specs=[
            pl.BlockSpec((tm, D_MODEL), lambda i: (i, 0)),
            pl.BlockSpec((D_MODEL, LANES), lambda i: (0, 0)),
        ],
        out_specs=[
            pl.BlockSpec((tm, LANES), lambda i: (i, 0)),
            pl.BlockSpec((1, LANES), lambda i: (0, 0)),
        ],
        out_shape=[
            jax.ShapeDtypeStruct((t, LANES), F32),
            jax.ShapeDtypeStruct((1, LANES), F32),
        ],
        scratch_shapes=[pltpu.VMEM((1, LANES), F32)],
        compiler_params=_params("arbitrary"),
        name="moe_router",
    )(hb, rw)


def _dispatch_kernel(pad_start_ref, pad_count_ref, n_used_ref, dest_ref, h_ref, xs_hbm, zero_ref, sem, pad_sem):
    n = h_ref.shape[0]
    tile_rows = zero_ref.shape[0]
    n_tiles = xs_hbm.shape[0] // tile_rows

    @pl.when(pl.program_id(0) == 0)
    def _():
        zero_ref[...] = jnp.zeros(zero_ref.shape, F32)
        for e in range(N_EXPERTS):
            def pad_copy(j, e=e):
                return pltpu.make_async_copy(
                    zero_ref.at[pl.ds(0, 1)], xs_hbm.at[pl.ds(pad_start_ref[e] + j, 1)], pad_sem)

            lax.fori_loop(0, pad_count_ref[e], lambda j, c: (pad_copy(j).start(), c)[1], 0)
            lax.fori_loop(0, pad_count_ref[e], lambda j, c: (pad_copy(j).wait(), c)[1], 0)

        def tile_copy(j):
            return pltpu.make_async_copy(
                zero_ref, xs_hbm.at[pl.ds(pl.multiple_of(j * tile_rows, tile_rows), tile_rows)], pad_sem)

        lax.fori_loop(n_used_ref[0], n_tiles, lambda j, c: (tile_copy(j).start(), c)[1], 0)
        lax.fori_loop(n_used_ref[0], n_tiles, lambda j, c: (tile_copy(j).wait(), c)[1], 0)

    def row_copy(r, k):
        return pltpu.make_async_copy(
            h_ref.at[pl.ds(r, 1)], xs_hbm.at[pl.ds(dest_ref[0, 0, TOP_K * r + k], 1)], sem)

    def start(r, carry):
        for k in range(TOP_K):
            row_copy(r, k).start()
        return carry

    def wait(r, carry):
        for k in range(TOP_K):
            row_copy(r, k).wait()
        return carry

    lax.fori_loop(0, n, start, 0, unroll=ROW_DMA_UNROLL)
    lax.fori_loop(0, n, wait, 0, unroll=ROW_DMA_UNROLL)


def _dispatch(hf, dest, pad_start, pad_count, n_used, n_tiles):
    t = hf.shape[0]
    tm = DISPATCH_TM
    return pl.pallas_call(
        _dispatch_kernel,
        grid_spec=pltpu.PrefetchScalarGridSpec(
            num_scalar_prefetch=3,
            grid=(t // tm,),
            in_specs=[
                pl.BlockSpec((1, 1, TOP_K * tm), lambda i, ps, pc, nu: (i, 0, 0), memory_space=pltpu.SMEM),
                pl.BlockSpec((tm, D_MODEL), lambda i, ps, pc, nu: (i, 0)),
            ],
            out_specs=pl.BlockSpec(memory_space=pl.ANY),
            scratch_shapes=[
                pltpu.VMEM((MOE_TM, D_MODEL), F32),
                pltpu.SemaphoreType.DMA(()),
                pltpu.SemaphoreType.DMA(()),
            ],
        ),
        out_shape=jax.ShapeDtypeStruct((n_tiles * MOE_TM, D_MODEL), F32),
        compiler_params=_params("arbitrary"),
        name="moe_dispatch",
    )(pad_start, pad_count, n_used, dest.reshape(t // tm, 1, TOP_K * tm), hf)


def _moe_ffn_kernel(te_ref, ok_ref, src_ref, xs_ref, wg_ref, wu_ref, wd_ref, ys_ref, xb_ref, acc_ref):
    del te_ref, src_ref
    i = pl.program_id(0)
    f = pl.program_id(1)
    ok = ok_ref[i] == 1

    @pl.when(ok & (f == 0))
    def _():
        xb_ref[...] = xs_ref[...].astype(BF16)
        acc_ref[...] = jnp.zeros(acc_ref.shape, F32)

    @pl.when(ok)
    def _():
        a = _swiglu(xb_ref[...], wg_ref[...], wu_ref[...])
        acc_ref[...] += jnp.dot(a, wd_ref[...], preferred_element_type=F32)

    @pl.when(f == pl.num_programs(1) - 1)
    def _():
        ys_ref[...] = jnp.where(ok, acc_ref[...], 0.0)


def _moe_ffn(xs, tile_expert, tile_ok, tile_src, wg, wu, wd):
    n_rows = xs.shape[0]
    d_ff = wg.shape[2]
    tm, tf = MOE_TM, MOE_TF
    assert d_ff % tf == 0
    n_f = d_ff // tf
    n_tiles = n_rows // tm

    def f_idx(i, f, ok):
        return jnp.where(ok[i] == 1, f, n_f - 1)

    return pl.pallas_call(
        _moe_ffn_kernel,
        grid_spec=pltpu.PrefetchScalarGridSpec(
            num_scalar_prefetch=3,
            grid=(n_tiles, n_f),
            in_specs=[
                pl.BlockSpec((tm, D_MODEL), lambda i, f, te, ok, src: (src[i], 0)),
                pl.BlockSpec((None, D_MODEL, tf), lambda i, f, te, ok, src: (te[i], 0, f_idx(i, f, ok))),
                pl.BlockSpec((None, D_MODEL, tf), lambda i, f, te, ok, src: (te[i], 0, f_idx(i, f, ok))),
                pl.BlockSpec((None, tf, D_MODEL), lambda i, f, te, ok, src: (te[i], f_idx(i, f, ok), 0)),
            ],
            out_specs=pl.BlockSpec((tm, D_MODEL), lambda i, f, te, ok, src: (i, 0)),
            scratch_shapes=[pltpu.VMEM((tm, D_MODEL), BF16), pltpu.VMEM((tm, D_MODEL), F32)],
        ),
        out_shape=jax.ShapeDtypeStruct((n_rows, D_MODEL), F32),
        compiler_params=_params("arbitrary", "arbitrary"),
        name="moe_ffn",
    )(tile_expert, tile_ok, tile_src, xs, wg, wu, wd)


def _combine_kernel(dest_ref, h_ref, route_ref, ys_hbm, g_ref, b_ref, o_ref, y_ref, sem):
    tm = h_ref.shape[0]

    def row_copy(r, k):
        return pltpu.make_async_copy(
            ys_hbm.at[pl.ds(dest_ref[0, 0, TOP_K * r + k], 1)], y_ref.at[k, pl.ds(r, 1)], sem)

    def start(r, carry):
        for k in range(TOP_K):
            row_copy(r, k).start()
        return carry

    def wait(r, carry):
        for k in range(TOP_K):
            row_copy(r, k).wait()
        return carry

    lax.fori_loop(0, tm, start, 0, unroll=ROW_DMA_UNROLL)
    lax.fori_loop(0, tm, wait, 0, unroll=ROW_DMA_UNROLL)

    route = route_ref[...]
    w1 = route[:, ROUTE_W0:ROUTE_W0 + 1]
    w2 = route[:, ROUTE_W1:ROUTE_W1 + 1]
    y = w1 * y_ref[0] + w2 * y_ref[1]
    o_ref[...] = _layer_norm(ALPHA * h_ref[...] + y, g_ref[...], b_ref[...])


def _combine(hf, route, dest, ys, ln_g, ln_b):
    t = hf.shape[0]
    tm = COMBINE_TM
    return pl.pallas_call(
        _combine_kernel,
        grid=(t // tm,),
        in_specs=[
            pl.BlockSpec((1, 1, TOP_K * tm), lambda i: (i, 0, 0), memory_space=pltpu.SMEM),
            pl.BlockSpec((tm, D_MODEL), lambda i: (i, 0)),
            pl.BlockSpec((tm, LANES), lambda i: (i, 0)),
            pl.BlockSpec(memory_space=pl.ANY),
            pl.BlockSpec((1, D_MODEL), lambda i: (0, 0)),
            pl.BlockSpec((1, D_MODEL), lambda i: (0, 0)),
        ],
        out_specs=pl.BlockSpec((tm, D_MODEL), lambda i: (i, 0)),
        out_shape=jax.ShapeDtypeStruct((t, D_MODEL), F32),
        scratch_shapes=[pltpu.VMEM((TOP_K, tm, D_MODEL), F32), pltpu.SemaphoreType.DMA(())],
        compiler_params=_params("arbitrary"),
        name="moe_combine",
    )(dest.reshape(t // tm, 1, TOP_K * tm), hf, route, ys, ln_g.reshape(1, D_MODEL), ln_b.reshape(1, D_MODEL))


def _moe(hf, hb, router_w, wg, wu, wd, ln_g, ln_b):
    t = hf.shape[0]
    tm = MOE_TM
    n_tiles = (TOP_K * t) // tm + N_EXPERTS
    route, counts = _router(hb, router_w)

    counts = counts[0, :N_EXPERTS].astype(jnp.int32)
    tiles_per_expert = (counts + tm - 1) // tm
    tile_end = jnp.cumsum(tiles_per_expert)
    group_start = (tile_end - tiles_per_expert) * tm
    tile_ids = jnp.arange(n_tiles, dtype=jnp.int32)
    n_used = tile_end[-1]
    last_used = jnp.maximum(n_used - 1, 0)
    expert_of_tile = jnp.sum(tile_ids[:, None] >= tile_end[None, :], axis=1).astype(jnp.int32)
    tile_ok = (tile_ids < n_used).astype(jnp.int32)
    tile_expert = jnp.where(tile_ok == 1, expert_of_tile, expert_of_tile[last_used])
    tile_expert = jnp.minimum(tile_expert, N_EXPERTS - 1)
    tile_src = jnp.where(tile_ok == 1, tile_ids, last_used)

    experts = route[:, ROUTE_E0:ROUTE_E1 + 1].astype(jnp.int32)
    ranks = route[:, ROUTE_R0:ROUTE_R1 + 1].astype(jnp.int32)
    dest = (group_start[experts] + ranks).reshape(-1)

    xs = _dispatch(hf, dest, group_start + counts, tiles_per_expert * tm - counts, n_used.reshape(1), n_tiles)
    ys = _moe_ffn(xs, tile_expert, tile_ok, tile_src, wg, wu, wd)
    return _combine(hf, route, dest, ys, ln_g, ln_b)


def kernel(x, w_in, lambdas, subln_w, pool_w, pool_scale, w_branch_attn, w_branch_pool, w_out, rel_bias,
           ln1_g, ln1_b, dense_w_gate, dense_w_up, dense_w_down, router_w, moe_w_gate, moe_w_up, moe_w_down,
           ln2_g, ln2_b):
    b, s, d = x.shape
    t = b * s
    xf = x.reshape(t, d)
    xin = xf
    bias_tiles = _bias_tiles(rel_bias)
    for l in range(DEPTH):
        lambda_init = 0.8 - 0.6 * math.exp(-0.3 * l)
        qkv, u, gates = _in_proj(xin, w_in[l].astype(BF16))
        attn = _attention(qkv.reshape(b, s, 3 * ATTN_WIDTH), bias_tiles, lambdas[l], subln_w[l], lambda_init)
        hf, hb = _mixer_tail(attn.reshape(t, ATTN_WIDTH), u, gates, xf, pool_w[l].astype(BF16), pool_scale[l],
                             w_branch_attn[l].astype(BF16), w_branch_pool[l].astype(BF16), w_out[l].astype(BF16),
                             ln1_g[l], ln1_b[l], s)
        if l % 2 == 0:
            xf, xin = _dense_ffn(hb, hf, dense_w_gate[l // 2].astype(BF16), dense_w_up[l // 2].astype(BF16),
                                 dense_w_down[l // 2].astype(BF16), ln2_g[l], ln2_b[l])
        else:
            xf = _moe(hf, hb, router_w[l // 2], moe_w_gate[l // 2].astype(BF16), moe_w_up[l // 2].astype(BF16),
                      moe_w_down[l // 2].astype(BF16), ln2_g[l], ln2_b[l])
            xin = xf
    return xf.reshape(b, s, d)
```

```python
import collections
import functools
import math

import jax
import jax.numpy as jnp
from jax import lax
from jax.experimental import pallas as pl
from jax.experimental.pallas import tpu as pltpu

D_MODEL = 2048
DEPTH = 2
N_HEADS = 8
HEAD_DIM = 64
HEAD_WIDTH = 2 * HEAD_DIM
ATTN_WIDTH = N_HEADS * HEAD_WIDTH
POOL_WINDOWS = (2, 4, 8, 16)
POOL_GROUP = 256
POOL_WIDTH = POOL_GROUP * len(POOL_WINDOWS)
MAX_WINDOW = max(POOL_WINDOWS)
N_BUCKETS = 32
MAX_DISTANCE = 128
N_EXPERTS = 8
TOP_K = 2
ALPHA = (2.0 * DEPTH) ** 0.25
LN_EPS = 1e-5
LOG2E = math.log2(math.e)
Q_SCALE = HEAD_DIM ** -0.5 * LOG2E
ONES_ROWS = 16

LANES = 128
SUBLANES = 8
MASK_VALUE = -1e30
VMEM_LIMIT = 56 * 1024 * 1024

PROJ_TM = 512
PROJ_TN = 1024
ATTN_TQ = 1024
ATTN_TK = 512
TAIL_TM = 256
FFN_TM = 512
FFN_TF = 512
ROUTER_TM = 512
DISPATCH_TM = 512
MOE_TM = 512
MOE_TF = 1024
COMBINE_TM = 256
ROW_DMA_UNROLL = 8

F32 = jnp.float32
BF16 = jnp.bfloat16


def _params(*sem):
    return pltpu.CompilerParams(dimension_semantics=sem, vmem_limit_bytes=VMEM_LIMIT)


def _layer_norm(r, g, b):
    mu = jnp.mean(r, axis=-1, keepdims=True)
    c = r - mu
    var = jnp.mean(c * c, axis=-1, keepdims=True)
    return c * lax.rsqrt(var + LN_EPS) * g + b


PROJ_HALF = 2 * D_MODEL
assert 3 * ATTN_WIDTH + POOL_WIDTH == PROJ_HALF


def _proj_mix_kernel(x_ref, w_ref, qkv_ref, u_ref):
    x = x_ref[...].astype(BF16)
    for j in range(PROJ_HALF // PROJ_TN):
        cols = slice(j * PROJ_TN, (j + 1) * PROJ_TN)
        acc = jnp.dot(x, w_ref[:, cols], preferred_element_type=F32)
        if cols.stop <= ATTN_WIDTH:
            qkv_ref[:, cols] = (acc * Q_SCALE).astype(BF16)
        elif cols.stop <= 3 * ATTN_WIDTH:
            qkv_ref[:, cols] = acc.astype(BF16)
        else:
            u_ref[:, j * PROJ_TN - 3 * ATTN_WIDTH:(j + 1) * PROJ_TN - 3 * ATTN_WIDTH] = acc


def _proj_gates_kernel(x_ref, w_ref, g_ref):
    x = x_ref[...].astype(BF16)
    for j in range(PROJ_HALF // PROJ_TN):
        cols = slice(j * PROJ_TN, (j + 1) * PROJ_TN)
        g_ref[:, cols] = jax.nn.sigmoid(jnp.dot(x, w_ref[:, cols], preferred_element_type=F32))


def _in_proj(x, w):
    t, k = x.shape
    tm = PROJ_TM
    assert ATTN_WIDTH % PROJ_TN == 0 and POOL_WIDTH % PROJ_TN == 0
    x_spec = pl.BlockSpec((tm, k), lambda i: (i, 0))

    def w_spec(half):
        return pl.BlockSpec((k, PROJ_HALF), lambda i: (0, half), pipeline_mode=pl.Buffered(1))

    qkv, u = pl.pallas_call(
        _proj_mix_kernel,
        grid=(t // tm,),
        in_specs=[x_spec, w_spec(0)],
        out_specs=[
            pl.BlockSpec((tm, 3 * ATTN_WIDTH), lambda i: (i, 0)),
            pl.BlockSpec((tm, POOL_WIDTH), lambda i: (i, 0)),
        ],
        out_shape=[
            jax.ShapeDtypeStruct((t, 3 * ATTN_WIDTH), BF16),
            jax.ShapeDtypeStruct((t, POOL_WIDTH), F32),
        ],
        compiler_params=_params("parallel"),
        name="in_proj_mix",
    )(x, w)
    gates = pl.pallas_call(
        _proj_gates_kernel,
        grid=(t // tm,),
        in_specs=[x_spec, w_spec(1)],
        out_specs=pl.BlockSpec((tm, PROJ_HALF), lambda i: (i, 0)),
        out_shape=jax.ShapeDtypeStruct((t, PROJ_HALF), F32),
        compiler_params=_params("parallel"),
        name="in_proj_gates",
    )(x, w)
    return qkv, u, gates


def _bias_kernel(rb_ref, o_ref):
    h = pl.program_id(0)
    tk, tq = o_ref.shape[2], o_ref.shape[3]
    kpos = lax.broadcasted_iota(jnp.int32, (tk, tq), 0)
    qpos = lax.broadcasted_iota(jnp.int32, (tk, tq), 1)
    max_exact = N_BUCKETS // 2
    far = rb_ref[h * N_BUCKETS + N_BUCKETS - 1]
    for blk in range(2):
        dist = qpos - kpos + blk * tk
        n = jnp.maximum(dist, 0)
        large = max_exact + (
            jnp.log(jnp.maximum(n, 1).astype(F32) / max_exact) / math.log(MAX_DISTANCE / max_exact)
            * (N_BUCKETS - max_exact)
        ).astype(jnp.int32)
        large = jnp.minimum(large, N_BUCKETS - 1)
        bucket = jnp.where(n < max_exact, n, large)
        bias = jnp.zeros((tk, tq), F32)
        for b in range(N_BUCKETS):
            bias = jnp.where(bucket == b, rb_ref[h * N_BUCKETS + b], bias)
        bias = (bias - far) * LOG2E
        if blk == 0:
            bias = jnp.where(dist >= 0, bias, MASK_VALUE)
        o_ref[0, blk] = bias


def _bias_tiles(rel_bias):
    assert ATTN_TK >= MAX_DISTANCE
    rb = jnp.transpose(rel_bias).reshape(-1)
    return pl.pallas_call(
        _bias_kernel,
        grid_spec=pltpu.PrefetchScalarGridSpec(
            num_scalar_prefetch=1,
            grid=(N_HEADS,),
            in_specs=[],
            out_specs=pl.BlockSpec((1, 2, ATTN_TK, ATTN_TK), lambda h, rb: (h, 0, 0, 0)),
        ),
        out_shape=jax.ShapeDtypeStruct((N_HEADS, 2, ATTN_TK, ATTN_TK), F32),
        compiler_params=_params("arbitrary"),
        name="bias_tiles",
    )(rb)


_Chunk = collections.namedtuple("_Chunk", ["c", "cols", "sub", "first_half", "bias_cols"])


def _attn_kernel(q_ref, k_ref, v_ref, bias_ref, lam_ref, sub_ref, o_ref, qt_ref, vt_ref, m_ref, acc_ref,
                 s0_ref, pp_ref, pa_ref, *, lambda_init):
    qi = pl.program_id(2)
    tq = q_ref.shape[1]
    n_kb, _, tk = vt_ref.shape

    @pl.when(qi == 0)
    def _():
        row = lax.broadcasted_iota(jnp.int32, (ONES_ROWS, tk), 0)
        ones_rows = jnp.where(row == 0, 1.0, 0.0).astype(BF16)
        for kb in range(n_kb):
            vt_ref[kb, 0:HEAD_WIDTH, :] = v_ref[0, kb * tk:(kb + 1) * tk, :].astype(F32).T.astype(BF16)
            vt_ref[kb, HEAD_WIDTH:HEAD_WIDTH + ONES_ROWS, :] = ones_rows

    qt = q_ref[0].astype(F32).T
    dim = lax.broadcasted_iota(jnp.int32, (HEAD_WIDTH, tq), 0)
    qt_ref[0] = jnp.where(dim < HEAD_DIM, qt, 0.0).astype(BF16)
    qt_ref[1] = jnp.where(dim >= HEAD_DIM, qt, 0.0).astype(BF16)

    m_ref[...] = jnp.full(m_ref.shape, MASK_VALUE, F32)
    acc_ref[...] = jnp.zeros(acc_ref.shape, F32)

    half = tk // 2
    chunks = [
        _Chunk(c, slice(j * half, (j + 1) * half), j // 2, j % 2 == 0, slice(j % 2 * half, (j % 2 + 1) * half))
        for c in range(2) for j in range(2 * tq // tk)
    ]
    far = ("far", "far")

    def active(kinds):
        return [ch for ch in chunks if kinds[ch.sub] is not None]

    def keys(kb):
        return k_ref[0, pl.ds(pl.multiple_of(kb * tk, tk), tk), :]

    def key_rows(ch, kinds):
        return tk // 2 if kinds[ch.sub] == "diag" and ch.first_half else tk

    def scores(kk, ch, kinds):
        return jnp.dot(kk[:key_rows(ch, kinds)], qt_ref[ch.c, :, ch.cols], preferred_element_type=F32)

    def add_bias(s, ch, kinds):
        kind = kinds[ch.sub]
        if kind == "diag":
            return s + bias_ref[0, 0, 0:s.shape[0], ch.bias_cols]
        if kind == "near" and ch.first_half:
            r0 = tk - MAX_DISTANCE
            return jnp.concatenate([s[:r0], s[r0:] + bias_ref[0, 1, r0:tk, ch.bias_cols]], axis=0)
        return s

    def softmax_update(s, ch, kinds):
        s = add_bias(s, ch, kinds)
        m_prev = m_ref[ch.c, :, ch.cols]
        m_new = jnp.maximum(m_prev, jnp.max(s, axis=0, keepdims=True))
        alpha = jnp.exp2(m_prev - m_new)
        p = jnp.exp2(s - m_new)
        m_ref[ch.c, :, ch.cols] = m_new
        return p.astype(BF16), alpha

    def accumulate(vt, p, alpha, ch):
        pv = jnp.dot(vt[:, :p.shape[0]], p, preferred_element_type=F32)
        acc_ref[ch.c, :, ch.cols] = alpha * acc_ref[ch.c, :, ch.cols] + pv

    def accumulate_pending(kb):
        accumulate(vt_ref[kb], pp_ref[...], pa_ref[...], chunks[-1])

    def process(kb, kinds, next_kinds):
        act = active(kinds)
        assert act[-1] is chunks[-1]
        kk = keys(kb)
        vt = vt_ref[kb]
        s_next = scores(kk, act[1], kinds)
        accumulate_pending(jnp.maximum(kb - 1, 0))
        p, alpha = softmax_update(s0_ref[0:key_rows(act[0], kinds), :], act[0], kinds)
        for i in range(1, len(act)):
            s_cur = s_next
            if i + 1 < len(act):
                s_next = scores(kk, act[i + 1], kinds)
            elif next_kinds is not None:
                s0_ref[...] = scores(keys(kb + 1), active(next_kinds)[0], far)
            accumulate(vt, p, alpha, act[i - 1])
            p, alpha = softmax_update(s_cur, act[i], kinds)
        pp_ref[...], pa_ref[...] = p, alpha

    s0_ref[...] = scores(keys(0), chunks[0], far)
    pp_ref[...] = jnp.zeros(pp_ref.shape, BF16)
    pa_ref[...] = jnp.ones(pa_ref.shape, F32)

    base = 2 * qi
    n_far = jnp.maximum(base - 1, 0)

    def far_pair(j, carry):
        process(2 * j, far, far)
        process(2 * j + 1, far, far)
        return carry

    lax.fori_loop(0, n_far // 2, far_pair, 0)

    @pl.when(n_far % 2 == 1)
    def _():
        process(n_far - 1, far, far)

    def last_blocks():
        process(base, ("diag", "near"), (None, "diag"))
        process(base + 1, (None, "diag"), None)

    @pl.when(qi >= 1)
    def _():
        process(base - 1, ("near", "far"), ("diag", "near"))
        last_blocks()

    @pl.when(qi == 0)
    def _():
        last_blocks()

    accumulate_pending(base + 1)

    lv = lam_ref[...]
    lam = (
        jnp.exp(jnp.sum(lv[0:1] * lv[1:2], axis=-1, keepdims=True))
        - jnp.exp(jnp.sum(lv[2:3] * lv[3:4], axis=-1, keepdims=True))
        + lambda_init
    )
    num = [acc_ref[c, 0:HEAD_WIDTH, :] for c in range(2)]
    den = [acc_ref[c, HEAD_WIDTH:HEAD_WIDTH + 1, :] for c in range(2)]
    ot = num[0] / den[0] - lam * (num[1] / den[1])
    ot = ot * lax.rsqrt(jnp.mean(ot * ot, axis=0, keepdims=True) + LN_EPS)
    o = ot.T * (sub_ref[...] * (1.0 - lambda_init))
    o_ref[0] = o.astype(o_ref.dtype)


def _attention(qkv, bias_tiles, lam_vec, subln_w, lambda_init):
    b, s, _ = qkv.shape
    tq, tk = ATTN_TQ, ATTN_TK
    assert tq == 2 * tk and s % tq == 0
    assert tk // 2 >= MAX_DISTANCE and (tk - MAX_DISTANCE) % SUBLANES == 0
    kernel = functools.partial(_attn_kernel, lambda_init=lambda_init)
    return pl.pallas_call(
        kernel,
        grid=(b, N_HEADS, s // tq),
        in_specs=[
            pl.BlockSpec((1, tq, HEAD_WIDTH), lambda bi, h, qi: (bi, qi, h)),
            pl.BlockSpec((1, s, HEAD_WIDTH), lambda bi, h, qi: (bi, 0, N_HEADS + h)),
            pl.BlockSpec((1, s, HEAD_WIDTH), lambda bi, h, qi: (bi, 0, 2 * N_HEADS + h)),
            pl.BlockSpec((1, 2, tk, tk), lambda bi, h, qi: (h, 0, 0, 0)),
            pl.BlockSpec((4, HEAD_DIM), lambda bi, h, qi: (0, 0)),
            pl.BlockSpec((1, HEAD_WIDTH), lambda bi, h, qi: (0, 0)),
        ],
        out_specs=pl.BlockSpec((1, tq, HEAD_WIDTH), lambda bi, h, qi: (bi, qi, h)),
        out_shape=jax.ShapeDtypeStruct((b, s, ATTN_WIDTH), BF16),
        scratch_shapes=[
            pltpu.VMEM((2, HEAD_WIDTH, tq), BF16),
            pltpu.VMEM((s // tk, HEAD_WIDTH + ONES_ROWS, tk), BF16),
            pltpu.VMEM((2, 1, tq), F32),
            pltpu.VMEM((2, HEAD_WIDTH + ONES_ROWS, tq), F32),
            pltpu.VMEM((tk, tk // 2), F32),
            pltpu.VMEM((tk, tk // 2), BF16),
            pltpu.VMEM((1, tk // 2), F32),
        ],
        compiler_params=_params("parallel", "parallel", "arbitrary"),
        name="diff_attention",
    )(qkv, qkv, qkv, bias_tiles, lam_vec, subln_w.reshape(1, HEAD_WIDTH))


def _tail_kernel(attn_ref, u_ref, halo_ref, ga_ref, gp_ref, x_ref, pw_ref, ps_ref, wba_ref, wbp_ref, wout_ref,
                 g_ref, b_ref, of_ref, ob_ref, sums_ref, *, tiles_per_seq):
    i = pl.program_id(0)
    tm = u_ref.shape[0]
    tile_in_seq = i % tiles_per_seq
    r0 = SUBLANES + MAX_WINDOW
    n = r0 + tm
    sums_ref[:, 0:SUBLANES, :] = jnp.zeros((sums_ref.shape[0], SUBLANES, POOL_WIDTH), F32)
    sums_ref[0, SUBLANES:r0, :] = jnp.where(tile_in_seq == 0, 0.0, halo_ref[...])
    sums_ref[0, r0:n, :] = u_ref[...]
    t = tile_in_seq * tm + lax.broadcasted_iota(jnp.int32, (tm, 1), 0)

    def doubled(k, lo, cs):
        shift = 2 ** k
        return sums_ref[k, lo:n, cs] + sums_ref[k, lo - shift:n - shift, cs]

    n_groups = len(POOL_WINDOWS)
    a_cols = D_MODEL // n_groups
    attn = attn_ref[...]

    def attn_branch_chunk(j):
        return jnp.dot(attn, wba_ref[:, j * a_cols:(j + 1) * a_cols], preferred_element_type=F32)

    a_chunks = [attn_branch_chunk(0)]
    for k in range(n_groups - 1):
        cs = slice((k + 1) * POOL_GROUP, POOL_WIDTH)
        sums_ref[k + 1, SUBLANES:n, cs] = doubled(k, SUBLANES, cs)

    pooled = []
    for g, w in enumerate(POOL_WINDOWS):
        if g > 0:
            a_chunks.append(attn_branch_chunk(g))
        cs = slice(g * POOL_GROUP, (g + 1) * POOL_GROUP)
        cur = sums_ref[0, r0:n, cs]
        win_sum = doubled(g, r0, cs)
        cnt = jnp.minimum(t + 1, w).astype(F32)
        z = (win_sum / cnt - cur).astype(BF16)
        y = jnp.dot(z, pw_ref[g], preferred_element_type=F32)
        pooled.append((y * ps_ref[:, cs]).astype(BF16))
    pool = jnp.concatenate(pooled, axis=-1)

    a = jnp.concatenate(a_chunks, axis=-1)
    p = jnp.dot(pool, wbp_ref[...], preferred_element_type=F32)
    merged = (ga_ref[...] * a + gp_ref[...] * p).astype(BF16)

    hm = tm // 2
    top, bottom = slice(0, hm), slice(hm, tm)
    o_cols = D_MODEL // 4

    def bottom_chunk(j):
        return jnp.dot(merged[bottom], wout_ref[:, j * o_cols:(j + 1) * o_cols], preferred_element_type=F32)

    def store(rows, out):
        of_ref[rows, :] = out
        ob_ref[rows, :] = out.astype(BF16)

    r_top = ALPHA * x_ref[top, :] + jnp.dot(merged[top], wout_ref[...], preferred_element_type=F32)
    chunks = [bottom_chunk(0)]
    centered = r_top - jnp.mean(r_top, axis=-1, keepdims=True)
    chunks.append(bottom_chunk(1))
    inv_std = lax.rsqrt(jnp.mean(centered * centered, axis=-1, keepdims=True) + LN_EPS)
    chunks.append(bottom_chunk(2))
    store(top, centered * inv_std * g_ref[...] + b_ref[...])
    chunks.append(bottom_chunk(3))
    r_bottom = ALPHA * x_ref[bottom, :] + jnp.concatenate(chunks, axis=-1)
    store(bottom, _layer_norm(r_bottom, g_ref[...], b_ref[...]))


def _resident(shape):
    zeros = (0,) * len(shape)
    return pl.BlockSpec(shape, lambda i: zeros, pipeline_mode=pl.Buffered(1))


def _mixer_tail(attn, u, gates, x, pool_w, pool_scale, w_ba, w_bp, w_out, ln_g, ln_b, seq):
    t = x.shape[0]
    tm = TAIL_TM
    assert seq % tm == 0 and tm % MAX_WINDOW == 0
    assert POOL_WINDOWS == tuple(2 ** (g + 1) for g in range(len(POOL_WINDOWS))) and MAX_WINDOW >= SUBLANES
    halo_blocks = tm // MAX_WINDOW
    kernel = functools.partial(_tail_kernel, tiles_per_seq=seq // tm)
    return pl.pallas_call(
        kernel,
        grid=(t // tm,),
        in_specs=[
            pl.BlockSpec((tm, ATTN_WIDTH), lambda i: (i, 0)),
            pl.BlockSpec((tm, POOL_WIDTH), lambda i: (i, 0)),
            pl.BlockSpec((MAX_WINDOW, POOL_WIDTH), lambda i: (jnp.maximum(i * halo_blocks - 1, 0), 0)),
            pl.BlockSpec((tm, D_MODEL), lambda i: (i, 0)),
            pl.BlockSpec((tm, D_MODEL), lambda i: (i, 1)),
            pl.BlockSpec((tm, D_MODEL), lambda i: (i, 0)),
            _resident(pool_w.shape),
            _resident((1, POOL_WIDTH)),
            _resident(w_ba.shape),
            _resident(w_bp.shape),
            _resident(w_out.shape),
            _resident((1, D_MODEL)),
            _resident((1, D_MODEL)),
        ],
        out_specs=[
            pl.BlockSpec((tm, D_MODEL), lambda i: (i, 0)),
            pl.BlockSpec((tm, D_MODEL), lambda i: (i, 0)),
        ],
        out_shape=[
            jax.ShapeDtypeStruct((t, D_MODEL), F32),
            jax.ShapeDtypeStruct((t, D_MODEL), BF16),
        ],
        scratch_shapes=[pltpu.VMEM((len(POOL_WINDOWS), SUBLANES + MAX_WINDOW + tm, POOL_WIDTH), F32)],
        compiler_params=_params("parallel"),
        name="mixer_tail",
    )(attn, u, u, gates, gates, x, pool_w, pool_scale.reshape(1, POOL_WIDTH), w_ba, w_bp, w_out,
      ln_g.reshape(1, D_MODEL), ln_b.reshape(1, D_MODEL))


def _swiglu(hb, wg, wu):
    g = jnp.dot(hb, wg, preferred_element_type=F32)
    u = jnp.dot(hb, wu, preferred_element_type=F32)
    return (g * jax.nn.sigmoid(g) * u).astype(BF16)


def _dense_ffn_kernel(hb_ref, hf_ref, wg_ref, wu_ref, wd_ref, g_ref, b_ref, of_ref, ob_ref, acc_ref):
    f = pl.program_id(1)

    @pl.when(f == 0)
    def _():
        acc_ref[...] = jnp.zeros(acc_ref.shape, F32)

    a = _swiglu(hb_ref[...], wg_ref[...], wu_ref[...])
    acc_ref[...] += jnp.dot(a, wd_ref[...], preferred_element_type=F32)

    @pl.when(f == pl.num_programs(1) - 1)
    def _():
        out = _layer_norm(ALPHA * hf_ref[...] + acc_ref[...], g_ref[...], b_ref[...])
        of_ref[...] = out
        ob_ref[...] = out.astype(BF16)


def _dense_ffn(hb, hf, wg, wu, wd, ln_g, ln_b):
    t = hb.shape[0]
    d_ff = wg.shape[1]
    tm, tf = FFN_TM, FFN_TF
    assert d_ff % tf == 0
    return pl.pallas_call(
        _dense_ffn_kernel,
        grid=(t // tm, d_ff // tf),
        in_specs=[
            pl.BlockSpec((tm, D_MODEL), lambda i, f: (i, 0)),
            pl.BlockSpec((tm, D_MODEL), lambda i, f: (i, 0)),
            pl.BlockSpec((D_MODEL, tf), lambda i, f: (0, f)),
            pl.BlockSpec((D_MODEL, tf), lambda i, f: (0, f)),
            pl.BlockSpec((tf, D_MODEL), lambda i, f: (f, 0)),
            pl.BlockSpec((1, D_MODEL), lambda i, f: (0, 0)),
            pl.BlockSpec((1, D_MODEL), lambda i, f: (0, 0)),
        ],
        out_specs=[
            pl.BlockSpec((tm, D_MODEL), lambda i, f: (i, 0)),
            pl.BlockSpec((tm, D_MODEL), lambda i, f: (i, 0)),
        ],
        out_shape=[
            jax.ShapeDtypeStruct((t, D_MODEL), F32),
            jax.ShapeDtypeStruct((t, D_MODEL), BF16),
        ],
        scratch_shapes=[pltpu.VMEM((tm, D_MODEL), F32)],
        compiler_params=_params("parallel", "arbitrary"),
        name="dense_ffn",
    )(hb, hf, wg, wu, wd, ln_g.reshape(1, D_MODEL), ln_b.reshape(1, D_MODEL))


ROUTE_W0, ROUTE_W1, ROUTE_E0, ROUTE_E1, ROUTE_R0, ROUTE_R1 = range(6)


def _router_kernel(h_ref, rw_ref, route_ref, counts_ref, carry_ref):
    i = pl.program_id(0)
    tm = h_ref.shape[0]

    @pl.when(i == 0)
    def _():
        carry_ref[...] = jnp.zeros(carry_ref.shape, F32)

    logits = jnp.dot(h_ref[...], rw_ref[...], preferred_element_type=F32)
    lane = lax.broadcasted_iota(jnp.int32, (tm, LANES), 1)
    lg = jnp.where(lane < N_EXPERTS, logits, -jnp.inf)
    m1 = jnp.max(lg, axis=-1, keepdims=True)
    e1 = jnp.min(jnp.where(lg == m1, lane, LANES), axis=-1, keepdims=True)
    lg2 = jnp.where(lane == e1, -jnp.inf, lg)
    m2 = jnp.max(lg2, axis=-1, keepdims=True)
    e2 = jnp.min(jnp.where(lg2 == m2, lane, LANES), axis=-1, keepdims=True)
    x2 = jnp.exp(m2 - m1)
    w1 = 1.0 / (1.0 + x2)
    w2 = x2 / (1.0 + x2)

    sel1 = lane == e1
    sel2 = lane == e2
    mask = jnp.where(sel1 | sel2, 1.0, 0.0)
    r = lax.broadcasted_iota(jnp.int32, (tm, tm), 0)
    c = lax.broadcasted_iota(jnp.int32, (tm, tm), 1)
    tri = jnp.where(r >= c, 1.0, 0.0).astype(BF16)
    incl = jnp.dot(tri, mask.astype(BF16), preferred_element_type=F32)
    rank = incl - mask + carry_ref[...]
    total = carry_ref[...] + jnp.sum(mask, axis=0, keepdims=True)
    carry_ref[...] = total
    counts_ref[...] = total

    r1 = jnp.sum(jnp.where(sel1, rank, 0.0), axis=-1, keepdims=True)
    r2 = jnp.sum(jnp.where(sel2, rank, 0.0), axis=-1, keepdims=True)
    cols = (w1, w2, e1.astype(F32), e2.astype(F32), r1, r2)
    route = jnp.zeros((tm, LANES), F32)
    for idx, val in enumerate(cols):
        route = jnp.where(lane == idx, val, route)
    route_ref[...] = route


def _router(hb, router_w):
    t = hb.shape[0]
    tm = ROUTER_TM
    rw = jnp.pad(router_w, ((0, 0), (0, LANES - N_EXPERTS))).astype(BF16)
    return pl.pallas_call(
        _router_kernel,
        grid=(t // tm,),
        in_specs=[
            pl.BlockSpec((tm, D_MODEL), lambda i: (i, 0)),
            pl.BlockSpec((D_MODEL, LANES), lambda i: (0, 0)),
        ],
        out_specs=[
            pl.BlockSpec((tm, LANES), lambda i: (i, 0)),
            pl.BlockSpec((1, LANES), lambda i: (0, 0)),
        ],
        out_shape=[
            jax.ShapeDtypeStruct((t, LANES), F32),
            jax.ShapeDtypeStruct((1, LANES), F32),
        ],
        scratch_shapes=[pltpu.VMEM((1, LANES), F32)],
        compiler_params=_params("arbitrary"),
        name="moe_router",
    )(hb, rw)


def _dispatch_kernel(pad_start_ref, pad_count_ref, n_used_ref, dest_ref, h_ref, xs_hbm, zero_ref, sem, pad_sem):
    n = h_ref.shape[0]
    tile_rows = zero_ref.shape[0]
    n_tiles = xs_hbm.shape[0] // tile_rows

    @pl.when(pl.program_id(0) == 0)
    def _():
        zero_ref[...] = jnp.zeros(zero_ref.shape, F32)
        for e in range(N_EXPERTS):
            def pad_copy(j, e=e):
                return pltpu.make_async_copy(
                    zero_ref.at[pl.ds(0, 1)], xs_hbm.at[pl.ds(pad_start_ref[e] + j, 1)], pad_sem)

            lax.fori_loop(0, pad_count_ref[e], lambda j, c: (pad_copy(j).start(), c)[1], 0)
            lax.fori_loop(0, pad_count_ref[e], lambda j, c: (pad_copy(j).wait(), c)[1], 0)

        def tile_copy(j):
            return pltpu.make_async_copy(
                zero_ref, xs_hbm.at[pl.ds(pl.multiple_of(j * tile_rows, tile_rows), tile_rows)], pad_sem)

        lax.fori_loop(n_used_ref[0], n_tiles, lambda j, c: (tile_copy(j).start(), c)[1], 0)
        lax.fori_loop(n_used_ref[0], n_tiles, lambda j, c: (tile_copy(j).wait(), c)[1], 0)

    def row_copy(r, k):
        return pltpu.make_async_copy(
            h_ref.at[pl.ds(r, 1)], xs_hbm.at[pl.ds(dest_ref[0, 0, TOP_K * r + k], 1)], sem)

    def start(r, carry):
        for k in range(TOP_K):
            row_copy(r, k).start()
        return carry

    def wait(r, carry):
        for k in range(TOP_K):
            row_copy(r, k).wait()
        return carry

    lax.fori_loop(0, n, start, 0, unroll=ROW_DMA_UNROLL)
    lax.fori_loop(0, n, wait, 0, unroll=ROW_DMA_UNROLL)


def _dispatch(hf, dest, pad_start, pad_count, n_used, n_tiles):
    t = hf.shape[0]
    tm = DISPATCH_TM
    return pl.pallas_call(
        _dispatch_kernel,
        grid_spec=pltpu.PrefetchScalarGridSpec(
            num_scalar_prefetch=3,
            grid=(t // tm,),
            in_specs=[
                pl.BlockSpec((1, 1, TOP_K * tm), lambda i, ps, pc, nu: (i, 0, 0), memory_space=pltpu.SMEM),
                pl.BlockSpec((tm, D_MODEL), lambda i, ps, pc, nu: (i, 0)),
            ],
            out_specs=pl.BlockSpec(memory_space=pl.ANY),
            scratch_shapes=[
                pltpu.VMEM((MOE_TM, D_MODEL), F32),
                pltpu.SemaphoreType.DMA(()),
                pltpu.SemaphoreType.DMA(()),
            ],
        ),
        out_shape=jax.ShapeDtypeStruct((n_tiles * MOE_TM, D_MODEL), F32),
        compiler_params=_params("arbitrary"),
        name="moe_dispatch",
    )(pad_start, pad_count, n_used, dest.reshape(t // tm, 1, TOP_K * tm), hf)


def _moe_ffn_kernel(te_ref, ok_ref, src_ref, xs_ref, wg_ref, wu_ref, wd_ref, ys_ref, xb_ref, acc_ref):
    del te_ref, src_ref
    i = pl.program_id(0)
    f = pl.program_id(1)
    ok = ok_ref[i] == 1

    @pl.when(ok & (f == 0))
    def _():
        xb_ref[...] = xs_ref[...].astype(BF16)
        acc_ref[...] = jnp.zeros(acc_ref.shape, F32)

    @pl.when(ok)
    def _():
        a = _swiglu(xb_ref[...], wg_ref[...], wu_ref[...])
        acc_ref[...] += jnp.dot(a, wd_ref[...], preferred_element_type=F32)

    @pl.when(f == pl.num_programs(1) - 1)
    def _():
        ys_ref[...] = jnp.where(ok, acc_ref[...], 0.0)


def _moe_ffn(xs, tile_expert, tile_ok, tile_src, wg, wu, wd):
    n_rows = xs.shape[0]
    d_ff = wg.shape[2]
    tm, tf = MOE_TM, MOE_TF
    assert d_ff % tf == 0
    n_f = d_ff // tf
    n_tiles = n_rows // tm

    def f_idx(i, f, ok):
        return jnp.where(ok[i] == 1, f, n_f - 1)

    return pl.pallas_call(
        _moe_ffn_kernel,
        grid_spec=pltpu.PrefetchScalarGridSpec(
            num_scalar_prefetch=3,
            grid=(n_tiles, n_f),
            in_specs=[
                pl.BlockSpec((tm, D_MODEL), lambda i, f, te, ok, src: (src[i], 0)),
                pl.BlockSpec((None, D_MODEL, tf), lambda i, f, te, ok, src: (te[i], 0, f_idx(i, f, ok))),
                pl.BlockSpec((None, D_MODEL, tf), lambda i, f, te, ok, src: (te[i], 0, f_idx(i, f, ok))),
                pl.BlockSpec((None, tf, D_MODEL), lambda i, f, te, ok, src: (te[i], f_idx(i, f, ok), 0)),
            ],
            out_specs=pl.BlockSpec((tm, D_MODEL), lambda i, f, te, ok, src: (i, 0)),
            scratch_shapes=[pltpu.VMEM((tm, D_MODEL), BF16), pltpu.VMEM((tm, D_MODEL), F32)],
        ),
        out_shape=jax.ShapeDtypeStruct((n_rows, D_MODEL), F32),
        compiler_params=_params("arbitrary", "arbitrary"),
        name="moe_ffn",
    )(tile_expert, tile_ok, tile_src, xs, wg, wu, wd)


def _combine_kernel(dest_ref, h_ref, route_ref, ys_hbm, g_ref, b_ref, o_ref, y_ref, sem):
    tm = h_ref.shape[0]

    def row_copy(r, k):
        return pltpu.make_async_copy(
            ys_hbm.at[pl.ds(dest_ref[0, 0, TOP_K * r + k], 1)], y_ref.at[k, pl.ds(r, 1)], sem)

    def start(r, carry):
        for k in range(TOP_K):
            row_copy(r, k).start()
        return carry

    def wait(r, carry):
        for k in range(TOP_K):
            row_copy(r, k).wait()
        return carry

    lax.fori_loop(0, tm, start, 0, unroll=ROW_DMA_UNROLL)
    lax.fori_loop(0, tm, wait, 0, unroll=ROW_DMA_UNROLL)

    route = route_ref[...]
    w1 = route[:, ROUTE_W0:ROUTE_W0 + 1]
    w2 = route[:, ROUTE_W1:ROUTE_W1 + 1]
    y = w1 * y_ref[0] + w2 * y_ref[1]
    o_ref[...] = _layer_norm(ALPHA * h_ref[...] + y, g_ref[...], b_ref[...])


def _combine(hf, route, dest, ys, ln_g, ln_b):
    t = hf.shape[0]
    tm = COMBINE_TM
    return pl.pallas_call(
        _combine_kernel,
        grid=(t // tm,),
        in_specs=[
            pl.BlockSpec((1, 1, TOP_K * tm), lambda i: (i, 0, 0), memory_space=pltpu.SMEM),
            pl.BlockSpec((tm, D_MODEL), lambda i: (i, 0)),
            pl.BlockSpec((tm, LANES), lambda i: (i, 0)),
            pl.BlockSpec(memory_space=pl.ANY),
            pl.BlockSpec((1, D_MODEL), lambda i: (0, 0)),
            pl.BlockSpec((1, D_MODEL), lambda i: (0, 0)),
        ],
        out_specs=pl.BlockSpec((tm, D_MODEL), lambda i: (i, 0)),
        out_shape=jax.ShapeDtypeStruct((t, D_MODEL), F32),
        scratch_shapes=[pltpu.VMEM((TOP_K, tm, D_MODEL), F32), pltpu.SemaphoreType.DMA(())],
        compiler_params=_params("arbitrary"),
        name="moe_combine",
    )(dest.reshape(t // tm, 1, TOP_K * tm), hf, route, ys, ln_g.reshape(1, D_MODEL), ln_b.reshape(1, D_MODEL))


def _moe(hf, hb, router_w, wg, wu, wd, ln_g, ln_b):
    t = hf.shape[0]
    tm = MOE_TM
    n_tiles = (TOP_K * t) // tm + N_EXPERTS
    route, counts = _router(hb, router_w)

    counts = counts[0, :N_EXPERTS].astype(jnp.int32)
    tiles_per_expert = (counts + tm - 1) // tm
    tile_end = jnp.cumsum(tiles_per_expert)
    group_start = (tile_end - tiles_per_expert) * tm
    tile_ids = jnp.arange(n_tiles, dtype=jnp.int32)
    n_used = tile_end[-1]
    last_used = jnp.maximum(n_used - 1, 0)
    expert_of_tile = jnp.sum(tile_ids[:, None] >= tile_end[None, :], axis=1).astype(jnp.int32)
    tile_ok = (tile_ids < n_used).astype(jnp.int32)
    tile_expert = jnp.where(tile_ok == 1, expert_of_tile, expert_of_tile[last_used])
    tile_expert = jnp.minimum(tile_expert, N_EXPERTS - 1)
    tile_src = jnp.where(tile_ok == 1, tile_ids, last_used)

    experts = route[:, ROUTE_E0:ROUTE_E1 + 1].astype(jnp.int32)
    ranks = route[:, ROUTE_R0:ROUTE_R1 + 1].astype(jnp.int32)
    dest = (group_start[experts] + ranks).reshape(-1)

    xs = _dispatch(hf, dest, group_start + counts, tiles_per_expert * tm - counts, n_used.reshape(1), n_tiles)
    ys = _moe_ffn(xs, tile_expert, tile_ok, tile_src, wg, wu, wd)
    return _combine(hf, route, dest, ys, ln_g, ln_b)


def kernel(x, w_in, lambdas, subln_w, pool_w, pool_scale, w_branch_attn, w_branch_pool, w_out, rel_bias,
           ln1_g, ln1_b, dense_w_gate, dense_w_up, dense_w_down, router_w, moe_w_gate, moe_w_up, moe_w_down,
           ln2_g, ln2_b):
    b, s, d = x.shape
    t = b * s
    xf = x.reshape(t, d)
    xin = xf
    bias_tiles = _bias_tiles(rel_bias)
    for l in range(DEPTH):
        lambda_init = 0.8 - 0.6 * math.exp(-0.3 * l)
        qkv, u, gates = _in_proj(xin, w_in[l].astype(BF16))
        attn = _attention(qkv.reshape(b, s, 3 * ATTN_WIDTH), bias_tiles, lambdas[l], subln_w[l], lambda_init)
        hf, hb = _mixer_tail(attn.reshape(t, ATTN_WIDTH), u, gates, xf, pool_w[l].astype(BF16), pool_scale[l],
                             w_branch_attn[l].astype(BF16), w_branch_pool[l].astype(BF16), w_out[l].astype(BF16),
                             ln1_g[l], ln1_b[l], s)
        if l % 2 == 0:
            xf, xin = _dense_ffn(hb, hf, dense_w_gate[l // 2].astype(BF16), dense_w_up[l // 2].astype(BF16),
                                 dense_w_down[l // 2].astype(BF16), ln2_g[l], ln2_b[l])
        else:
            xf = _moe(hf, hb, router_w[l // 2], moe_w_gate[l // 2].astype(BF16), moe_w_up[l // 2].astype(BF16),
                      moe_w_down[l // 2].astype(BF16), ln2_g[l], ln2_b[l])
            xin = xf
    return xf.reshape(b, s, d)
```

```python
import collections
import functools
import math

import jax
import jax.numpy as jnp
from jax import lax
from jax.experimental import pallas as pl
from jax.experimental.pallas import tpu as pltpu

D_MODEL = 2048
DEPTH = 2
N_HEADS = 8
HEAD_DIM = 64
HEAD_WIDTH = 2 * HEAD_DIM
ATTN_WIDTH = N_HEADS * HEAD_WIDTH
POOL_WINDOWS = (2, 4, 8, 16)
POOL_GROUP = 256
POOL_WIDTH = POOL_GROUP * len(POOL_WINDOWS)
MAX_WINDOW = max(POOL_WINDOWS)
N_BUCKETS = 32
MAX_DISTANCE = 128
N_EXPERTS = 8
TOP_K = 2
ALPHA = (2.0 * DEPTH) ** 0.25
LN_EPS = 1e-5
LOG2E = math.log2(math.e)
Q_SCALE = HEAD_DIM ** -0.5 * LOG2E
ONES_ROWS = 16

LANES = 128
SUBLANES = 8
MASK_VALUE = -1e30
VMEM_LIMIT = 56 * 1024 * 1024

PROJ_TM = 512
PROJ_TN = 1024
ATTN_TQ = 2048
ATTN_TK = 512
TAIL_TM = 256
FFN_TM = 512
FFN_TF = 512
ROUTER_TM = 512
DISPATCH_TM = 512
MOE_TM = 512
MOE_TF = 1024
COMBINE_TM = 256
ROW_DMA_UNROLL = 8

F32 = jnp.float32
BF16 = jnp.bfloat16


def _params(*sem):
    return pltpu.CompilerParams(dimension_semantics=sem, vmem_limit_bytes=VMEM_LIMIT)


def _layer_norm(r, g, b):
    mu = jnp.mean(r, axis=-1, keepdims=True)
    c = r - mu
    var = jnp.mean(c * c, axis=-1, keepdims=True)
    return c * lax.rsqrt(var + LN_EPS) * g + b


PROJ_HALF = 2 * D_MODEL
assert 3 * ATTN_WIDTH + POOL_WIDTH == PROJ_HALF


def _proj_mix_kernel(x_ref, w_ref, qkv_ref, u_ref):
    x = x_ref[...].astype(BF16)
    for j in range(PROJ_HALF // PROJ_TN):
        cols = slice(j * PROJ_TN, (j + 1) * PROJ_TN)
        acc = jnp.dot(x, w_ref[:, cols], preferred_element_type=F32)
        if cols.stop <= ATTN_WIDTH:
            qkv_ref[:, cols] = (acc * Q_SCALE).astype(BF16)
        elif cols.stop <= 3 * ATTN_WIDTH:
            qkv_ref[:, cols] = acc.astype(BF16)
        else:
            u_ref[:, j * PROJ_TN - 3 * ATTN_WIDTH:(j + 1) * PROJ_TN - 3 * ATTN_WIDTH] = acc


def _proj_gates_kernel(x_ref, w_ref, g_ref):
    x = x_ref[...].astype(BF16)
    for j in range(PROJ_HALF // PROJ_TN):
        cols = slice(j * PROJ_TN, (j + 1) * PROJ_TN)
        g_ref[:, cols] = jax.nn.sigmoid(jnp.dot(x, w_ref[:, cols], preferred_element_type=F32))


def _in_proj(x, w):
    t, k = x.shape
    tm = PROJ_TM
    assert ATTN_WIDTH % PROJ_TN == 0 and POOL_WIDTH % PROJ_TN == 0
    x_spec = pl.BlockSpec((tm, k), lambda i: (i, 0))

    def w_spec(half):
        return pl.BlockSpec((k, PROJ_HALF), lambda i: (0, half), pipeline_mode=pl.Buffered(1))

    qkv, u = pl.pallas_call(
        _proj_mix_kernel,
        grid=(t // tm,),
        in_specs=[x_spec, w_spec(0)],
        out_specs=[
            pl.BlockSpec((tm, 3 * ATTN_WIDTH), lambda i: (i, 0)),
            pl.BlockSpec((tm, POOL_WIDTH), lambda i: (i, 0)),
        ],
        out_shape=[
            jax.ShapeDtypeStruct((t, 3 * ATTN_WIDTH), BF16),
            jax.ShapeDtypeStruct((t, POOL_WIDTH), F32),
        ],
        compiler_params=_params("parallel"),
        name="in_proj_mix",
    )(x, w)
    gates = pl.pallas_call(
        _proj_gates_kernel,
        grid=(t // tm,),
        in_specs=[x_spec, w_spec(1)],
        out_specs=pl.BlockSpec((tm, PROJ_HALF), lambda i: (i, 0)),
        out_shape=jax.ShapeDtypeStruct((t, PROJ_HALF), F32),
        compiler_params=_params("parallel"),
        name="in_proj_gates",
    )(x, w)
    return qkv, u, gates


def _bias_kernel(rb_ref, o_ref):
    h = pl.program_id(0)
    tk, tq = o_ref.shape[2], o_ref.shape[3]
    kpos = lax.broadcasted_iota(jnp.int32, (tk, tq), 0)
    qpos = lax.broadcasted_iota(jnp.int32, (tk, tq), 1)
    max_exact = N_BUCKETS // 2
    far = rb_ref[h * N_BUCKETS + N_BUCKETS - 1]
    for blk in range(2):
        dist = qpos - kpos + blk * tk
        n = jnp.maximum(dist, 0)
        large = max_exact + (
            jnp.log(jnp.maximum(n, 1).astype(F32) / max_exact) / math.log(MAX_DISTANCE / max_exact)
            * (N_BUCKETS - max_exact)
        ).astype(jnp.int32)
        large = jnp.minimum(large, N_BUCKETS - 1)
        bucket = jnp.where(n < max_exact, n, large)
        bias = jnp.zeros((tk, tq), F32)
        for b in range(N_BUCKETS):
            bias = jnp.where(bucket == b, rb_ref[h * N_BUCKETS + b], bias)
        bias = (bias - far) * LOG2E
        if blk == 0:
            bias = jnp.where(dist >= 0, bias, MASK_VALUE)
        o_ref[0, blk] = bias


def _bias_tiles(rel_bias):
    assert ATTN_TK >= MAX_DISTANCE
    rb = jnp.transpose(rel_bias).reshape(-1)
    return pl.pallas_call(
        _bias_kernel,
        grid_spec=pltpu.PrefetchScalarGridSpec(
            num_scalar_prefetch=1,
            grid=(N_HEADS,),
            in_specs=[],
            out_specs=pl.BlockSpec((1, 2, ATTN_TK, ATTN_TK), lambda h, rb: (h, 0, 0, 0)),
        ),
        out_shape=jax.ShapeDtypeStruct((N_HEADS, 2, ATTN_TK, ATTN_TK), F32),
        compiler_params=_params("arbitrary"),
        name="bias_tiles",
    )(rb)


_Chunk = collections.namedtuple("_Chunk", ["c", "cols", "sub", "first_half", "bias_cols"])


def _attn_kernel(q_ref, k_ref, v_ref, bias_ref, lam_ref, sub_ref, o_ref, qt_ref, vt_ref, m_ref, acc_ref,
                 s0_ref, pp_ref, pa_ref, *, lambda_init):
    qi = pl.program_id(2)
    tq = q_ref.shape[1]
    n_kb, _, tk = vt_ref.shape

    @pl.when(qi == 0)
    def _():
        row = lax.broadcasted_iota(jnp.int32, (ONES_ROWS, tk), 0)
        ones_rows = jnp.where(row == 0, 1.0, 0.0).astype(BF16)
        for kb in range(n_kb):
            vt_ref[kb, 0:HEAD_WIDTH, :] = v_ref[0, kb * tk:(kb + 1) * tk, :].astype(F32).T.astype(BF16)
            vt_ref[kb, HEAD_WIDTH:HEAD_WIDTH + ONES_ROWS, :] = ones_rows

    qt = q_ref[0].astype(F32).T
    dim = lax.broadcasted_iota(jnp.int32, (HEAD_WIDTH, tq), 0)
    qt_ref[0] = jnp.where(dim < HEAD_DIM, qt, 0.0).astype(BF16)
    qt_ref[1] = jnp.where(dim >= HEAD_DIM, qt, 0.0).astype(BF16)

    m_ref[...] = jnp.full(m_ref.shape, MASK_VALUE, F32)
    acc_ref[...] = jnp.zeros(acc_ref.shape, F32)

    n_sub = tq // tk
    half = tk // 2
    chunks = [
        _Chunk(c, slice(j * half, (j + 1) * half), j // 2, j % 2 == 0, slice(j % 2 * half, (j % 2 + 1) * half))
        for c in range(2) for j in range(2 * n_sub)
    ]
    far = ("far",) * n_sub

    def active(kinds):
        return [ch for ch in chunks if kinds[ch.sub] is not None]

    def keys(kb):
        return k_ref[0, pl.ds(pl.multiple_of(kb * tk, tk), tk), :]

    def key_rows(ch, kinds):
        return tk // 2 if kinds[ch.sub] == "diag" and ch.first_half else tk

    def scores(kk, ch, kinds):
        return jnp.dot(kk[:key_rows(ch, kinds)], qt_ref[ch.c, :, ch.cols], preferred_element_type=F32)

    def add_bias(s, ch, kinds):
        kind = kinds[ch.sub]
        if kind == "diag":
            return s + bias_ref[0, 0, 0:s.shape[0], ch.bias_cols]
        if kind == "near" and ch.first_half:
            r0 = tk - MAX_DISTANCE
            return jnp.concatenate([s[:r0], s[r0:] + bias_ref[0, 1, r0:tk, ch.bias_cols]], axis=0)
        return s

    def softmax_update(s, ch, kinds):
        s = add_bias(s, ch, kinds)
        m_prev = m_ref[ch.c, :, ch.cols]
        m_new = jnp.maximum(m_prev, jnp.max(s, axis=0, keepdims=True))
        alpha = jnp.exp2(m_prev - m_new)
        p = jnp.exp2(s - m_new)
        m_ref[ch.c, :, ch.cols] = m_new
        return p.astype(BF16), alpha

    def accumulate(vt, p, alpha, ch):
        pv = jnp.dot(vt[:, :p.shape[0]], p, preferred_element_type=F32)
        acc_ref[ch.c, :, ch.cols] = alpha * acc_ref[ch.c, :, ch.cols] + pv

    def accumulate_pending(kb):
        accumulate(vt_ref[kb], pp_ref[...], pa_ref[...], chunks[-1])

    def process(kb, kinds, next_kinds):
        act = active(kinds)
        assert act[-1] is chunks[-1]
        kk = keys(kb)
        vt = vt_ref[kb]
        s_next = scores(kk, act[1], kinds)
        accumulate_pending(jnp.maximum(kb - 1, 0))
        p, alpha = softmax_update(s0_ref[0:key_rows(act[0], kinds), :], act[0], kinds)
        for i in range(1, len(act)):
            s_cur = s_next
            if i + 1 < len(act):
                s_next = scores(kk, act[i + 1], kinds)
            elif next_kinds is not None:
                s0_ref[...] = scores(keys(kb + 1), active(next_kinds)[0], far)
            accumulate(vt, p, alpha, act[i - 1])
            p, alpha = softmax_update(s_cur, act[i], kinds)
        pp_ref[...], pa_ref[...] = p, alpha

    s0_ref[...] = scores(keys(0), chunks[0], far)
    pp_ref[...] = jnp.zeros(pp_ref.shape, BF16)
    pa_ref[...] = jnp.ones(pa_ref.shape, F32)

    base = n_sub * qi
    n_far = jnp.maximum(base - 1, 0)

    def far_pair(j, carry):
        process(2 * j, far, far)
        process(2 * j + 1, far, far)
        return carry

    lax.fori_loop(0, n_far // 2, far_pair, 0)

    @pl.when(n_far % 2 == 1)
    def _():
        process(n_far - 1, far, far)

    def kinds_at(r):
        return tuple("far" if r < j - 1 else "near" if r == j - 1 else "diag" if r == j else None
                     for j in range(n_sub))

    def last_blocks(first):
        for r in range(first, n_sub):
            process(base + r, kinds_at(r), kinds_at(r + 1) if r + 1 < n_sub else None)

    @pl.when(qi >= 1)
    def _():
        last_blocks(-1)

    @pl.when(qi == 0)
    def _():
        last_blocks(0)

    accumulate_pending(base + n_sub - 1)

    lv = lam_ref[...]
    lam = (
        jnp.exp(jnp.sum(lv[0:1] * lv[1:2], axis=-1, keepdims=True))
        - jnp.exp(jnp.sum(lv[2:3] * lv[3:4], axis=-1, keepdims=True))
        + lambda_init
    )
    num = [acc_ref[c, 0:HEAD_WIDTH, :] for c in range(2)]
    den = [acc_ref[c, HEAD_WIDTH:HEAD_WIDTH + 1, :] for c in range(2)]
    ot = num[0] / den[0] - lam * (num[1] / den[1])
    ot = ot * lax.rsqrt(jnp.mean(ot * ot, axis=0, keepdims=True) + LN_EPS)
    o = ot.T * (sub_ref[...] * (1.0 - lambda_init))
    o_ref[0] = o.astype(o_ref.dtype)


def _attention(qkv, bias_tiles, lam_vec, subln_w, lambda_init):
    b, s, _ = qkv.shape
    tq, tk = ATTN_TQ, ATTN_TK
    assert tq % tk == 0 and s % tq == 0
    assert tk // 2 >= MAX_DISTANCE and (tk - MAX_DISTANCE) % SUBLANES == 0
    kernel = functools.partial(_attn_kernel, lambda_init=lambda_init)
    return pl.pallas_call(
        kernel,
        grid=(b, N_HEADS, s // tq),
        in_specs=[
            pl.BlockSpec((1, tq, HEAD_WIDTH), lambda bi, h, qi: (bi, qi, h)),
            pl.BlockSpec((1, s, HEAD_WIDTH), lambda bi, h, qi: (bi, 0, N_HEADS + h)),
            pl.BlockSpec((1, s, HEAD_WIDTH), lambda bi, h, qi: (bi, 0, 2 * N_HEADS + h)),
            pl.BlockSpec((1, 2, tk, tk), lambda bi, h, qi: (h, 0, 0, 0)),
            pl.BlockSpec((4, HEAD_DIM), lambda bi, h, qi: (0, 0)),
            pl.BlockSpec((1, HEAD_WIDTH), lambda bi, h, qi: (0, 0)),
        ],
        out_specs=pl.BlockSpec((1, tq, HEAD_WIDTH), lambda bi, h, qi: (bi, qi, h)),
        out_shape=jax.ShapeDtypeStruct((b, s, ATTN_WIDTH), BF16),
        scratch_shapes=[
            pltpu.VMEM((2, HEAD_WIDTH, tq), BF16),
            pltpu.VMEM((s // tk, HEAD_WIDTH + ONES_ROWS, tk), BF16),
            pltpu.VMEM((2, 1, tq), F32),
            pltpu.VMEM((2, HEAD_WIDTH + ONES_ROWS, tq), F32),
            pltpu.VMEM((tk, tk // 2), F32),
            pltpu.VMEM((tk, tk // 2), BF16),
            pltpu.VMEM((1, tk // 2), F32),
        ],
        compiler_params=_params("parallel", "parallel", "arbitrary"),
        name="diff_attention",
    )(qkv, qkv, qkv, bias_tiles, lam_vec, subln_w.reshape(1, HEAD_WIDTH))


def _tail_kernel(attn_ref, u_ref, halo_ref, ga_ref, gp_ref, x_ref, pw_ref, ps_ref, wba_ref, wbp_ref, wout_ref,
                 g_ref, b_ref, of_ref, ob_ref, sums_ref, *, tiles_per_seq):
    i = pl.program_id(0)
    tm = u_ref.shape[0]
    tile_in_seq = i % tiles_per_seq
    r0 = SUBLANES + MAX_WINDOW
    n = r0 + tm
    sums_ref[:, 0:SUBLANES, :] = jnp.zeros((sums_ref.shape[0], SUBLANES, POOL_WIDTH), F32)
    sums_ref[0, SUBLANES:r0, :] = jnp.where(tile_in_seq == 0, 0.0, halo_ref[...])
    sums_ref[0, r0:n, :] = u_ref[...]
    t = tile_in_seq * tm + lax.broadcasted_iota(jnp.int32, (tm, 1), 0)

    def doubled(k, lo, cs):
        shift = 2 ** k
        return sums_ref[k, lo:n, cs] + sums_ref[k, lo - shift:n - shift, cs]

    n_groups = len(POOL_WINDOWS)
    a_cols = D_MODEL // n_groups
    attn = attn_ref[...]

    def attn_branch_chunk(j):
        return jnp.dot(attn, wba_ref[:, j * a_cols:(j + 1) * a_cols], preferred_element_type=F32)

    a_chunks = [attn_branch_chunk(0)]
    for k in range(n_groups - 1):
        cs = slice((k + 1) * POOL_GROUP, POOL_WIDTH)
        sums_ref[k + 1, SUBLANES:n, cs] = doubled(k, SUBLANES, cs)

    pooled = []
    for g, w in enumerate(POOL_WINDOWS):
        if g > 0:
            a_chunks.append(attn_branch_chunk(g))
        cs = slice(g * POOL_GROUP, (g + 1) * POOL_GROUP)
        cur = sums_ref[0, r0:n, cs]
        win_sum = doubled(g, r0, cs)
        cnt = jnp.minimum(t + 1, w).astype(F32)
        z = (win_sum / cnt - cur).astype(BF16)
        y = jnp.dot(z, pw_ref[g], preferred_element_type=F32)
        pooled.append((y * ps_ref[:, cs]).astype(BF16))
    pool = jnp.concatenate(pooled, axis=-1)

    a = jnp.concatenate(a_chunks, axis=-1)
    p = jnp.dot(pool, wbp_ref[...], preferred_element_type=F32)
    merged = (ga_ref[...] * a + gp_ref[...] * p).astype(BF16)

    hm = tm // 2
    top, bottom = slice(0, hm), slice(hm, tm)
    o_cols = D_MODEL // 4

    def bottom_chunk(j):
        return jnp.dot(merged[bottom], wout_ref[:, j * o_cols:(j + 1) * o_cols], preferred_element_type=F32)

    def store(rows, out):
        of_ref[rows, :] = out
        ob_ref[rows, :] = out.astype(BF16)

    r_top = ALPHA * x_ref[top, :] + jnp.dot(merged[top], wout_ref[...], preferred_element_type=F32)
    chunks = [bottom_chunk(0)]
    centered = r_top - jnp.mean(r_top, axis=-1, keepdims=True)
    chunks.append(bottom_chunk(1))
    inv_std = lax.rsqrt(jnp.mean(centered * centered, axis=-1, keepdims=True) + LN_EPS)
    chunks.append(bottom_chunk(2))
    store(top, centered * inv_std * g_ref[...] + b_ref[...])
    chunks.append(bottom_chunk(3))
    r_bottom = ALPHA * x_ref[bottom, :] + jnp.concatenate(chunks, axis=-1)
    store(bottom, _layer_norm(r_bottom, g_ref[...], b_ref[...]))


def _resident(shape):
    zeros = (0,) * len(shape)
    return pl.BlockSpec(shape, lambda i: zeros, pipeline_mode=pl.Buffered(1))


def _mixer_tail(attn, u, gates, x, pool_w, pool_scale, w_ba, w_bp, w_out, ln_g, ln_b, seq):
    t = x.shape[0]
    tm = TAIL_TM
    assert seq % tm == 0 and tm % MAX_WINDOW == 0
    assert POOL_WINDOWS == tuple(2 ** (g + 1) for g in range(len(POOL_WINDOWS))) and MAX_WINDOW >= SUBLANES
    halo_blocks = tm // MAX_WINDOW
    kernel = functools.partial(_tail_kernel, tiles_per_seq=seq // tm)
    return pl.pallas_call(
        kernel,
        grid=(t // tm,),
        in_specs=[
            pl.BlockSpec((tm, ATTN_WIDTH), lambda i: (i, 0)),
            pl.BlockSpec((tm, POOL_WIDTH), lambda i: (i, 0)),
            pl.BlockSpec((MAX_WINDOW, POOL_WIDTH), lambda i: (jnp.maximum(i * halo_blocks - 1, 0), 0)),
            pl.BlockSpec((tm, D_MODEL), lambda i: (i, 0)),
            pl.BlockSpec((tm, D_MODEL), lambda i: (i, 1)),
            pl.BlockSpec((tm, D_MODEL), lambda i: (i, 0)),
            _resident(pool_w.shape),
            _resident((1, POOL_WIDTH)),
            _resident(w_ba.shape),
            _resident(w_bp.shape),
            _resident(w_out.shape),
            _resident((1, D_MODEL)),
            _resident((1, D_MODEL)),
        ],
        out_specs=[
            pl.BlockSpec((tm, D_MODEL), lambda i: (i, 0)),
            pl.BlockSpec((tm, D_MODEL), lambda i: (i, 0)),
        ],
        out_shape=[
            jax.ShapeDtypeStruct((t, D_MODEL), F32),
            jax.ShapeDtypeStruct((t, D_MODEL), BF16),
        ],
        scratch_shapes=[pltpu.VMEM((len(POOL_WINDOWS), SUBLANES + MAX_WINDOW + tm, POOL_WIDTH), F32)],
        compiler_params=_params("parallel"),
        name="mixer_tail",
    )(attn, u, u, gates, gates, x, pool_w, pool_scale.reshape(1, POOL_WIDTH), w_ba, w_bp, w_out,
      ln_g.reshape(1, D_MODEL), ln_b.reshape(1, D_MODEL))


def _swiglu(hb, wg, wu):
    g = jnp.dot(hb, wg, preferred_element_type=F32)
    u = jnp.dot(hb, wu, preferred_element_type=F32)
    return (g * jax.nn.sigmoid(g) * u).astype(BF16)


def _dense_ffn_kernel(hb_ref, hf_ref, wg_ref, wu_ref, wd_ref, g_ref, b_ref, of_ref, ob_ref, acc_ref):
    f = pl.program_id(1)

    @pl.when(f == 0)
    def _():
        acc_ref[...] = jnp.zeros(acc_ref.shape, F32)

    a = _swiglu(hb_ref[...], wg_ref[...], wu_ref[...])
    acc_ref[...] += jnp.dot(a, wd_ref[...], preferred_element_type=F32)

    @pl.when(f == pl.num_programs(1) - 1)
    def _():
        out = _layer_norm(ALPHA * hf_ref[...] + acc_ref[...], g_ref[...], b_ref[...])
        of_ref[...] = out
        ob_ref[...] = out.astype(BF16)


def _dense_ffn(hb, hf, wg, wu, wd, ln_g, ln_b):
    t = hb.shape[0]
    d_ff = wg.shape[1]
    tm, tf = FFN_TM, FFN_TF
    assert d_ff % tf == 0
    return pl.pallas_call(
        _dense_ffn_kernel,
        grid=(t // tm, d_ff // tf),
        in_specs=[
            pl.BlockSpec((tm, D_MODEL), lambda i, f: (i, 0)),
            pl.BlockSpec((tm, D_MODEL), lambda i, f: (i, 0)),
            pl.BlockSpec((D_MODEL, tf), lambda i, f: (0, f)),
            pl.BlockSpec((D_MODEL, tf), lambda i, f: (0, f)),
            pl.BlockSpec((tf, D_MODEL), lambda i, f: (f, 0)),
            pl.BlockSpec((1, D_MODEL), lambda i, f: (0, 0)),
            pl.BlockSpec((1, D_MODEL), lambda i, f: (0, 0)),
        ],
        out_specs=[
            pl.BlockSpec((tm, D_MODEL), lambda i, f: (i, 0)),
            pl.BlockSpec((tm, D_MODEL), lambda i, f: (i, 0)),
        ],
        out_shape=[
            jax.ShapeDtypeStruct((t, D_MODEL), F32),
            jax.ShapeDtypeStruct((t, D_MODEL), BF16),
        ],
        scratch_shapes=[pltpu.VMEM((tm, D_MODEL), F32)],
        compiler_params=_params("parallel", "arbitrary"),
        name="dense_ffn",
    )(hb, hf, wg, wu, wd, ln_g.reshape(1, D_MODEL), ln_b.reshape(1, D_MODEL))


ROUTE_W0, ROUTE_W1, ROUTE_E0, ROUTE_E1, ROUTE_R0, ROUTE_R1 = range(6)


def _router_kernel(h_ref, rw_ref, route_ref, counts_ref, carry_ref):
    i = pl.program_id(0)
    tm = h_ref.shape[0]

    @pl.when(i == 0)
    def _():
        carry_ref[...] = jnp.zeros(carry_ref.shape, F32)

    logits = jnp.dot(h_ref[...], rw_ref[...], preferred_element_type=F32)
    lane = lax.broadcasted_iota(jnp.int32, (tm, LANES), 1)
    lg = jnp.where(lane < N_EXPERTS, logits, -jnp.inf)
    m1 = jnp.max(lg, axis=-1, keepdims=True)
    e1 = jnp.min(jnp.where(lg == m1, lane, LANES), axis=-1, keepdims=True)
    lg2 = jnp.where(lane == e1, -jnp.inf, lg)
    m2 = jnp.max(lg2, axis=-1, keepdims=True)
    e2 = jnp.min(jnp.where(lg2 == m2, lane, LANES), axis=-1, keepdims=True)
    x2 = jnp.exp(m2 - m1)
    w1 = 1.0 / (1.0 + x2)
    w2 = x2 / (1.0 + x2)

    sel1 = lane == e1
    sel2 = lane == e2
    mask = jnp.where(sel1 | sel2, 1.0, 0.0)
    r = lax.broadcasted_iota(jnp.int32, (tm, tm), 0)
    c = lax.broadcasted_iota(jnp.int32, (tm, tm), 1)
    tri = jnp.where(r >= c, 1.0, 0.0).astype(BF16)
    incl = jnp.dot(tri, mask.astype(BF16), preferred_element_type=F32)
    rank = incl - mask + carry_ref[...]
    total = carry_ref[...] + jnp.sum(mask, axis=0, keepdims=True)
    carry_ref[...] = total
    counts_ref[...] = total

    r1 = jnp.sum(jnp.where(sel1, rank, 0.0), axis=-1, keepdims=True)
    r2 = jnp.sum(jnp.where(sel2, rank, 0.0), axis=-1, keepdims=True)
    cols = (w1, w2, e1.astype(F32), e2.astype(F32), r1, r2)
    route = jnp.zeros((tm, LANES), F32)
    for idx, val in enumerate(cols):
        route = jnp.where(lane == idx, val, route)
    route_ref[...] = route


def _router(hb, router_w):
    t = hb.shape[0]
    tm = ROUTER_TM
    rw = jnp.pad(router_w, ((0, 0), (0, LANES - N_EXPERTS))).astype(BF16)
    return pl.pallas_call(
        _router_kernel,
        grid=(t // tm,),
        in_specs=[
            pl.BlockSpec((tm, D_MODEL), lambda i: (i, 0)),
            pl.BlockSpec((D_MODEL, LANES), lambda i: (0, 0)),
        ],
        out_specs=[
            pl.BlockSpec((tm, LANES), lambda i: (i, 0)),
            pl.BlockSpec((1, LANES), lambda i: (0, 0)),
        ],
        out_shape=[
            jax.ShapeDtypeStruct((t, LANES), F32),
            jax.ShapeDtypeStruct((1, LANES), F32),
        ],
        scratch_shapes=[pltpu.VMEM((1, LANES), F32)],
        compiler_params=_params("arbitrary"),
        name="moe_router",
    )(hb, rw)


def _dispatch_kernel(pad_start_ref, pad_count_ref, n_used_ref, dest_ref, h_ref, xs_hbm, zero_ref, sem, pad_sem):
    n = h_ref.shape[0]
    tile_rows = zero_ref.shape[0]
    n_tiles = xs_hbm.shape[0] // tile_rows

    @pl.when(pl.program_id(0) == 0)
    def _():
        zero_ref[...] = jnp.zeros(zero_ref.shape, F32)
        for e in range(N_EXPERTS):
            def pad_copy(j, e=e):
                return pltpu.make_async_copy(
                    zero_ref.at[pl.ds(0, 1)], xs_hbm.at[pl.ds(pad_start_ref[e] + j, 1)], pad_sem)

            lax.fori_loop(0, pad_count_ref[e], lambda j, c: (pad_copy(j).start(), c)[1], 0)
            lax.fori_loop(0, pad_count_ref[e], lambda j, c: (pad_copy(j).wait(), c)[1], 0)

        def tile_copy(j):
            return pltpu.make_async_copy(
                zero_ref, xs_hbm.at[pl.ds(pl.multiple_of(j * tile_rows, tile_rows), tile_rows)], pad_sem)

        lax.fori_loop(n_used_ref[0], n_tiles, lambda j, c: (tile_copy(j).start(), c)[1], 0)
        lax.fori_loop(n_used_ref[0], n_tiles, lambda j, c: (tile_copy(j).wait(), c)[1], 0)

    def row_copy(r, k):
        return pltpu.make_async_copy(
            h_ref.at[pl.ds(r, 1)], xs_hbm.at[pl.ds(dest_ref[0, 0, TOP_K * r + k], 1)], sem)

    def start(r, carry):
        for k in range(TOP_K):
            row_copy(r, k).start()
        return carry

    def wait(r, carry):
        for k in range(TOP_K):
            row_copy(r, k).wait()
        return carry

    lax.fori_loop(0, n, start, 0, unroll=ROW_DMA_UNROLL)
    lax.fori_loop(0, n, wait, 0, unroll=ROW_DMA_UNROLL)


def _dispatch(hf, dest, pad_start, pad_count, n_used, n_tiles):
    t = hf.shape[0]
    tm = DISPATCH_TM
    return pl.pallas_call(
        _dispatch_kernel,
        grid_spec=pltpu.PrefetchScalarGridSpec(
            num_scalar_prefetch=3,
            grid=(t // tm,),
            in_specs=[
                pl.BlockSpec((1, 1, TOP_K * tm), lambda i, ps, pc, nu: (i, 0, 0), memory_space=pltpu.SMEM),
                pl.BlockSpec((tm, D_MODEL), lambda i, ps, pc, nu: (i, 0)),
            ],
            out_specs=pl.BlockSpec(memory_space=pl.ANY),
            scratch_shapes=[
                pltpu.VMEM((MOE_TM, D_MODEL), F32),
                pltpu.SemaphoreType.DMA(()),
                pltpu.SemaphoreType.DMA(()),
            ],
        ),
        out_shape=jax.ShapeDtypeStruct((n_tiles * MOE_TM, D_MODEL), F32),
        compiler_params=_params("arbitrary"),
        name="moe_dispatch",
    )(pad_start, pad_count, n_used, dest.reshape(t // tm, 1, TOP_K * tm), hf)


def _moe_ffn_kernel(te_ref, ok_ref, src_ref, xs_ref, wg_ref, wu_ref, wd_ref, ys_ref, xb_ref, acc_ref):
    del te_ref, src_ref
    i = pl.program_id(0)
    f = pl.program_id(1)
    ok = ok_ref[i] == 1

    @pl.when(ok & (f == 0))
    def _():
        xb_ref[...] = xs_ref[...].astype(BF16)
        acc_ref[...] = jnp.zeros(acc_ref.shape, F32)

    @pl.when(ok)
    def _():
        a = _swiglu(xb_ref[...], wg_ref[...], wu_ref[...])
        acc_ref[...] += jnp.dot(a, wd_ref[...], preferred_element_type=F32)

    @pl.when(f == pl.num_programs(1) - 1)
    def _():
        ys_ref[...] = jnp.where(ok, acc_ref[...], 0.0)


def _moe_ffn(xs, tile_expert, tile_ok, tile_src, wg, wu, wd):
    n_rows = xs.shape[0]
    d_ff = wg.shape[2]
    tm, tf = MOE_TM, MOE_TF
    assert d_ff % tf == 0
    n_f = d_ff // tf
    n_tiles = n_rows // tm

    def f_idx(i, f, ok):
        return jnp.where(ok[i] == 1, f, n_f - 1)

    return pl.pallas_call(
        _moe_ffn_kernel,
        grid_spec=pltpu.PrefetchScalarGridSpec(
            num_scalar_prefetch=3,
            grid=(n_tiles, n_f),
            in_specs=[
                pl.BlockSpec((tm, D_MODEL), lambda i, f, te, ok, src: (src[i], 0)),
                pl.BlockSpec((None, D_MODEL, tf), lambda i, f, te, ok, src: (te[i], 0, f_idx(i, f, ok))),
                pl.BlockSpec((None, D_MODEL, tf), lambda i, f, te, ok, src: (te[i], 0, f_idx(i, f, ok))),
                pl.BlockSpec((None, tf, D_MODEL), lambda i, f, te, ok, src: (te[i], f_idx(i, f, ok), 0)),
            ],
            out_specs=pl.BlockSpec((tm, D_MODEL), lambda i, f, te, ok, src: (i, 0)),
            scratch_shapes=[pltpu.VMEM((tm, D_MODEL), BF16), pltpu.VMEM((tm, D_MODEL), F32)],
        ),
        out_shape=jax.ShapeDtypeStruct((n_rows, D_MODEL), F32),
        compiler_params=_params("arbitrary", "arbitrary"),
        name="moe_ffn",
    )(tile_expert, tile_ok, tile_src, xs, wg, wu, wd)


def _combine_kernel(dest_ref, h_ref, route_ref, ys_hbm, g_ref, b_ref, o_ref, y_ref, sem):
    tm = h_ref.shape[0]

    def row_copy(r, k):
        return pltpu.make_async_copy(
            ys_hbm.at[pl.ds(dest_ref[0, 0, TOP_K * r + k], 1)], y_ref.at[k, pl.ds(r, 1)], sem)

    def start(r, carry):
        for k in range(TOP_K):
            row_copy(r, k).start()
        return carry

    def wait(r, carry):
        for k in range(TOP_K):
            row_copy(r, k).wait()
        return carry

    lax.fori_loop(0, tm, start, 0, unroll=ROW_DMA_UNROLL)
    lax.fori_loop(0, tm, wait, 0, unroll=ROW_DMA_UNROLL)

    route = route_ref[...]
    w1 = route[:, ROUTE_W0:ROUTE_W0 + 1]
    w2 = route[:, ROUTE_W1:ROUTE_W1 + 1]
    y = w1 * y_ref[0] + w2 * y_ref[1]
    o_ref[...] = _layer_norm(ALPHA * h_ref[...] + y, g_ref[...], b_ref[...])


def _combine(hf, route, dest, ys, ln_g, ln_b):
    t = hf.shape[0]
    tm = COMBINE_TM
    return pl.pallas_call(
        _combine_kernel,
        grid=(t // tm,),
        in_specs=[
            pl.BlockSpec((1, 1, TOP_K * tm), lambda i: (i, 0, 0), memory_space=pltpu.SMEM),
            pl.BlockSpec((tm, D_MODEL), lambda i: (i, 0)),
            pl.BlockSpec((tm, LANES), lambda i: (i, 0)),
            pl.BlockSpec(memory_space=pl.ANY),
            pl.BlockSpec((1, D_MODEL), lambda i: (0, 0)),
            pl.BlockSpec((1, D_MODEL), lambda i: (0, 0)),
        ],
        out_specs=pl.BlockSpec((tm, D_MODEL), lambda i: (i, 0)),
        out_shape=jax.ShapeDtypeStruct((t, D_MODEL), F32),
        scratch_shapes=[pltpu.VMEM((TOP_K, tm, D_MODEL), F32), pltpu.SemaphoreType.DMA(())],
        compiler_params=_params("arbitrary"),
        name="moe_combine",
    )(dest.reshape(t // tm, 1, TOP_K * tm), hf, route, ys, ln_g.reshape(1, D_MODEL), ln_b.reshape(1, D_MODEL))


def _moe(hf, hb, router_w, wg, wu, wd, ln_g, ln_b):
    t = hf.shape[0]
    tm = MOE_TM
    n_tiles = (TOP_K * t) // tm + N_EXPERTS
    route, counts = _router(hb, router_w)

    counts = counts[0, :N_EXPERTS].astype(jnp.int32)
    tiles_per_expert = (counts + tm - 1) // tm
    tile_end = jnp.cumsum(tiles_per_expert)
    group_start = (tile_end - tiles_per_expert) * tm
    tile_ids = jnp.arange(n_tiles, dtype=jnp.int32)
    n_used = tile_end[-1]
    last_used = jnp.maximum(n_used - 1, 0)
    expert_of_tile = jnp.sum(tile_ids[:, None] >= tile_end[None, :], axis=1).astype(jnp.int32)
    tile_ok = (tile_ids < n_used).astype(jnp.int32)
    tile_expert = jnp.where(tile_ok == 1, expert_of_tile, expert_of_tile[last_used])
    tile_expert = jnp.minimum(tile_expert, N_EXPERTS - 1)
    tile_src = jnp.where(tile_ok == 1, tile_ids, last_used)

    experts = route[:, ROUTE_E0:ROUTE_E1 + 1].astype(jnp.int32)
    ranks = route[:, ROUTE_R0:ROUTE_R1 + 1].astype(jnp.int32)
    dest = (group_start[experts] + ranks).reshape(-1)

    xs = _dispatch(hf, dest, group_start + counts, tiles_per_expert * tm - counts, n_used.reshape(1), n_tiles)
    ys = _moe_ffn(xs, tile_expert, tile_ok, tile_src, wg, wu, wd)
    return _combine(hf, route, dest, ys, ln_g, ln_b)


def kernel(x, w_in, lambdas, subln_w, pool_w, pool_scale, w_branch_attn, w_branch_pool, w_out, rel_bias,
           ln1_g, ln1_b, dense_w_gate, dense_w_up, dense_w_down, router_w, moe_w_gate, moe_w_up, moe_w_down,
           ln2_g, ln2_b):
    b, s, d = x.shape
    t = b * s
    xf = x.reshape(t, d)
    xin = xf
    bias_tiles = _bias_tiles(rel_bias)
    for l in range(DEPTH):
        lambda_init = 0.8 - 0.6 * math.exp(-0.3 * l)
        qkv, u, gates = _in_proj(xin, w_in[l].astype(BF16))
        attn = _attention(qkv.reshape(b, s, 3 * ATTN_WIDTH), bias_tiles, lambdas[l], subln_w[l], lambda_init)
        hf, hb = _mixer_tail(attn.reshape(t, ATTN_WIDTH), u, gates, xf, pool_w[l].astype(BF16), pool_scale[l],
                             w_branch_attn[l].astype(BF16), w_branch_pool[l].astype(BF16), w_out[l].astype(BF16),
                             ln1_g[l], ln1_b[l], s)
        if l % 2 == 0:
            xf, xin = _dense_ffn(hb, hf, dense_w_gate[l // 2].astype(BF16), dense_w_up[l // 2].astype(BF16),
                                 dense_w_down[l // 2].astype(BF16), ln2_g[l], ln2_b[l])
        else:
            xf = _moe(hf, hb, router_w[l // 2], moe_w_gate[l // 2].astype(BF16), moe_w_up[l // 2].astype(BF16),
                      moe_w_down[l // 2].astype(BF16), ln2_g[l], ln2_b[l])
            xin = xf
    return xf.reshape(b, s, d)
```

```python
import collections
import functools
import math

import jax
import jax.numpy as jnp
from jax import lax
from jax.experimental import pallas as pl
from jax.experimental.pallas import tpu as pltpu

D_MODEL = 2048
DEPTH = 2
N_HEADS = 8
HEAD_DIM = 64
HEAD_WIDTH = 2 * HEAD_DIM
ATTN_WIDTH = N_HEADS * HEAD_WIDTH
POOL_WINDOWS = (2, 4, 8, 16)
POOL_GROUP = 256
POOL_WIDTH = POOL_GROUP * len(POOL_WINDOWS)
MAX_WINDOW = max(POOL_WINDOWS)
N_BUCKETS = 32
MAX_DISTANCE = 128
N_EXPERTS = 8
TOP_K = 2
ALPHA = (2.0 * DEPTH) ** 0.25
LN_EPS = 1e-5
LOG2E = math.log2(math.e)
Q_SCALE = HEAD_DIM ** -0.5 * LOG2E
ONES_ROWS = 16

LANES = 128
SUBLANES = 8
MASK_VALUE = -1e30
VMEM_LIMIT = 56 * 1024 * 1024

PROJ_TM = 512
PROJ_TN = 1024
ATTN_TQ = 2048
ATTN_TK = 512
TAIL_TM = 256
FFN_TM = 512
FFN_TF = 512
ROUTER_TM = 512
DISPATCH_TM = 512
MOE_TM = 512
MOE_TF = 1024
COMBINE_TM = 256
ROW_DMA_UNROLL = 8

F32 = jnp.float32
BF16 = jnp.bfloat16


def _params(*sem):
    return pltpu.CompilerParams(dimension_semantics=sem, vmem_limit_bytes=VMEM_LIMIT)


def _layer_norm(r, g, b):
    mu = jnp.mean(r, axis=-1, keepdims=True)
    c = r - mu
    var = jnp.mean(c * c, axis=-1, keepdims=True)
    return c * lax.rsqrt(var + LN_EPS) * g + b


PROJ_HALF = 2 * D_MODEL
assert 3 * ATTN_WIDTH + POOL_WIDTH == PROJ_HALF


def _proj_mix_kernel(x_ref, w_ref, qkv_ref, u_ref):
    x = x_ref[...].astype(BF16)
    for j in range(PROJ_HALF // PROJ_TN):
        cols = slice(j * PROJ_TN, (j + 1) * PROJ_TN)
        acc = jnp.dot(x, w_ref[:, cols], preferred_element_type=F32)
        if cols.stop > 3 * ATTN_WIDTH:
            u_ref[:, j * PROJ_TN - 3 * ATTN_WIDTH:(j + 1) * PROJ_TN - 3 * ATTN_WIDTH] = acc
            continue
        if cols.stop <= ATTN_WIDTH:
            acc = acc * Q_SCALE
        for hh in range(PROJ_TN // HEAD_WIDTH):
            head_cols = slice(hh * HEAD_WIDTH, (hh + 1) * HEAD_WIDTH)
            qkv_ref[cols.start // HEAD_WIDTH + hh] = acc[:, head_cols].astype(BF16)


def _proj_gates_kernel(x_ref, w_ref, g_ref):
    x = x_ref[...].astype(BF16)
    for j in range(PROJ_HALF // PROJ_TN):
        cols = slice(j * PROJ_TN, (j + 1) * PROJ_TN)
        g_ref[:, cols] = jax.nn.sigmoid(jnp.dot(x, w_ref[:, cols], preferred_element_type=F32))


def _in_proj(x, w):
    t, k = x.shape
    tm = PROJ_TM
    assert ATTN_WIDTH % PROJ_TN == 0 and POOL_WIDTH % PROJ_TN == 0
    x_spec = pl.BlockSpec((tm, k), lambda i: (i, 0))

    def w_spec(half):
        return pl.BlockSpec((k, PROJ_HALF), lambda i: (0, half), pipeline_mode=pl.Buffered(1))

    qkv, u = pl.pallas_call(
        _proj_mix_kernel,
        grid=(t // tm,),
        in_specs=[x_spec, w_spec(0)],
        out_specs=[
            pl.BlockSpec((3 * N_HEADS, tm, HEAD_WIDTH), lambda i: (0, i, 0)),
            pl.BlockSpec((tm, POOL_WIDTH), lambda i: (i, 0)),
        ],
        out_shape=[
            jax.ShapeDtypeStruct((3 * N_HEADS, t, HEAD_WIDTH), BF16),
            jax.ShapeDtypeStruct((t, POOL_WIDTH), F32),
        ],
        compiler_params=_params("parallel"),
        name="in_proj_mix",
    )(x, w)
    gates = pl.pallas_call(
        _proj_gates_kernel,
        grid=(t // tm,),
        in_specs=[x_spec, w_spec(1)],
        out_specs=pl.BlockSpec((tm, PROJ_HALF), lambda i: (i, 0)),
        out_shape=jax.ShapeDtypeStruct((t, PROJ_HALF), F32),
        compiler_params=_params("parallel"),
        name="in_proj_gates",
    )(x, w)
    return qkv, u, gates


def _bias_kernel(rb_ref, o_ref):
    h = pl.program_id(0)
    tk, tq = o_ref.shape[2], o_ref.shape[3]
    kpos = lax.broadcasted_iota(jnp.int32, (tk, tq), 0)
    qpos = lax.broadcasted_iota(jnp.int32, (tk, tq), 1)
    max_exact = N_BUCKETS // 2
    far = rb_ref[h * N_BUCKETS + N_BUCKETS - 1]
    for blk in range(2):
        dist = qpos - kpos + blk * tk
        n = jnp.maximum(dist, 0)
        large = max_exact + (
            jnp.log(jnp.maximum(n, 1).astype(F32) / max_exact) / math.log(MAX_DISTANCE / max_exact)
            * (N_BUCKETS - max_exact)
        ).astype(jnp.int32)
        large = jnp.minimum(large, N_BUCKETS - 1)
        bucket = jnp.where(n < max_exact, n, large)
        bias = jnp.zeros((tk, tq), F32)
        for b in range(N_BUCKETS):
            bias = jnp.where(bucket == b, rb_ref[h * N_BUCKETS + b], bias)
        bias = (bias - far) * LOG2E
        if blk == 0:
            bias = jnp.where(dist >= 0, bias, MASK_VALUE)
        o_ref[0, blk] = bias


def _bias_tiles(rel_bias):
    assert ATTN_TK >= MAX_DISTANCE
    rb = jnp.transpose(rel_bias).reshape(-1)
    return pl.pallas_call(
        _bias_kernel,
        grid_spec=pltpu.PrefetchScalarGridSpec(
            num_scalar_prefetch=1,
            grid=(N_HEADS,),
            in_specs=[],
            out_specs=pl.BlockSpec((1, 2, ATTN_TK, ATTN_TK), lambda h, rb: (h, 0, 0, 0)),
        ),
        out_shape=jax.ShapeDtypeStruct((N_HEADS, 2, ATTN_TK, ATTN_TK), F32),
        compiler_params=_params("arbitrary"),
        name="bias_tiles",
    )(rb)


_Chunk = collections.namedtuple("_Chunk", ["c", "cols", "sub", "first_half", "bias_cols"])


def _attn_kernel(q_ref, k_ref, v_ref, bias_ref, lam_ref, sub_ref, o_ref, qt_ref, vt_ref, m_ref, acc_ref,
                 s0_ref, pp_ref, pa_ref, *, lambda_init):
    qi = pl.program_id(2)
    tq = q_ref.shape[1]
    n_kb, _, tk = vt_ref.shape

    @pl.when(qi == 0)
    def _():
        row = lax.broadcasted_iota(jnp.int32, (ONES_ROWS, tk), 0)
        ones_rows = jnp.where(row == 0, 1.0, 0.0).astype(BF16)
        for kb in range(n_kb):
            vt_ref[kb, 0:HEAD_WIDTH, :] = v_ref[0, kb * tk:(kb + 1) * tk, :].astype(F32).T.astype(BF16)
            vt_ref[kb, HEAD_WIDTH:HEAD_WIDTH + ONES_ROWS, :] = ones_rows

    qt = q_ref[0].astype(F32).T
    dim = lax.broadcasted_iota(jnp.int32, (HEAD_WIDTH, tq), 0)
    qt_ref[0] = jnp.where(dim < HEAD_DIM, qt, 0.0).astype(BF16)
    qt_ref[1] = jnp.where(dim >= HEAD_DIM, qt, 0.0).astype(BF16)

    m_ref[...] = jnp.full(m_ref.shape, MASK_VALUE, F32)
    acc_ref[...] = jnp.zeros(acc_ref.shape, F32)

    n_sub = tq // tk
    half = tk // 2
    chunks = [
        _Chunk(c, slice(j * half, (j + 1) * half), j // 2, j % 2 == 0, slice(j % 2 * half, (j % 2 + 1) * half))
        for c in range(2) for j in range(2 * n_sub)
    ]
    far = ("far",) * n_sub

    def active(kinds):
        return [ch for ch in chunks if kinds[ch.sub] is not None]

    def keys(kb):
        return k_ref[0, pl.ds(pl.multiple_of(kb * tk, tk), tk), :]

    def key_rows(ch, kinds):
        return tk // 2 if kinds[ch.sub] == "diag" and ch.first_half else tk

    def scores(kk, ch, kinds):
        return jnp.dot(kk[:key_rows(ch, kinds)], qt_ref[ch.c, :, ch.cols], preferred_element_type=F32)

    def add_bias(s, ch, kinds):
        kind = kinds[ch.sub]
        if kind == "diag":
            return s + bias_ref[0, 0, 0:s.shape[0], ch.bias_cols]
        if kind == "near" and ch.first_half:
            r0 = tk - MAX_DISTANCE
            return jnp.concatenate([s[:r0], s[r0:] + bias_ref[0, 1, r0:tk, ch.bias_cols]], axis=0)
        return s

    def softmax_update(s, ch, kinds):
        s = add_bias(s, ch, kinds)
        m_prev = m_ref[ch.c, :, ch.cols]
        m_new = jnp.maximum(m_prev, jnp.max(s, axis=0, keepdims=True))
        alpha = jnp.exp2(m_prev - m_new)
        p = jnp.exp2(s - m_new)
        m_ref[ch.c, :, ch.cols] = m_new
        return p.astype(BF16), alpha

    def accumulate(vt, p, alpha, ch):
        pv = jnp.dot(vt[:, :p.shape[0]], p, preferred_element_type=F32)
        acc_ref[ch.c, :, ch.cols] = alpha * acc_ref[ch.c, :, ch.cols] + pv

    def accumulate_pending(kb):
        accumulate(vt_ref[kb], pp_ref[...], pa_ref[...], chunks[-1])

    def process(kb, kinds, next_kinds):
        act = active(kinds)
        assert act[-1] is chunks[-1]
        kk = keys(kb)
        vt = vt_ref[kb]
        s_next = scores(kk, act[1], kinds)
        accumulate_pending(jnp.maximum(kb - 1, 0))
        p, alpha = softmax_update(s0_ref[0:key_rows(act[0], kinds), :], act[0], kinds)
        for i in range(1, len(act)):
            s_cur = s_next
            if i + 1 < len(act):
                s_next = scores(kk, act[i + 1], kinds)
            elif next_kinds is not None:
                s0_ref[...] = scores(keys(kb + 1), active(next_kinds)[0], far)
            accumulate(vt, p, alpha, act[i - 1])
            p, alpha = softmax_update(s_cur, act[i], kinds)
        pp_ref[...], pa_ref[...] = p, alpha

    s0_ref[...] = scores(keys(0), chunks[0], far)
    pp_ref[...] = jnp.zeros(pp_ref.shape, BF16)
    pa_ref[...] = jnp.ones(pa_ref.shape, F32)

    base = n_sub * qi
    n_far = jnp.maximum(base - 1, 0)

    def far_pair(j, carry):
        process(2 * j, far, far)
        process(2 * j + 1, far, far)
        return carry

    lax.fori_loop(0, n_far // 2, far_pair, 0)

    @pl.when(n_far % 2 == 1)
    def _():
        process(n_far - 1, far, far)

    def kinds_at(r):
        return tuple("far" if r < j - 1 else "near" if r == j - 1 else "diag" if r == j else None
                     for j in range(n_sub))

    def last_blocks(first):
        for r in range(first, n_sub):
            process(base + r, kinds_at(r), kinds_at(r + 1) if r + 1 < n_sub else None)

    @pl.when(qi >= 1)
    def _():
        last_blocks(-1)

    @pl.when(qi == 0)
    def _():
        last_blocks(0)

    accumulate_pending(base + n_sub - 1)

    lv = lam_ref[...]
    lam = (
        jnp.exp(jnp.sum(lv[0:1] * lv[1:2], axis=-1, keepdims=True))
        - jnp.exp(jnp.sum(lv[2:3] * lv[3:4], axis=-1, keepdims=True))
        + lambda_init
    )
    num = [acc_ref[c, 0:HEAD_WIDTH, :] for c in range(2)]
    den = [acc_ref[c, HEAD_WIDTH:HEAD_WIDTH + 1, :] for c in range(2)]
    ot = num[0] / den[0] - lam * (num[1] / den[1])
    ot = ot * lax.rsqrt(jnp.mean(ot * ot, axis=0, keepdims=True) + LN_EPS)
    o = ot.T * (sub_ref[...] * (1.0 - lambda_init))
    o_ref[0] = o.astype(o_ref.dtype)


def _attention(qkv, bias_tiles, lam_vec, subln_w, lambda_init, b, s):
    tq, tk = ATTN_TQ, ATTN_TK
    n_q = s // tq
    assert tq % tk == 0 and s % tq == 0 and qkv.shape == (3 * N_HEADS, b * s, HEAD_WIDTH)
    assert tk // 2 >= MAX_DISTANCE and (tk - MAX_DISTANCE) % SUBLANES == 0
    kernel = functools.partial(_attn_kernel, lambda_init=lambda_init)
    return pl.pallas_call(
        kernel,
        grid=(b, N_HEADS, n_q),
        in_specs=[
            pl.BlockSpec((1, tq, HEAD_WIDTH), lambda bi, h, qi: (h, bi * n_q + qi, 0)),
            pl.BlockSpec((1, s, HEAD_WIDTH), lambda bi, h, qi: (N_HEADS + h, bi, 0)),
            pl.BlockSpec((1, s, HEAD_WIDTH), lambda bi, h, qi: (2 * N_HEADS + h, bi, 0)),
            pl.BlockSpec((1, 2, tk, tk), lambda bi, h, qi: (h, 0, 0, 0)),
            pl.BlockSpec((4, HEAD_DIM), lambda bi, h, qi: (0, 0)),
            pl.BlockSpec((1, HEAD_WIDTH), lambda bi, h, qi: (0, 0)),
        ],
        out_specs=pl.BlockSpec((1, tq, HEAD_WIDTH), lambda bi, h, qi: (h, bi * n_q + qi, 0)),
        out_shape=jax.ShapeDtypeStruct((N_HEADS, b * s, HEAD_WIDTH), BF16),
        scratch_shapes=[
            pltpu.VMEM((2, HEAD_WIDTH, tq), BF16),
            pltpu.VMEM((s // tk, HEAD_WIDTH + ONES_ROWS, tk), BF16),
            pltpu.VMEM((2, 1, tq), F32),
            pltpu.VMEM((2, HEAD_WIDTH + ONES_ROWS, tq), F32),
            pltpu.VMEM((tk, tk // 2), F32),
            pltpu.VMEM((tk, tk // 2), BF16),
            pltpu.VMEM((1, tk // 2), F32),
        ],
        compiler_params=_params("parallel", "parallel", "arbitrary"),
        name="diff_attention",
    )(qkv, qkv, qkv, bias_tiles, lam_vec, subln_w.reshape(1, HEAD_WIDTH))


def _tail_kernel(attn_ref, u_ref, halo_ref, ga_ref, gp_ref, x_ref, pw_ref, ps_ref, wba_ref, wbp_ref, wout_ref,
                 g_ref, b_ref, of_ref, ob_ref, sums_ref, *, tiles_per_seq):
    i = pl.program_id(0)
    tm = u_ref.shape[0]
    tile_in_seq = i % tiles_per_seq
    r0 = SUBLANES + MAX_WINDOW
    n = r0 + tm
    sums_ref[:, 0:SUBLANES, :] = jnp.zeros((sums_ref.shape[0], SUBLANES, POOL_WIDTH), F32)
    sums_ref[0, SUBLANES:r0, :] = jnp.where(tile_in_seq == 0, 0.0, halo_ref[...])
    sums_ref[0, r0:n, :] = u_ref[...]
    t = tile_in_seq * tm + lax.broadcasted_iota(jnp.int32, (tm, 1), 0)

    def doubled(k, lo, cs):
        shift = 2 ** k
        return sums_ref[k, lo:n, cs] + sums_ref[k, lo - shift:n - shift, cs]

    n_groups = len(POOL_WINDOWS)
    a_cols = D_MODEL // n_groups
    attn = jnp.concatenate([attn_ref[h] for h in range(N_HEADS)], axis=-1)

    def attn_branch_chunk(j):
        return jnp.dot(attn, wba_ref[:, j * a_cols:(j + 1) * a_cols], preferred_element_type=F32)

    a_chunks = [attn_branch_chunk(0)]
    for k in range(n_groups - 1):
        cs = slice((k + 1) * POOL_GROUP, POOL_WIDTH)
        sums_ref[k + 1, SUBLANES:n, cs] = doubled(k, SUBLANES, cs)

    pooled = []
    for g, w in enumerate(POOL_WINDOWS):
        if g > 0:
            a_chunks.append(attn_branch_chunk(g))
        cs = slice(g * POOL_GROUP, (g + 1) * POOL_GROUP)
        cur = sums_ref[0, r0:n, cs]
        win_sum = doubled(g, r0, cs)
        cnt = jnp.minimum(t + 1, w).astype(F32)
        z = (win_sum / cnt - cur).astype(BF16)
        y = jnp.dot(z, pw_ref[g], preferred_element_type=F32)
        pooled.append((y * ps_ref[:, cs]).astype(BF16))
    pool = jnp.concatenate(pooled, axis=-1)

    a = jnp.concatenate(a_chunks, axis=-1)
    p = jnp.dot(pool, wbp_ref[...], preferred_element_type=F32)
    merged = (ga_ref[...] * a + gp_ref[...] * p).astype(BF16)

    hm = tm // 2
    top, bottom = slice(0, hm), slice(hm, tm)
    o_cols = D_MODEL // 4

    def bottom_chunk(j):
        return jnp.dot(merged[bottom], wout_ref[:, j * o_cols:(j + 1) * o_cols], preferred_element_type=F32)

    def store(rows, out):
        of_ref[rows, :] = out
        ob_ref[rows, :] = out.astype(BF16)

    r_top = ALPHA * x_ref[top, :] + jnp.dot(merged[top], wout_ref[...], preferred_element_type=F32)
    chunks = [bottom_chunk(0)]
    centered = r_top - jnp.mean(r_top, axis=-1, keepdims=True)
    chunks.append(bottom_chunk(1))
    inv_std = lax.rsqrt(jnp.mean(centered * centered, axis=-1, keepdims=True) + LN_EPS)
    chunks.append(bottom_chunk(2))
    store(top, centered * inv_std * g_ref[...] + b_ref[...])
    chunks.append(bottom_chunk(3))
    r_bottom = ALPHA * x_ref[bottom, :] + jnp.concatenate(chunks, axis=-1)
    store(bottom, _layer_norm(r_bottom, g_ref[...], b_ref[...]))


def _resident(shape):
    zeros = (0,) * len(shape)
    return pl.BlockSpec(shape, lambda i: zeros, pipeline_mode=pl.Buffered(1))


def _mixer_tail(attn, u, gates, x, pool_w, pool_scale, w_ba, w_bp, w_out, ln_g, ln_b, seq):
    t = x.shape[0]
    tm = TAIL_TM
    assert seq % tm == 0 and tm % MAX_WINDOW == 0
    assert POOL_WINDOWS == tuple(2 ** (g + 1) for g in range(len(POOL_WINDOWS))) and MAX_WINDOW >= SUBLANES
    halo_blocks = tm // MAX_WINDOW
    kernel = functools.partial(_tail_kernel, tiles_per_seq=seq // tm)
    return pl.pallas_call(
        kernel,
        grid=(t // tm,),
        in_specs=[
            pl.BlockSpec((N_HEADS, tm, HEAD_WIDTH), lambda i: (0, i, 0)),
            pl.BlockSpec((tm, POOL_WIDTH), lambda i: (i, 0)),
            pl.BlockSpec((MAX_WINDOW, POOL_WIDTH), lambda i: (jnp.maximum(i * halo_blocks - 1, 0), 0)),
            pl.BlockSpec((tm, D_MODEL), lambda i: (i, 0)),
            pl.BlockSpec((tm, D_MODEL), lambda i: (i, 1)),
            pl.BlockSpec((tm, D_MODEL), lambda i: (i, 0)),
            _resident(pool_w.shape),
            _resident((1, POOL_WIDTH)),
            _resident(w_ba.shape),
            _resident(w_bp.shape),
            _resident(w_out.shape),
            _resident((1, D_MODEL)),
            _resident((1, D_MODEL)),
        ],
        out_specs=[
            pl.BlockSpec((tm, D_MODEL), lambda i: (i, 0)),
            pl.BlockSpec((tm, D_MODEL), lambda i: (i, 0)),
        ],
        out_shape=[
            jax.ShapeDtypeStruct((t, D_MODEL), F32),
            jax.ShapeDtypeStruct((t, D_MODEL), BF16),
        ],
        scratch_shapes=[pltpu.VMEM((len(POOL_WINDOWS), SUBLANES + MAX_WINDOW + tm, POOL_WIDTH), F32)],
        compiler_params=_params("parallel"),
        name="mixer_tail",
    )(attn, u, u, gates, gates, x, pool_w, pool_scale.reshape(1, POOL_WIDTH), w_ba, w_bp, w_out,
      ln_g.reshape(1, D_MODEL), ln_b.reshape(1, D_MODEL))


def _swiglu(hb, wg, wu):
    g = jnp.dot(hb, wg, preferred_element_type=F32)
    u = jnp.dot(hb, wu, preferred_element_type=F32)
    return (g * jax.nn.sigmoid(g) * u).astype(BF16)


def _dense_ffn_kernel(hb_ref, hf_ref, wg_ref, wu_ref, wd_ref, g_ref, b_ref, of_ref, ob_ref, acc_ref):
    f = pl.program_id(1)

    @pl.when(f == 0)
    def _():
        acc_ref[...] = jnp.zeros(acc_ref.shape, F32)

    a = _swiglu(hb_ref[...], wg_ref[...], wu_ref[...])
    acc_ref[...] += jnp.dot(a, wd_ref[...], preferred_element_type=F32)

    @pl.when(f == pl.num_programs(1) - 1)
    def _():
        out = _layer_norm(ALPHA * hf_ref[...] + acc_ref[...], g_ref[...], b_ref[...])
        of_ref[...] = out
        ob_ref[...] = out.astype(BF16)


def _dense_ffn(hb, hf, wg, wu, wd, ln_g, ln_b):
    t = hb.shape[0]
    d_ff = wg.shape[1]
    tm, tf = FFN_TM, FFN_TF
    assert d_ff % tf == 0
    return pl.pallas_call(
        _dense_ffn_kernel,
        grid=(t // tm, d_ff // tf),
        in_specs=[
            pl.BlockSpec((tm, D_MODEL), lambda i, f: (i, 0)),
            pl.BlockSpec((tm, D_MODEL), lambda i, f: (i, 0)),
            pl.BlockSpec((D_MODEL, tf), lambda i, f: (0, f)),
            pl.BlockSpec((D_MODEL, tf), lambda i, f: (0, f)),
            pl.BlockSpec((tf, D_MODEL), lambda i, f: (f, 0)),
            pl.BlockSpec((1, D_MODEL), lambda i, f: (0, 0)),
            pl.BlockSpec((1, D_MODEL), lambda i, f: (0, 0)),
        ],
        out_specs=[
            pl.BlockSpec((tm, D_MODEL), lambda i, f: (i, 0)),
            pl.BlockSpec((tm, D_MODEL), lambda i, f: (i, 0)),
        ],
        out_shape=[
            jax.ShapeDtypeStruct((t, D_MODEL), F32),
            jax.ShapeDtypeStruct((t, D_MODEL), BF16),
        ],
        scratch_shapes=[pltpu.VMEM((tm, D_MODEL), F32)],
        compiler_params=_params("parallel", "arbitrary"),
        name="dense_ffn",
    )(hb, hf, wg, wu, wd, ln_g.reshape(1, D_MODEL), ln_b.reshape(1, D_MODEL))


ROUTE_W0, ROUTE_W1, ROUTE_E0, ROUTE_E1, ROUTE_R0, ROUTE_R1 = range(6)


def _router_kernel(h_ref, rw_ref, route_ref, counts_ref, carry_ref):
    i = pl.program_id(0)
    tm = h_ref.shape[0]

    @pl.when(i == 0)
    def _():
        carry_ref[...] = jnp.zeros(carry_ref.shape, F32)

    logits = jnp.dot(h_ref[...], rw_ref[...], preferred_element_type=F32)
    lane = lax.broadcasted_iota(jnp.int32, (tm, LANES), 1)
    lg = jnp.where(lane < N_EXPERTS, logits, -jnp.inf)
    m1 = jnp.max(lg, axis=-1, keepdims=True)
    e1 = jnp.min(jnp.where(lg == m1, lane, LANES), axis=-1, keepdims=True)
    lg2 = jnp.where(lane == e1, -jnp.inf, lg)
    m2 = jnp.max(lg2, axis=-1, keepdims=True)
    e2 = jnp.min(jnp.where(lg2 == m2, lane, LANES), axis=-1, keepdims=True)
    x2 = jnp.exp(m2 - m1)
    w1 = 1.0 / (1.0 + x2)
    w2 = x2 / (1.0 + x2)

    sel1 = lane == e1
    sel2 = lane == e2
    mask = jnp.where(sel1 | sel2, 1.0, 0.0)
    r = lax.broadcasted_iota(jnp.int32, (tm, tm), 0)
    c = lax.broadcasted_iota(jnp.int32, (tm, tm), 1)
    tri = jnp.where(r >= c, 1.0, 0.0).astype(BF16)
    incl = jnp.dot(tri, mask.astype(BF16), preferred_element_type=F32)
    rank = incl - mask + carry_ref[...]
    total = carry_ref[...] + jnp.sum(mask, axis=0, keepdims=True)
    carry_ref[...] = total
    counts_ref[...] = total

    r1 = jnp.sum(jnp.where(sel1, rank, 0.0), axis=-1, keepdims=True)
    r2 = jnp.sum(jnp.where(sel2, rank, 0.0), axis=-1, keepdims=True)
    cols = (w1, w2, e1.astype(F32), e2.astype(F32), r1, r2)
    route = jnp.zeros((tm, LANES), F32)
    for idx, val in enumerate(cols):
        route = jnp.where(lane == idx, val, route)
    route_ref[...] = route


def _router(hb, router_w):
    t = hb.shape[0]
    tm = ROUTER_TM
    rw = jnp.pad(router_w, ((0, 0), (0, LANES - N_EXPERTS))).astype(BF16)
    return pl.pallas_call(
        _router_kernel,
        grid=(t // tm,),
        in_specs=[
            pl.BlockSpec((tm, D_MODEL), lambda i: (i, 0)),
            pl.BlockSpec((D_MODEL, LANES), lambda i: (0, 0)),
        ],
        out_specs=[
            pl.BlockSpec((tm, LANES), lambda i: (i, 0)),
            pl.BlockSpec((1, LANES), lambda i: (0, 0)),
        ],
        out_shape=[
            jax.ShapeDtypeStruct((t, LANES), F32),
            jax.ShapeDtypeStruct((1, LANES), F32),
        ],
        scratch_shapes=[pltpu.VMEM((1, LANES), F32)],
        compiler_params=_params("arbitrary"),
        name="moe_router",
    )(hb, rw)


def _dispatch_kernel(pad_start_ref, pad_count_ref, n_used_ref, dest_ref, h_ref, xs_hbm, zero_ref, sem, pad_sem):
    n = h_ref.shape[0]
    tile_rows = zero_ref.shape[0]
    n_tiles = xs_hbm.shape[0] // tile_rows

    @pl.when(pl.program_id(0) == 0)
    def _():
        zero_ref[...] = jnp.zeros(zero_ref.shape, F32)
        for e in range(N_EXPERTS):
            def pad_copy(j, e=e):
                return pltpu.make_async_copy(
                    zero_ref.at[pl.ds(0, 1)], xs_hbm.at[pl.ds(pad_start_ref[e] + j, 1)], pad_sem)

            lax.fori_loop(0, pad_count_ref[e], lambda j, c: (pad_copy(j).start(), c)[1], 0)
            lax.fori_loop(0, pad_count_ref[e], lambda j, c: (pad_copy(j).wait(), c)[1], 0)

        def tile_copy(j):
            return pltpu.make_async_copy(
                zero_ref, xs_hbm.at[pl.ds(pl.multiple_of(j * tile_rows, tile_rows), tile_rows)], pad_sem)

        lax.fori_loop(n_used_ref[0], n_tiles, lambda j, c: (tile_copy(j).start(), c)[1], 0)
        lax.fori_loop(n_used_ref[0], n_tiles, lambda j, c: (tile_copy(j).wait(), c)[1], 0)

    def row_copy(r, k):
        return pltpu.make_async_copy(
            h_ref.at[pl.ds(r, 1)], xs_hbm.at[pl.ds(dest_ref[0, 0, TOP_K * r + k], 1)], sem)

    def start(r, carry):
        for k in range(TOP_K):
            row_copy(r, k).start()
        return carry

    def wait(r, carry):
        for k in range(TOP_K):
            row_copy(r, k).wait()
        return carry

    lax.fori_loop(0, n, start, 0, unroll=ROW_DMA_UNROLL)
    lax.fori_loop(0, n, wait, 0, unroll=ROW_DMA_UNROLL)


def _dispatch(hf, dest, pad_start, pad_count, n_used, n_tiles):
    t = hf.shape[0]
    tm = DISPATCH_TM
    return pl.pallas_call(
        _dispatch_kernel,
        grid_spec=pltpu.PrefetchScalarGridSpec(
            num_scalar_prefetch=3,
            grid=(t // tm,),
            in_specs=[
                pl.BlockSpec((1, 1, TOP_K * tm), lambda i, ps, pc, nu: (i, 0, 0), memory_space=pltpu.SMEM),
                pl.BlockSpec((tm, D_MODEL), lambda i, ps, pc, nu: (i, 0)),
            ],
            out_specs=pl.BlockSpec(memory_space=pl.ANY),
            scratch_shapes=[
                pltpu.VMEM((MOE_TM, D_MODEL), F32),
                pltpu.SemaphoreType.DMA(()),
                pltpu.SemaphoreType.DMA(()),
            ],
        ),
        out_shape=jax.ShapeDtypeStruct((n_tiles * MOE_TM, D_MODEL), F32),
        compiler_params=_params("arbitrary"),
        name="moe_dispatch",
    )(pad_start, pad_count, n_used, dest.reshape(t // tm, 1, TOP_K * tm), hf)


def _moe_ffn_kernel(te_ref, ok_ref, src_ref, xs_ref, wg_ref, wu_ref, wd_ref, ys_ref, xb_ref, acc_ref):
    del te_ref, src_ref
    i = pl.program_id(0)
    f = pl.program_id(1)
    ok = ok_ref[i] == 1

    @pl.when(ok & (f == 0))
    def _():
        xb_ref[...] = xs_ref[...].astype(BF16)
        acc_ref[...] = jnp.zeros(acc_ref.shape, F32)

    @pl.when(ok)
    def _():
        a = _swiglu(xb_ref[...], wg_ref[...], wu_ref[...])
        acc_ref[...] += jnp.dot(a, wd_ref[...], preferred_element_type=F32)

    @pl.when(f == pl.num_programs(1) - 1)
    def _():
        ys_ref[...] = jnp.where(ok, acc_ref[...], 0.0)


def _moe_ffn(xs, tile_expert, tile_ok, tile_src, wg, wu, wd):
    n_rows = xs.shape[0]
    d_ff = wg.shape[2]
    tm, tf = MOE_TM, MOE_TF
    assert d_ff % tf == 0
    n_f = d_ff // tf
    n_tiles = n_rows // tm

    def f_idx(i, f, ok):
        return jnp.where(ok[i] == 1, f, n_f - 1)

    return pl.pallas_call(
        _moe_ffn_kernel,
        grid_spec=pltpu.PrefetchScalarGridSpec(
            num_scalar_prefetch=3,
            grid=(n_tiles, n_f),
            in_specs=[
                pl.BlockSpec((tm, D_MODEL), lambda i, f, te, ok, src: (src[i], 0)),
                pl.BlockSpec((None, D_MODEL, tf), lambda i, f, te, ok, src: (te[i], 0, f_idx(i, f, ok))),
                pl.BlockSpec((None, D_MODEL, tf), lambda i, f, te, ok, src: (te[i], 0, f_idx(i, f, ok))),
                pl.BlockSpec((None, tf, D_MODEL), lambda i, f, te, ok, src: (te[i], f_idx(i, f, ok), 0)),
            ],
            out_specs=pl.BlockSpec((tm, D_MODEL), lambda i, f, te, ok, src: (i, 0)),
            scratch_shapes=[pltpu.VMEM((tm, D_MODEL), BF16), pltpu.VMEM((tm, D_MODEL), F32)],
        ),
        out_shape=jax.ShapeDtypeStruct((n_rows, D_MODEL), F32),
        compiler_params=_params("arbitrary", "arbitrary"),
        name="moe_ffn",
    )(tile_expert, tile_ok, tile_src, xs, wg, wu, wd)


def _combine_kernel(dest_ref, h_ref, route_ref, ys_hbm, g_ref, b_ref, o_ref, y_ref, sem):
    tm = h_ref.shape[0]

    def row_copy(r, k):
        return pltpu.make_async_copy(
            ys_hbm.at[pl.ds(dest_ref[0, 0, TOP_K * r + k], 1)], y_ref.at[k, pl.ds(r, 1)], sem)

    def start(r, carry):
        for k in range(TOP_K):
            row_copy(r, k).start()
        return carry

    def wait(r, carry):
        for k in range(TOP_K):
            row_copy(r, k).wait()
        return carry

    lax.fori_loop(0, tm, start, 0, unroll=ROW_DMA_UNROLL)
    lax.fori_loop(0, tm, wait, 0, unroll=ROW_DMA_UNROLL)

    route = route_ref[...]
    w1 = route[:, ROUTE_W0:ROUTE_W0 + 1]
    w2 = route[:, ROUTE_W1:ROUTE_W1 + 1]
    y = w1 * y_ref[0] + w2 * y_ref[1]
    o_ref[...] = _layer_norm(ALPHA * h_ref[...] + y, g_ref[...], b_ref[...])


def _combine(hf, route, dest, ys, ln_g, ln_b):
    t = hf.shape[0]
    tm = COMBINE_TM
    return pl.pallas_call(
        _combine_kernel,
        grid=(t // tm,),
        in_specs=[
            pl.BlockSpec((1, 1, TOP_K * tm), lambda i: (i, 0, 0), memory_space=pltpu.SMEM),
            pl.BlockSpec((tm, D_MODEL), lambda i: (i, 0)),
            pl.BlockSpec((tm, LANES), lambda i: (i, 0)),
            pl.BlockSpec(memory_space=pl.ANY),
            pl.BlockSpec((1, D_MODEL), lambda i: (0, 0)),
            pl.BlockSpec((1, D_MODEL), lambda i: (0, 0)),
        ],
        out_specs=pl.BlockSpec((tm, D_MODEL), lambda i: (i, 0)),
        out_shape=jax.ShapeDtypeStruct((t, D_MODEL), F32),
        scratch_shapes=[pltpu.VMEM((TOP_K, tm, D_MODEL), F32), pltpu.SemaphoreType.DMA(())],
        compiler_params=_params("arbitrary"),
        name="moe_combine",
    )(dest.reshape(t // tm, 1, TOP_K * tm), hf, route, ys, ln_g.reshape(1, D_MODEL), ln_b.reshape(1, D_MODEL))


def _moe(hf, hb, router_w, wg, wu, wd, ln_g, ln_b):
    t = hf.shape[0]
    tm = MOE_TM
    n_tiles = (TOP_K * t) // tm + N_EXPERTS
    route, counts = _router(hb, router_w)

    counts = counts[0, :N_EXPERTS].astype(jnp.int32)
    tiles_per_expert = (counts + tm - 1) // tm
    tile_end = jnp.cumsum(tiles_per_expert)
    group_start = (tile_end - tiles_per_expert) * tm
    tile_ids = jnp.arange(n_tiles, dtype=jnp.int32)
    n_used = tile_end[-1]
    last_used = jnp.maximum(n_used - 1, 0)
    expert_of_tile = jnp.sum(tile_ids[:, None] >= tile_end[None, :], axis=1).astype(jnp.int32)
    tile_ok = (tile_ids < n_used).astype(jnp.int32)
    tile_expert = jnp.where(tile_ok == 1, expert_of_tile, expert_of_tile[last_used])
    tile_expert = jnp.minimum(tile_expert, N_EXPERTS - 1)
    tile_src = jnp.where(tile_ok == 1, tile_ids, last_used)

    experts = route[:, ROUTE_E0:ROUTE_E1 + 1].astype(jnp.int32)
    ranks = route[:, ROUTE_R0:ROUTE_R1 + 1].astype(jnp.int32)
    dest = (group_start[experts] + ranks).reshape(-1)

    xs = _dispatch(hf, dest, group_start + counts, tiles_per_expert * tm - counts, n_used.reshape(1), n_tiles)
    ys = _moe_ffn(xs, tile_expert, tile_ok, tile_src, wg, wu, wd)
    return _combine(hf, route, dest, ys, ln_g, ln_b)


def kernel(x, w_in, lambdas, subln_w, pool_w, pool_scale, w_branch_attn, w_branch_pool, w_out, rel_bias,
           ln1_g, ln1_b, dense_w_gate, dense_w_up, dense_w_down, router_w, moe_w_gate, moe_w_up, moe_w_down,
           ln2_g, ln2_b):
    b, s, d = x.shape
    t = b * s
    xf = x.reshape(t, d)
    xin = xf
    bias_tiles = _bias_tiles(rel_bias)
    for l in range(DEPTH):
        lambda_init = 0.8 - 0.6 * math.exp(-0.3 * l)
        qkv, u, gates = _in_proj(xin, w_in[l].astype(BF16))
        attn = _attention(qkv, bias_tiles, lambdas[l], subln_w[l], lambda_init, b, s)
        hf, hb = _mixer_tail(attn, u, gates, xf, pool_w[l].astype(BF16), pool_scale[l],
                             w_branch_attn[l].astype(BF16), w_branch_pool[l].astype(BF16), w_out[l].astype(BF16),
                             ln1_g[l], ln1_b[l], s)
        if l % 2 == 0:
            xf, xin = _dense_ffn(hb, hf, dense_w_gate[l // 2].astype(BF16), dense_w_up[l // 2].astype(BF16),
                                 dense_w_down[l // 2].astype(BF16), ln2_g[l], ln2_b[l])
        else:
            xf = _moe(hf, hb, router_w[l // 2], moe_w_gate[l // 2].astype(BF16), moe_w_up[l // 2].astype(BF16),
                      moe_w_down[l // 2].astype(BF16), ln2_g[l], ln2_b[l])
            xin = xf
    return xf.reshape(b, s, d)
```

```python
import collections
import functools
import math

import jax
import jax.numpy as jnp
from jax import lax
from jax.experimental import pallas as pl
from jax.experimental.pallas import tpu as pltpu

D_MODEL = 2048
DEPTH = 2
N_HEADS = 8
HEAD_DIM = 64
HEAD_WIDTH = 2 * HEAD_DIM
ATTN_WIDTH = N_HEADS * HEAD_WIDTH
POOL_WINDOWS = (2, 4, 8, 16)
POOL_GROUP = 256
POOL_WIDTH = POOL_GROUP * len(POOL_WINDOWS)
MAX_WINDOW = max(POOL_WINDOWS)
N_BUCKETS = 32
MAX_DISTANCE = 128
N_EXPERTS = 8
TOP_K = 2
ALPHA = (2.0 * DEPTH) ** 0.25
LN_EPS = 1e-5
LOG2E = math.log2(math.e)
Q_SCALE = HEAD_DIM ** -0.5 * LOG2E
ONES_ROWS = 16

LANES = 128
SUBLANES = 8
MASK_VALUE = -1e30
VMEM_LIMIT = 56 * 1024 * 1024

PROJ_TM = 512
PROJ_TN = 1024
ATTN_TQ = 2048
ATTN_TK = 512
TAIL_TM = 256
FFN_TM = 512
FFN_TF = 512
ROUTER_TM = 512
DISPATCH_TM = 512
MOE_TM = 512
MOE_TF = 1024
COMBINE_TM = 256
ROW_DMA_UNROLL = 8

F32 = jnp.float32
BF16 = jnp.bfloat16


def _params(*sem):
    return pltpu.CompilerParams(dimension_semantics=sem, vmem_limit_bytes=VMEM_LIMIT)


def _layer_norm(r, g, b):
    mu = jnp.mean(r, axis=-1, keepdims=True)
    c = r - mu
    var = jnp.mean(c * c, axis=-1, keepdims=True)
    return c * lax.rsqrt(var + LN_EPS) * g + b


PROJ_HALF = 2 * D_MODEL
assert 3 * ATTN_WIDTH + POOL_WIDTH == PROJ_HALF


def _proj_mix_kernel(x_ref, w_ref, qkv_ref, u_ref):
    x = x_ref[...].astype(BF16)
    for j in range(PROJ_HALF // PROJ_TN):
        cols = slice(j * PROJ_TN, (j + 1) * PROJ_TN)
        acc = jnp.dot(x, w_ref[:, cols], preferred_element_type=F32)
        if cols.stop <= ATTN_WIDTH:
            qkv_ref[:, cols] = (acc * Q_SCALE).astype(BF16)
        elif cols.stop <= 3 * ATTN_WIDTH:
            qkv_ref[:, cols] = acc.astype(BF16)
        else:
            u_ref[:, j * PROJ_TN - 3 * ATTN_WIDTH:(j + 1) * PROJ_TN - 3 * ATTN_WIDTH] = acc


def _proj_gates_kernel(x_ref, w_ref, g_ref):
    x = x_ref[...].astype(BF16)
    for j in range(PROJ_HALF // PROJ_TN):
        cols = slice(j * PROJ_TN, (j + 1) * PROJ_TN)
        g_ref[:, cols] = jax.nn.sigmoid(jnp.dot(x, w_ref[:, cols], preferred_element_type=F32))


def _in_proj(x, w):
    t, k = x.shape
    tm = PROJ_TM
    assert ATTN_WIDTH % PROJ_TN == 0 and POOL_WIDTH % PROJ_TN == 0
    x_spec = pl.BlockSpec((tm, k), lambda i: (i, 0))

    def w_spec(half):
        return pl.BlockSpec((k, PROJ_HALF), lambda i: (0, half), pipeline_mode=pl.Buffered(1))

    qkv, u = pl.pallas_call(
        _proj_mix_kernel,
        grid=(t // tm,),
        in_specs=[x_spec, w_spec(0)],
        out_specs=[
            pl.BlockSpec((tm, 3 * ATTN_WIDTH), lambda i: (i, 0)),
            pl.BlockSpec((tm, POOL_WIDTH), lambda i: (i, 0)),
        ],
        out_shape=[
            jax.ShapeDtypeStruct((t, 3 * ATTN_WIDTH), BF16),
            jax.ShapeDtypeStruct((t, POOL_WIDTH), F32),
        ],
        compiler_params=_params("parallel"),
        name="in_proj_mix",
    )(x, w)
    gates = pl.pallas_call(
        _proj_gates_kernel,
        grid=(t // tm,),
        in_specs=[x_spec, w_spec(1)],
        out_specs=pl.BlockSpec((tm, PROJ_HALF), lambda i: (i, 0)),
        out_shape=jax.ShapeDtypeStruct((t, PROJ_HALF), F32),
        compiler_params=_params("parallel"),
        name="in_proj_gates",
    )(x, w)
    return qkv, u, gates


def _bias_kernel(rb_ref, o_ref):
    h = pl.program_id(0)
    tk, tq = o_ref.shape[2], o_ref.shape[3]
    kpos = lax.broadcasted_iota(jnp.int32, (tk, tq), 0)
    qpos = lax.broadcasted_iota(jnp.int32, (tk, tq), 1)
    max_exact = N_BUCKETS // 2
    far = rb_ref[h * N_BUCKETS + N_BUCKETS - 1]
    for blk in range(2):
        dist = qpos - kpos + blk * tk
        n = jnp.maximum(dist, 0)
        large = max_exact + (
            jnp.log(jnp.maximum(n, 1).astype(F32) / max_exact) / math.log(MAX_DISTANCE / max_exact)
            * (N_BUCKETS - max_exact)
        ).astype(jnp.int32)
        large = jnp.minimum(large, N_BUCKETS - 1)
        bucket = jnp.where(n < max_exact, n, large)
        bias = jnp.zeros((tk, tq), F32)
        for b in range(N_BUCKETS):
            bias = jnp.where(bucket == b, rb_ref[h * N_BUCKETS + b], bias)
        bias = (bias - far) * LOG2E
        if blk == 0:
            bias = jnp.where(dist >= 0, bias, MASK_VALUE)
        o_ref[0, blk] = bias


def _bias_tiles(rel_bias):
    assert ATTN_TK >= MAX_DISTANCE
    rb = jnp.transpose(rel_bias).reshape(-1)
    return pl.pallas_call(
        _bias_kernel,
        grid_spec=pltpu.PrefetchScalarGridSpec(
            num_scalar_prefetch=1,
            grid=(N_HEADS,),
            in_specs=[],
            out_specs=pl.BlockSpec((1, 2, ATTN_TK, ATTN_TK), lambda h, rb: (h, 0, 0, 0)),
        ),
        out_shape=jax.ShapeDtypeStruct((N_HEADS, 2, ATTN_TK, ATTN_TK), F32),
        compiler_params=_params("arbitrary"),
        name="bias_tiles",
    )(rb)


_Chunk = collections.namedtuple("_Chunk", ["c", "cols", "sub", "first_half", "bias_cols"])


def _attn_kernel(q_ref, k_ref, v_ref, bias_ref, lam_ref, sub_ref, o_ref, qt_ref, vt_ref, m_ref, acc_ref,
                 s0_ref, pp_ref, pa_ref, *, lambda_init):
    qi = pl.program_id(2)
    tq = q_ref.shape[1]
    n_kb, _, tk = vt_ref.shape

    @pl.when(qi == 0)
    def _():
        row = lax.broadcasted_iota(jnp.int32, (ONES_ROWS, tk), 0)
        ones_rows = jnp.where(row == 0, 1.0, 0.0).astype(BF16)
        for kb in range(n_kb):
            vt_ref[kb, 0:HEAD_WIDTH, :] = v_ref[0, kb * tk:(kb + 1) * tk, :].astype(F32).T.astype(BF16)
            vt_ref[kb, HEAD_WIDTH:HEAD_WIDTH + ONES_ROWS, :] = ones_rows

    qt = q_ref[0].astype(F32).T
    dim = lax.broadcasted_iota(jnp.int32, (HEAD_WIDTH, tq), 0)
    qt_ref[0] = jnp.where(dim < HEAD_DIM, qt, 0.0).astype(BF16)
    qt_ref[1] = jnp.where(dim >= HEAD_DIM, qt, 0.0).astype(BF16)

    m_ref[...] = jnp.full(m_ref.shape, MASK_VALUE, F32)
    acc_ref[...] = jnp.zeros(acc_ref.shape, F32)

    n_sub = tq // tk
    half = tk // 2
    chunks = [
        _Chunk(c, slice(j * half, (j + 1) * half), j // 2, j % 2 == 0, slice(j % 2 * half, (j % 2 + 1) * half))
        for c in range(2) for j in range(2 * n_sub)
    ]
    far = ("far",) * n_sub

    def active(kinds):
        return [ch for ch in chunks if kinds[ch.sub] is not None]

    def keys(kb):
        return k_ref[0, pl.ds(pl.multiple_of(kb * tk, tk), tk), :]

    def key_rows(ch, kinds):
        return tk // 2 if kinds[ch.sub] == "diag" and ch.first_half else tk

    def scores(kk, ch, kinds):
        return jnp.dot(kk[:key_rows(ch, kinds)], qt_ref[ch.c, :, ch.cols], preferred_element_type=F32)

    def add_bias(s, ch, kinds):
        kind = kinds[ch.sub]
        if kind == "diag":
            return s + bias_ref[0, 0, 0:s.shape[0], ch.bias_cols]
        if kind == "near" and ch.first_half:
            r0 = tk - MAX_DISTANCE
            return jnp.concatenate([s[:r0], s[r0:] + bias_ref[0, 1, r0:tk, ch.bias_cols]], axis=0)
        return s

    def softmax_update(s, ch, kinds):
        s = add_bias(s, ch, kinds)
        m_prev = m_ref[ch.c, :, ch.cols]
        m_new = jnp.maximum(m_prev, jnp.max(s, axis=0, keepdims=True))
        alpha = jnp.exp2(m_prev - m_new)
        p = jnp.exp2(s - m_new)
        m_ref[ch.c, :, ch.cols] = m_new
        return p.astype(BF16), alpha

    def accumulate(vt, p, alpha, ch):
        pv = jnp.dot(vt[:, :p.shape[0]], p, preferred_element_type=F32)
        acc_ref[ch.c, :, ch.cols] = alpha * acc_ref[ch.c, :, ch.cols] + pv

    def accumulate_pending(kb):
        accumulate(vt_ref[kb], pp_ref[...], pa_ref[...], chunks[-1])

    def process(kb, kinds, next_kinds):
        act = active(kinds)
        assert act[-1] is chunks[-1]
        kk = keys(kb)
        vt = vt_ref[kb]
        s_next = scores(kk, act[1], kinds)
        accumulate_pending(jnp.maximum(kb - 1, 0))
        p, alpha = softmax_update(s0_ref[0:key_rows(act[0], kinds), :], act[0], kinds)
        for i in range(1, len(act)):
            s_cur = s_next
            if i + 1 < len(act):
                s_next = scores(kk, act[i + 1], kinds)
            elif next_kinds is not None:
                s0_ref[...] = scores(keys(kb + 1), active(next_kinds)[0], far)
            accumulate(vt, p, alpha, act[i - 1])
            p, alpha = softmax_update(s_cur, act[i], kinds)
        pp_ref[...], pa_ref[...] = p, alpha

    s0_ref[...] = scores(keys(0), chunks[0], far)
    pp_ref[...] = jnp.zeros(pp_ref.shape, BF16)
    pa_ref[...] = jnp.ones(pa_ref.shape, F32)

    base = n_sub * qi
    n_far = jnp.maximum(base - 1, 0)

    def far_pair(j, carry):
        process(2 * j, far, far)
        process(2 * j + 1, far, far)
        return carry

    lax.fori_loop(0, n_far // 2, far_pair, 0)

    @pl.when(n_far % 2 == 1)
    def _():
        process(n_far - 1, far, far)

    def kinds_at(r):
        return tuple("far" if r < j - 1 else "near" if r == j - 1 else "diag" if r == j else None
                     for j in range(n_sub))

    def last_blocks(first):
        for r in range(first, n_sub):
            process(base + r, kinds_at(r), kinds_at(r + 1) if r + 1 < n_sub else None)

    @pl.when(qi >= 1)
    def _():
        last_blocks(-1)

    @pl.when(qi == 0)
    def _():
        last_blocks(0)

    accumulate_pending(base + n_sub - 1)

    lv = lam_ref[...]
    lam = (
        jnp.exp(jnp.sum(lv[0:1] * lv[1:2], axis=-1, keepdims=True))
        - jnp.exp(jnp.sum(lv[2:3] * lv[3:4], axis=-1, keepdims=True))
        + lambda_init
    )
    num = [acc_ref[c, 0:HEAD_WIDTH, :] for c in range(2)]
    den = [acc_ref[c, HEAD_WIDTH:HEAD_WIDTH + 1, :] for c in range(2)]
    ot = num[0] / den[0] - lam * (num[1] / den[1])
    ot = ot * lax.rsqrt(jnp.mean(ot * ot, axis=0, keepdims=True) + LN_EPS)
    o = ot.T * (sub_ref[...] * (1.0 - lambda_init))
    o_ref[0] = o.astype(o_ref.dtype)


def _attention(qkv, bias_tiles, lam_vec, subln_w, lambda_init):
    b, s, _ = qkv.shape
    tq, tk = ATTN_TQ, ATTN_TK
    assert tq % tk == 0 and s % tq == 0
    assert tk // 2 >= MAX_DISTANCE and (tk - MAX_DISTANCE) % SUBLANES == 0
    kernel = functools.partial(_attn_kernel, lambda_init=lambda_init)
    return pl.pallas_call(
        kernel,
        grid=(b, N_HEADS, s // tq),
        in_specs=[
            pl.BlockSpec((1, tq, HEAD_WIDTH), lambda bi, h, qi: (bi, qi, h)),
            pl.BlockSpec((1, s, HEAD_WIDTH), lambda bi, h, qi: (bi, 0, N_HEADS + h)),
            pl.BlockSpec((1, s, HEAD_WIDTH), lambda bi, h, qi: (bi, 0, 2 * N_HEADS + h)),
            pl.BlockSpec((1, 2, tk, tk), lambda bi, h, qi: (h, 0, 0, 0)),
            pl.BlockSpec((4, HEAD_DIM), lambda bi, h, qi: (0, 0)),
            pl.BlockSpec((1, HEAD_WIDTH), lambda bi, h, qi: (0, 0)),
        ],
        out_specs=pl.BlockSpec((1, tq, HEAD_WIDTH), lambda bi, h, qi: (bi, qi, h)),
        out_shape=jax.ShapeDtypeStruct((b, s, ATTN_WIDTH), BF16),
        scratch_shapes=[
            pltpu.VMEM((2, HEAD_WIDTH, tq), BF16),
            pltpu.VMEM((s // tk, HEAD_WIDTH + ONES_ROWS, tk), BF16),
            pltpu.VMEM((2, 1, tq), F32),
            pltpu.VMEM((2, HEAD_WIDTH + ONES_ROWS, tq), F32),
            pltpu.VMEM((tk, tk // 2), F32),
            pltpu.VMEM((tk, tk // 2), BF16),
            pltpu.VMEM((1, tk // 2), F32),
        ],
        compiler_params=_params("parallel", "parallel", "arbitrary"),
        name="diff_attention",
    )(qkv, qkv, qkv, bias_tiles, lam_vec, subln_w.reshape(1, HEAD_WIDTH))


def _tail_kernel(attn_ref, u_ref, halo_ref, ga_ref, gp_ref, x_ref, pw_ref, ps_ref, wba_ref, wbp_ref, wout_ref,
                 g_ref, b_ref, of_ref, ob_ref, sums_ref, *, tiles_per_seq):
    i = pl.program_id(0)
    tm = u_ref.shape[0]
    tile_in_seq = i % tiles_per_seq
    r0 = SUBLANES + MAX_WINDOW
    n = r0 + tm
    sums_ref[:, 0:SUBLANES, :] = jnp.zeros((sums_ref.shape[0], SUBLANES, POOL_WIDTH), F32)
    sums_ref[0, SUBLANES:r0, :] = jnp.where(tile_in_seq == 0, 0.0, halo_ref[...])
    sums_ref[0, r0:n, :] = u_ref[...]
    t = tile_in_seq * tm + lax.broadcasted_iota(jnp.int32, (tm, 1), 0)

    def doubled(k, lo, cs):
        shift = 2 ** k
        return sums_ref[k, lo:n, cs] + sums_ref[k, lo - shift:n - shift, cs]

    n_groups = len(POOL_WINDOWS)
    a_cols = D_MODEL // n_groups
    attn = attn_ref[...]

    def attn_branch_chunk(j):
        return jnp.dot(attn, wba_ref[:, j * a_cols:(j + 1) * a_cols], preferred_element_type=F32)

    a_chunks = [attn_branch_chunk(0)]
    for k in range(n_groups - 1):
        cs = slice((k + 1) * POOL_GROUP, POOL_WIDTH)
        sums_ref[k + 1, SUBLANES:n, cs] = doubled(k, SUBLANES, cs)

    pooled = []
    for g, w in enumerate(POOL_WINDOWS):
        if g > 0:
            a_chunks.append(attn_branch_chunk(g))
        cs = slice(g * POOL_GROUP, (g + 1) * POOL_GROUP)
        cur = sums_ref[0, r0:n, cs]
        win_sum = doubled(g, r0, cs)
        cnt = jnp.minimum(t + 1, w).astype(F32)
        z = (win_sum / cnt - cur).astype(BF16)
        y = jnp.dot(z, pw_ref[g], preferred_element_type=F32)
        pooled.append((y * ps_ref[:, cs]).astype(BF16))
    pool = jnp.concatenate(pooled, axis=-1)

    a = jnp.concatenate(a_chunks, axis=-1)
    p = jnp.dot(pool, wbp_ref[...], preferred_element_type=F32)
    merged = (ga_ref[...] * a + gp_ref[...] * p).astype(BF16)

    hm = tm // 2
    top, bottom = slice(0, hm), slice(hm, tm)
    o_cols = D_MODEL // 4

    def bottom_chunk(j):
        return jnp.dot(merged[bottom], wout_ref[:, j * o_cols:(j + 1) * o_cols], preferred_element_type=F32)

    def store(rows, out):
        of_ref[rows, :] = out
        ob_ref[rows, :] = out.astype(BF16)

    r_top = ALPHA * x_ref[top, :] + jnp.dot(merged[top], wout_ref[...], preferred_element_type=F32)
    chunks = [bottom_chunk(0)]
    centered = r_top - jnp.mean(r_top, axis=-1, keepdims=True)
    chunks.append(bottom_chunk(1))
    inv_std = lax.rsqrt(jnp.mean(centered * centered, axis=-1, keepdims=True) + LN_EPS)
    chunks.append(bottom_chunk(2))
    store(top, centered * inv_std * g_ref[...] + b_ref[...])
    chunks.append(bottom_chunk(3))
    r_bottom = ALPHA * x_ref[bottom, :] + jnp.concatenate(chunks, axis=-1)
    store(bottom, _layer_norm(r_bottom, g_ref[...], b_ref[...]))


def _resident(shape):
    zeros = (0,) * len(shape)
    return pl.BlockSpec(shape, lambda i: zeros, pipeline_mode=pl.Buffered(1))


def _mixer_tail(attn, u, gates, x, pool_w, pool_scale, w_ba, w_bp, w_out, ln_g, ln_b, seq):
    t = x.shape[0]
    tm = TAIL_TM
    assert seq % tm == 0 and tm % MAX_WINDOW == 0
    assert POOL_WINDOWS == tuple(2 ** (g + 1) for g in range(len(POOL_WINDOWS))) and MAX_WINDOW >= SUBLANES
    halo_blocks = tm // MAX_WINDOW
    kernel = functools.partial(_tail_kernel, tiles_per_seq=seq // tm)
    return pl.pallas_call(
        kernel,
        grid=(t // tm,),
        in_specs=[
            pl.BlockSpec((tm, ATTN_WIDTH), lambda i: (i, 0)),
            pl.BlockSpec((tm, POOL_WIDTH), lambda i: (i, 0)),
            pl.BlockSpec((MAX_WINDOW, POOL_WIDTH), lambda i: (jnp.maximum(i * halo_blocks - 1, 0), 0)),
            pl.BlockSpec((tm, D_MODEL), lambda i: (i, 0)),
            pl.BlockSpec((tm, D_MODEL), lambda i: (i, 1)),
            pl.BlockSpec((tm, D_MODEL), lambda i: (i, 0)),
            _resident(pool_w.shape),
            _resident((1, POOL_WIDTH)),
            _resident(w_ba.shape),
            _resident(w_bp.shape),
            _resident(w_out.shape),
            _resident((1, D_MODEL)),
            _resident((1, D_MODEL)),
        ],
        out_specs=[
            pl.BlockSpec((tm, D_MODEL), lambda i: (i, 0)),
            pl.BlockSpec((tm, D_MODEL), lambda i: (i, 0)),
        ],
        out_shape=[
            jax.ShapeDtypeStruct((t, D_MODEL), F32),
            jax.ShapeDtypeStruct((t, D_MODEL), BF16),
        ],
        scratch_shapes=[pltpu.VMEM((len(POOL_WINDOWS), SUBLANES + MAX_WINDOW + tm, POOL_WIDTH), F32)],
        compiler_params=_params("parallel"),
        name="mixer_tail",
    )(attn, u, u, gates, gates, x, pool_w, pool_scale.reshape(1, POOL_WIDTH), w_ba, w_bp, w_out,
      ln_g.reshape(1, D_MODEL), ln_b.reshape(1, D_MODEL))


def _swiglu(hb, wg, wu):
    g = jnp.dot(hb, wg, preferred_element_type=F32)
    u = jnp.dot(hb, wu, preferred_element_type=F32)
    return (g * jax.nn.sigmoid(g) * u).astype(BF16)


def _dense_ffn_kernel(hb_ref, hf_ref, wg_ref, wu_ref, wd_ref, g_ref, b_ref, of_ref, ob_ref, acc_ref):
    f = pl.program_id(1)

    @pl.when(f == 0)
    def _():
        acc_ref[...] = jnp.zeros(acc_ref.shape, F32)

    a = _swiglu(hb_ref[...], wg_ref[...], wu_ref[...])
    acc_ref[...] += jnp.dot(a, wd_ref[...], preferred_element_type=F32)

    @pl.when(f == pl.num_programs(1) - 1)
    def _():
        out = _layer_norm(ALPHA * hf_ref[...] + acc_ref[...], g_ref[...], b_ref[...])
        of_ref[...] = out
        ob_ref[...] = out.astype(BF16)


def _dense_ffn(hb, hf, wg, wu, wd, ln_g, ln_b):
    t = hb.shape[0]
    d_ff = wg.shape[1]
    tm, tf = FFN_TM, FFN_TF
    assert d_ff % tf == 0
    return pl.pallas_call(
        _dense_ffn_kernel,
        grid=(t // tm, d_ff // tf),
        in_specs=[
            pl.BlockSpec((tm, D_MODEL), lambda i, f: (i, 0)),
            pl.BlockSpec((tm, D_MODEL), lambda i, f: (i, 0)),
            pl.BlockSpec((D_MODEL, tf), lambda i, f: (0, f)),
            pl.BlockSpec((D_MODEL, tf), lambda i, f: (0, f)),
            pl.BlockSpec((tf, D_MODEL), lambda i, f: (f, 0)),
            pl.BlockSpec((1, D_MODEL), lambda i, f: (0, 0)),
            pl.BlockSpec((1, D_MODEL), lambda i, f: (0, 0)),
        ],
        out_specs=[
            pl.BlockSpec((tm, D_MODEL), lambda i, f: (i, 0)),
            pl.BlockSpec((tm, D_MODEL), lambda i, f: (i, 0)),
        ],
        out_shape=[
            jax.ShapeDtypeStruct((t, D_MODEL), F32),
            jax.ShapeDtypeStruct((t, D_MODEL), BF16),
        ],
        scratch_shapes=[pltpu.VMEM((tm, D_MODEL), F32)],
        compiler_params=_params("parallel", "arbitrary"),
        name="dense_ffn",
    )(hb, hf, wg, wu, wd, ln_g.reshape(1, D_MODEL), ln_b.reshape(1, D_MODEL))


ROUTE_W0, ROUTE_W1, ROUTE_E0, ROUTE_E1, ROUTE_R0, ROUTE_R1 = range(6)


def _router_kernel(h_ref, rw_ref, route_ref, counts_ref, carry_ref):
    i = pl.program_id(0)
    tm = h_ref.shape[0]

    @pl.when(i == 0)
    def _():
        carry_ref[...] = jnp.zeros(carry_ref.shape, F32)

    logits = jnp.dot(h_ref[...], rw_ref[...], preferred_element_type=F32)
    lane = lax.broadcasted_iota(jnp.int32, (tm, LANES), 1)
    lg = jnp.where(lane < N_EXPERTS, logits, -jnp.inf)
    m1 = jnp.max(lg, axis=-1, keepdims=True)
    e1 = jnp.min(jnp.where(lg == m1, lane, LANES), axis=-1, keepdims=True)
    lg2 = jnp.where(lane == e1, -jnp.inf, lg)
    m2 = jnp.max(lg2, axis=-1, keepdims=True)
    e2 = jnp.min(jnp.where(lg2 == m2, lane, LANES), axis=-1, keepdims=True)
    x2 = jnp.exp(m2 - m1)
    w1 = 1.0 / (1.0 + x2)
    w2 = x2 / (1.0 + x2)

    sel1 = lane == e1
    sel2 = lane == e2
    mask = jnp.where(sel1 | sel2, 1.0, 0.0)
    r = lax.broadcasted_iota(jnp.int32, (tm, tm), 0)
    c = lax.broadcasted_iota(jnp.int32, (tm, tm), 1)
    tri = jnp.where(r >= c, 1.0, 0.0).astype(BF16)
    incl = jnp.dot(tri, mask.astype(BF16), preferred_element_type=F32)
    rank = incl - mask + carry_ref[...]
    total = carry_ref[...] + jnp.sum(mask, axis=0, keepdims=True)
    carry_ref[...] = total
    counts_ref[...] = total

    r1 = jnp.sum(jnp.where(sel1, rank, 0.0), axis=-1, keepdims=True)
    r2 = jnp.sum(jnp.where(sel2, rank, 0.0), axis=-1, keepdims=True)
    cols = (w1, w2, e1.astype(F32), e2.astype(F32), r1, r2)
    route = jnp.zeros((tm, LANES), F32)
    for idx, val in enumerate(cols):
        route = jnp.where(lane == idx, val, route)
    route_ref[...] = route


def _router(hb, router_w):
    t = hb.shape[0]
    tm = ROUTER_TM
    rw = jnp.pad(router_w, ((0, 0), (0, LANES - N_EXPERTS))).astype(BF16)
    return pl.pallas_call(
        _router_kernel,
        grid=(t // tm,),
        in_specs=[
            pl.BlockSpec((tm, D_MODEL), lambda i: (i, 0)),
            pl.BlockSpec((D_MODEL, LANES), lambda i: (0, 0)),
        ],
        out_specs=[
            pl.BlockSpec((tm, LANES), lambda i: (i, 0)),
            pl.BlockSpec((1, LANES), lambda i: (0, 0)),
        ],
        out_shape=[
            jax.ShapeDtypeStruct((t, LANES), F32),
            jax.ShapeDtypeStruct((1, LANES), F32),
        ],
        scratch_shapes=[pltpu.VMEM((1, LANES), F32)],
        compiler_params=_params("arbitrary"),
        name="moe_router",
    )(hb, rw)


def _dispatch_kernel(pad_start_ref, pad_count_ref, n_used_ref, dest_ref, h_ref, xs_hbm, zero_ref, sem, pad_sem):
    n = h_ref.shape[0]
    tile_rows = zero_ref.shape[0]
    n_tiles = xs_hbm.shape[0] // tile_rows

    @pl.when(pl.program_id(0) == 0)
    def _():
        zero_ref[...] = jnp.zeros(zero_ref.shape, F32)
        for e in range(N_EXPERTS):
            def pad_copy(j, e=e):
                return pltpu.make_async_copy(
                    zero_ref.at[pl.ds(0, 1)], xs_hbm.at[pl.ds(pad_start_ref[e] + j, 1)], pad_sem)

            lax.fori_loop(0, pad_count_ref[e], lambda j, c: (pad_copy(j).start(), c)[1], 0)
            lax.fori_loop(0, pad_count_ref[e], lambda j, c: (pad_copy(j).wait(), c)[1], 0)

        def tile_copy(j):
            return pltpu.make_async_copy(
                zero_ref, xs_hbm.at[pl.ds(pl.multiple_of(j * tile_rows, tile_rows), tile_rows)], pad_sem)

        lax.fori_loop(n_used_ref[0], n_tiles, lambda j, c: (tile_copy(j).start(), c)[1], 0)
        lax.fori_loop(n_used_ref[0], n_tiles, lambda j, c: (tile_copy(j).wait(), c)[1], 0)

    def row_copy(r, k):
        return pltpu.make_async_copy(
            h_ref.at[pl.ds(r, 1)], xs_hbm.at[pl.ds(dest_ref[0, 0, TOP_K * r + k], 1)], sem)

    def start(r, carry):
        for k in range(TOP_K):
            row_copy(r, k).start(priority=k)
        return carry

    def wait(r, carry):
        for k in range(TOP_K):
            row_copy(r, k).wait()
        return carry

    lax.fori_loop(0, n, start, 0, unroll=ROW_DMA_UNROLL)
    lax.fori_loop(0, n, wait, 0, unroll=ROW_DMA_UNROLL)


def _dispatch(hf, dest, pad_start, pad_count, n_used, n_tiles):
    t = hf.shape[0]
    tm = DISPATCH_TM
    return pl.pallas_call(
        _dispatch_kernel,
        grid_spec=pltpu.PrefetchScalarGridSpec(
            num_scalar_prefetch=3,
            grid=(t // tm,),
            in_specs=[
                pl.BlockSpec((1, 1, TOP_K * tm), lambda i, ps, pc, nu: (i, 0, 0), memory_space=pltpu.SMEM),
                pl.BlockSpec((tm, D_MODEL), lambda i, ps, pc, nu: (i, 0)),
            ],
            out_specs=pl.BlockSpec(memory_space=pl.ANY),
            scratch_shapes=[
                pltpu.VMEM((MOE_TM, D_MODEL), F32),
                pltpu.SemaphoreType.DMA(()),
                pltpu.SemaphoreType.DMA(()),
            ],
        ),
        out_shape=jax.ShapeDtypeStruct((n_tiles * MOE_TM, D_MODEL), F32),
        compiler_params=_params("arbitrary"),
        name="moe_dispatch",
    )(pad_start, pad_count, n_used, dest.reshape(t // tm, 1, TOP_K * tm), hf)


def _moe_ffn_kernel(te_ref, ok_ref, src_ref, xs_ref, wg_ref, wu_ref, wd_ref, ys_ref, xb_ref, acc_ref):
    del te_ref, src_ref
    i = pl.program_id(0)
    f = pl.program_id(1)
    ok = ok_ref[i] == 1

    @pl.when(ok & (f == 0))
    def _():
        xb_ref[...] = xs_ref[...].astype(BF16)
        acc_ref[...] = jnp.zeros(acc_ref.shape, F32)

    @pl.when(ok)
    def _():
        a = _swiglu(xb_ref[...], wg_ref[...], wu_ref[...])
        acc_ref[...] += jnp.dot(a, wd_ref[...], preferred_element_type=F32)

    @pl.when(f == pl.num_programs(1) - 1)
    def _():
        ys_ref[...] = jnp.where(ok, acc_ref[...], 0.0)


def _moe_ffn(xs, tile_expert, tile_ok, tile_src, wg, wu, wd):
    n_rows = xs.shape[0]
    d_ff = wg.shape[2]
    tm, tf = MOE_TM, MOE_TF
    assert d_ff % tf == 0
    n_f = d_ff // tf
    n_tiles = n_rows // tm

    def f_idx(i, f, ok):
        return jnp.where(ok[i] == 1, f, n_f - 1)

    return pl.pallas_call(
        _moe_ffn_kernel,
        grid_spec=pltpu.PrefetchScalarGridSpec(
            num_scalar_prefetch=3,
            grid=(n_tiles, n_f),
            in_specs=[
                pl.BlockSpec((tm, D_MODEL), lambda i, f, te, ok, src: (src[i], 0)),
                pl.BlockSpec((None, D_MODEL, tf), lambda i, f, te, ok, src: (te[i], 0, f_idx(i, f, ok))),
                pl.BlockSpec((None, D_MODEL, tf), lambda i, f, te, ok, src: (te[i], 0, f_idx(i, f, ok))),
                pl.BlockSpec((None, tf, D_MODEL), lambda i, f, te, ok, src: (te[i], f_idx(i, f, ok), 0)),
            ],
            out_specs=pl.BlockSpec((tm, D_MODEL), lambda i, f, te, ok, src: (i, 0)),
            scratch_shapes=[pltpu.VMEM((tm, D_MODEL), BF16), pltpu.VMEM((tm, D_MODEL), F32)],
        ),
        out_shape=jax.ShapeDtypeStruct((n_rows, D_MODEL), F32),
        compiler_params=_params("arbitrary", "arbitrary"),
        name="moe_ffn",
    )(tile_expert, tile_ok, tile_src, xs, wg, wu, wd)


def _combine_kernel(dest_ref, h_ref, route_ref, ys_hbm, g_ref, b_ref, o_ref, y_ref, sem):
    tm = h_ref.shape[0]

    def row_copy(r, k):
        return pltpu.make_async_copy(
            ys_hbm.at[pl.ds(dest_ref[0, 0, TOP_K * r + k], 1)], y_ref.at[k, pl.ds(r, 1)], sem)

    def start(r, carry):
        for k in range(TOP_K):
            row_copy(r, k).start(priority=k)
        return carry

    def wait(r, carry):
        for k in range(TOP_K):
            row_copy(r, k).wait()
        return carry

    lax.fori_loop(0, tm, start, 0, unroll=ROW_DMA_UNROLL)
    lax.fori_loop(0, tm, wait, 0, unroll=ROW_DMA_UNROLL)

    route = route_ref[...]
    w1 = route[:, ROUTE_W0:ROUTE_W0 + 1]
    w2 = route[:, ROUTE_W1:ROUTE_W1 + 1]
    y = w1 * y_ref[0] + w2 * y_ref[1]
    o_ref[...] = _layer_norm(ALPHA * h_ref[...] + y, g_ref[...], b_ref[...])


def _combine(hf, route, dest, ys, ln_g, ln_b):
    t = hf.shape[0]
    tm = COMBINE_TM
    return pl.pallas_call(
        _combine_kernel,
        grid=(t // tm,),
        in_specs=[
            pl.BlockSpec((1, 1, TOP_K * tm), lambda i: (i, 0, 0), memory_space=pltpu.SMEM),
            pl.BlockSpec((tm, D_MODEL), lambda i: (i, 0)),
            pl.BlockSpec((tm, LANES), lambda i: (i, 0)),
            pl.BlockSpec(memory_space=pl.ANY),
            pl.BlockSpec((1, D_MODEL), lambda i: (0, 0)),
            pl.BlockSpec((1, D_MODEL), lambda i: (0, 0)),
        ],
        out_specs=pl.BlockSpec((tm, D_MODEL), lambda i: (i, 0)),
        out_shape=jax.ShapeDtypeStruct((t, D_MODEL), F32),
        scratch_shapes=[pltpu.VMEM((TOP_K, tm, D_MODEL), F32), pltpu.SemaphoreType.DMA(())],
        compiler_params=_params("arbitrary"),
        name="moe_combine",
    )(dest.reshape(t // tm, 1, TOP_K * tm), hf, route, ys, ln_g.reshape(1, D_MODEL), ln_b.reshape(1, D_MODEL))


def _moe(hf, hb, router_w, wg, wu, wd, ln_g, ln_b):
    t = hf.shape[0]
    tm = MOE_TM
    n_tiles = (TOP_K * t) // tm + N_EXPERTS
    route, counts = _router(hb, router_w)

    counts = counts[0, :N_EXPERTS].astype(jnp.int32)
    tiles_per_expert = (counts + tm - 1) // tm
    tile_end = jnp.cumsum(tiles_per_expert)
    group_start = (tile_end - tiles_per_expert) * tm
    tile_ids = jnp.arange(n_tiles, dtype=jnp.int32)
    n_used = tile_end[-1]
    last_used = jnp.maximum(n_used - 1, 0)
    expert_of_tile = jnp.sum(tile_ids[:, None] >= tile_end[None, :], axis=1).astype(jnp.int32)
    tile_ok = (tile_ids < n_used).astype(jnp.int32)
    tile_expert = jnp.where(tile_ok == 1, expert_of_tile, expert_of_tile[last_used])
    tile_expert = jnp.minimum(tile_expert, N_EXPERTS - 1)
    tile_src = jnp.where(tile_ok == 1, tile_ids, last_used)

    experts = route[:, ROUTE_E0:ROUTE_E1 + 1].astype(jnp.int32)
    ranks = route[:, ROUTE_R0:ROUTE_R1 + 1].astype(jnp.int32)
    dest = (group_start[experts] + ranks).reshape(-1)

    xs = _dispatch(hf, dest, group_start + counts, tiles_per_expert * tm - counts, n_used.reshape(1), n_tiles)
    ys = _moe_ffn(xs, tile_expert, tile_ok, tile_src, wg, wu, wd)
    return _combine(hf, route, dest, ys, ln_g, ln_b)


def kernel(x, w_in, lambdas, subln_w, pool_w, pool_scale, w_branch_attn, w_branch_pool, w_out, rel_bias,
           ln1_g, ln1_b, dense_w_gate, dense_w_up, dense_w_down, router_w, moe_w_gate, moe_w_up, moe_w_down,
           ln2_g, ln2_b):
    b, s, d = x.shape
    t = b * s
    xf = x.reshape(t, d)
    xin = xf
    bias_tiles = _bias_tiles(rel_bias)
    for l in range(DEPTH):
        lambda_init = 0.8 - 0.6 * math.exp(-0.3 * l)
        qkv, u, gates = _in_proj(xin, w_in[l].astype(BF16))
        attn = _attention(qkv.reshape(b, s, 3 * ATTN_WIDTH), bias_tiles, lambdas[l], subln_w[l], lambda_init)
        hf, hb = _mixer_tail(attn.reshape(t, ATTN_WIDTH), u, gates, xf, pool_w[l].astype(BF16), pool_scale[l],
                             w_branch_attn[l].astype(BF16), w_branch_pool[l].astype(BF16), w_out[l].astype(BF16),
                             ln1_g[l], ln1_b[l], s)
        if l % 2 == 0:
            xf, xin = _dense_ffn(hb, hf, dense_w_gate[l // 2].astype(BF16), dense_w_up[l // 2].astype(BF16),
                                 dense_w_down[l // 2].astype(BF16), ln2_g[l], ln2_b[l])
        else:
            xf = _moe(hf, hb, router_w[l // 2], moe_w_gate[l // 2].astype(BF16), moe_w_up[l // 2].astype(BF16),
                      moe_w_down[l // 2].astype(BF16), ln2_g[l], ln2_b[l])
            xin = xf
    return xf.reshape(b, s, d)
```
